```python
import math
import jax
import jax.numpy as jnp
from jax import lax
import numpy as np


D_MODEL = 2048
BATCH = 2
SEQ = 4096
DEPTH = 2
DEC_BATCH = 128
DEC_SEQ = 8
PAST_LEN = 2048
PAGE_SIZE = 128

N_HEADS = 16
HEAD_DIM = D_MODEL // N_HEADS
N_KV = 4
HG = N_HEADS // N_KV
L_CMP = 32
STRIDE = 16
CMP_HID = HEAD_DIM
L_SLC = 64
N_SEL = 16
WINDOW = 512
Q_BLOCK = 128
CONV_W = 31
D_FF = -(-8 * D_MODEL // (3 * 256)) * 256
N_BUCKETS = 32
MAX_DIST = 128
N_A = DEPTH // 2
N_B = DEPTH - N_A
ALPHA = (2 * DEPTH) ** 0.25
BETA = (8 * DEPTH) ** -0.25
LN_EPS = 1e-5
NEG = -1e30
FORCE_BONUS = 1e4
NO_WINDOW = 2 ** 30

kernel_name = 'hybrid_conformer_nsa_yoco_step'


def layer_norm(x, g, b):
    xf = x.astype(jnp.float32)
    mu = xf.mean(-1, keepdims=True)
    var = jnp.square(xf - mu).mean(-1, keepdims=True)
    return ((xf - mu) * lax.rsqrt(var + LN_EPS) * g.astype(jnp.float32) + b.astype(jnp.float32)).astype(x.dtype)


def masked_softmax(s, mask):
    s = jnp.where(mask, s, NEG)
    m = jnp.max(s, axis=-1, keepdims=True)
    e = jnp.where(mask, jnp.exp(s - m), 0.0)
    return e / jnp.maximum(e.sum(-1, keepdims=True), 1e-30)


def t5_bucket(dist):
    n = jnp.maximum(dist, 0)
    max_exact = N_BUCKETS // 2
    nf = jnp.maximum(n, 1).astype(jnp.float32)
    large = max_exact + (jnp.log(nf / max_exact) / math.log(MAX_DIST / max_exact) * (N_BUCKETS - max_exact)).astype(jnp.int32)
    large = jnp.minimum(large, N_BUCKETS - 1)
    return jnp.where(n < max_exact, n, large)


def attend(q, k, v, qpos, kpos, window, rel_bias):
    bq, tq = q.shape[:2]
    qg = q.reshape(bq, tq, N_KV, HG, HEAD_DIM)
    s = jnp.einsum('btghd,bsgd->bghts', qg, k, preferred_element_type=jnp.float32) * (HEAD_DIM ** -0.5)
    dist = qpos[:, :, None] - kpos[:, None, :]
    bias = rel_bias.astype(jnp.float32)[t5_bucket(dist)]
    bias = bias.reshape(dist.shape + (N_KV, HG)).transpose(0, 3, 4, 1, 2)
    mask = ((dist >= 0) & (dist < window) & (kpos[:, None, :] >= 0))[:, None, None]
    p = masked_softmax(s + bias, mask)
    o = jnp.einsum('bghts,bsgd->btghd', p.astype(v.dtype), v)
    return o.reshape(bq, tq, N_HEADS, HEAD_DIM), p


def compress(rows, pe, w1, b1, w2, b2):
    bc, t = rows.shape[:2]
    r_n = L_CMP // STRIDE
    nc = (t - L_CMP) // STRIDE + 1
    nch = nc + r_n - 1
    chunks = rows[:, :nch * STRIDE].reshape(bc, nch, STRIDE, N_KV, HEAD_DIM)
    chunks = chunks.transpose(0, 1, 3, 2, 4).reshape(bc, nch, N_KV, STRIDE * HEAD_DIM)
    w1r = w1.reshape(r_n, STRIDE * HEAD_DIM, CMP_HID)
    h = sum(jnp.einsum('bcgf,fe->bcge', chunks[:, r:r + nc], w1r[r]) for r in range(r_n))
    h = h + (pe.reshape(-1) @ w1 + b1)
    return jax.nn.gelu(h) @ w2 + b2


def cmp_to_slc(nc, ns):
    cs = jnp.arange(nc)[:, None] * STRIDE
    js = jnp.arange(ns)[None, :] * L_SLC
    ov = jnp.clip(jnp.minimum(cs + L_CMP, js + L_SLC) - jnp.maximum(cs, js), 0, None)
    return ov.astype(jnp.float32) / L_CMP


def select_blocks(p_cmp, qpos, ns):
    imp = jnp.einsum('bghtc,cj->bgtj', p_cmp, cmp_to_slc(p_cmp.shape[-1], ns))
    j = jnp.arange(ns)
    cur = (qpos // L_SLC)[:, None]
    forced = (j == 0) | (j == cur) | (j == cur - 1)
    score = jnp.where(j <= cur, imp + jnp.where(forced, FORCE_BONUS, 0.0), NEG)
    _, idx = lax.top_k(score, min(N_SEL, ns))
    return idx


def select_attend(q, k, v, qpos, idx, rel_bias):
    bq, tq = q.shape[:2]
    ns = k.shape[1] // L_SLC
    n = idx.shape[-1]
    tc = tq if tq <= Q_BLOCK else Q_BLOCK
    nq = tq // tc
    kb = k.reshape(bq, ns, L_SLC, N_KV, HEAD_DIM).transpose(0, 3, 1, 2, 4)
    vb = v.reshape(bq, ns, L_SLC, N_KV, HEAD_DIM).transpose(0, 3, 1, 2, 4)
    qc = q.reshape(bq * nq, tc, N_KV, HG, HEAD_DIM)
    ic = idx.reshape(bq, N_KV, nq, tc, n).transpose(0, 2, 1, 3, 4).reshape(bq * nq, N_KV, tc, n)
    pc = jnp.tile(qpos.reshape(nq, tc), (bq, 1))
    bc = jnp.repeat(jnp.arange(bq), nq)
    g_ar = jnp.arange(N_KV)[:, None, None]
    tab = rel_bias.astype(jnp.float32).reshape(N_BUCKETS, N_KV, HG)
    scale = HEAD_DIM ** -0.5

    def one(args):
        b, qb, ib, pb = args
        kg = kb[b][g_ar, ib]
        vg = vb[b][g_ar, ib]
        kp = ib[..., None] * L_SLC + jnp.arange(L_SLC)
        dist = pb[None, :, None, None] - kp
        s = jnp.einsum('tghd,gtnld->ghtnl', qb, kg, preferred_element_type=jnp.float32) * scale
        bias = tab[t5_bucket(dist), g_ar[..., None]].transpose(0, 4, 1, 2, 3)
        s = (s + bias).reshape(N_KV, HG, tc, n * L_SLC)
        mask = (dist >= 0).reshape(N_KV, 1, tc, n * L_SLC)
        p = masked_softmax(s, mask).reshape(N_KV, HG, tc, n, L_SLC)
        return jnp.einsum('ghtnl,gtnld->tghd', p.astype(vg.dtype), vg)

    o = lax.map(one, (bc, qc, ic, pc))
    return o.reshape(bq, tq, N_HEADS, HEAD_DIM)


def window_attend_banded(q, k, v, rel_bias):
    bq, t = q.shape[:2]
    nb = t // Q_BLOCK
    span = WINDOW + Q_BLOCK
    band = jnp.arange(nb)[:, None] * Q_BLOCK + jnp.arange(span)[None, :]
    padw = ((0, 0), (WINDOW, 0), (0, 0), (0, 0))
    kband = jnp.pad(k, padw)[:, band].reshape(bq * nb, span, N_KV, HEAD_DIM)
    vband = jnp.pad(v, padw)[:, band].reshape(bq * nb, span, N_KV, HEAD_DIM)
    qb = q.reshape(bq * nb, Q_BLOCK, N_HEADS, HEAD_DIM)
    qpos = jnp.tile(jnp.arange(t, dtype=jnp.int32).reshape(nb, Q_BLOCK), (bq, 1))
    kpos = jnp.tile((band - WINDOW).astype(jnp.int32), (bq, 1))
    o, _ = attend(qb, kband, vband, qpos, kpos, WINDOW, rel_bias)
    return o.reshape(bq, t, N_HEADS, HEAD_DIM)


def conv_mixer(x, state, w_pw1, b_pw1, w_dw, b_dw, ln_g, ln_b, w_pw2, b_pw2):
    a, gate = jnp.split(x @ w_pw1 + b_pw1, 2, axis=-1)
    u = a * jax.nn.sigmoid(gate)
    ucat = jnp.concatenate([state.astype(u.dtype), u], axis=1)
    y = lax.conv_general_dilated(ucat, w_dw[:, None, :].astype(u.dtype), (1,), 'VALID',
                                 dimension_numbers=('NWC', 'WIO', 'NWC'),
                                 feature_group_count=D_MODEL) + b_dw
    y = jax.nn.silu(layer_norm(y, ln_g, ln_b))
    return y @ w_pw2 + b_pw2, ucat[:, -(CONV_W - 1):]


def swiglu(x, w_in, w_out):
    a, b = jnp.split(x @ w_in, 2, axis=-1)
    return (jax.nn.silu(a) * b) @ w_out


def nsa_mixer(x, qpos, ctx, w_qg, w_o, rel_bias, banded):
    kc, vc, ckpos, ks, vs, kw, vw, wkpos = ctx
    bq, t, _ = x.shape
    qg = x @ w_qg
    q = qg[..., :N_HEADS * HEAD_DIM].reshape(bq, t, N_HEADS, HEAD_DIM)
    gates = jax.nn.sigmoid(qg[..., N_HEADS * HEAD_DIM:].astype(jnp.float32)).reshape(bq, t, N_HEADS, 3)
    o_cmp, p_cmp = attend(q, kc, vc, qpos[None], ckpos[None], NO_WINDOW, rel_bias)
    idx = select_blocks(p_cmp, qpos, ks.shape[1] // L_SLC)
    o_slc = select_attend(q, ks, vs, qpos, idx, rel_bias)
    if banded:
        o_win = window_attend_banded(q, kw, vw, rel_bias)
    else:
        o_win, _ = attend(q, kw, vw, qpos[None], wkpos[None], WINDOW, rel_bias)
    o = gates[..., 0:1] * o_cmp + gates[..., 1:2] * o_slc + gates[..., 2:3] * o_win
    return o.astype(x.dtype).reshape(bq, t, N_HEADS * HEAD_DIM) @ w_o


def gather_pages(pool, page_table):
    return pool[page_table].reshape(page_table.shape[0], -1, N_KV, HEAD_DIM)


def pad_to_block(r):
    pad = (-r.shape[1]) % L_SLC
    return jnp.pad(r, ((0, 0), (0, pad), (0, 0), (0, 0)))


def setup_inputs(seed: int = 0) -> dict:
    key = jax.random.key(seed)
    keys = iter(jax.random.split(key, 48))
    f32 = jnp.float32

    def nrm(shape, scale):
        return jax.random.normal(next(keys), shape, f32) * scale

    n_pages = PAST_LEN // PAGE_SIZE
    n_used = DEC_BATCH * n_pages
    n_pool = n_used + (n_used + 3) // 4
    perm = jax.random.permutation(next(keys), n_pool)
    page_table = perm[:n_used].reshape(DEC_BATCH, n_pages).astype(jnp.int32)
    wb = min(WINDOW, PAST_LEN)
    d = D_MODEL
    hq = N_HEADS * HEAD_DIM
    return {
        'x_prompt': nrm((BATCH, SEQ, d), 1.0),
        'x_sample': nrm((DEC_BATCH, DEC_SEQ, d), 1.0),
        'cache_k_cmp': nrm((n_pool, PAGE_SIZE, N_KV, HEAD_DIM), 1.0),
        'cache_v_cmp': nrm((n_pool, PAGE_SIZE, N_KV, HEAD_DIM), 1.0),
        'cache_k_slc': nrm((n_pool, PAGE_SIZE, N_KV, HEAD_DIM), 1.0),
        'cache_v_slc': nrm((n_pool, PAGE_SIZE, N_KV, HEAD_DIM), 1.0),
        'state_k_win': nrm((DEC_BATCH, wb, N_KV, HEAD_DIM), 1.0),
        'state_v_win': nrm((DEC_BATCH, wb, N_KV, HEAD_DIM), 1.0),
        'state_conv': nrm((N_A, DEC_BATCH, CONV_W - 1, d), 0.5),
        'page_table': page_table,
        'rel_bias': nrm((N_BUCKETS, N_HEADS), 0.5),
        'conv_w_pw1': nrm((N_A, d, 2 * d), d ** -0.5),
        'conv_b_pw1': nrm((N_A, 2 * d), 0.01),
        'conv_w_dw': nrm((N_A, CONV_W, d), CONV_W ** -0.5),
        'conv_b_dw': nrm((N_A, d), 0.01),
        'conv_ln_g': 1.0 + nrm((N_A, d), 0.01),
        'conv_ln_b': nrm((N_A, d), 0.01),
        'conv_w_pw2': nrm((N_A, d, d), BETA * d ** -0.5),
        'conv_b_pw2': nrm((N_A, d), 0.01),
        'nsa_w_qg': nrm((N_B, d, hq + 3 * N_HEADS), d ** -0.5),
        'nsa_w_o': nrm((N_B, hq, d), BETA * hq ** -0.5),
        'w_kv': nrm((d, 6 * N_KV * HEAD_DIM), d ** -0.5),
        'cmp_pe': nrm((2, L_CMP, HEAD_DIM), 0.5),
        'cmp_w1': nrm((2, L_CMP * HEAD_DIM, CMP_HID), (L_CMP * HEAD_DIM) ** -0.5),
        'cmp_b1': nrm((2, CMP_HID), 0.01),
        'cmp_w2': nrm((2, CMP_HID, HEAD_DIM), CMP_HID ** -0.5),
        'cmp_b2': nrm((2, HEAD_DIM), 0.01),
        'ffn_w_in': nrm((DEPTH, d, 2 * D_FF), d ** -0.5),
        'ffn_w_out': nrm((DEPTH, D_FF, d), BETA * D_FF ** -0.5),
        'ln_mix_g': 1.0 + nrm((DEPTH, d), 0.01),
        'ln_mix_b': nrm((DEPTH, d), 0.01),
        'ln_ffn_g': 1.0 + nrm((DEPTH, d), 0.01),
        'ln_ffn_b': nrm((DEPTH, d), 0.01),
    }


def reference(x_prompt, x_sample, cache_k_cmp, cache_v_cmp, cache_k_slc, cache_v_slc,
              state_k_win, state_v_win, state_conv, page_table,
              rel_bias, conv_w_pw1, conv_b_pw1, conv_w_dw, conv_b_dw, conv_ln_g, conv_ln_b,
              conv_w_pw2, conv_b_pw2, nsa_w_qg, nsa_w_o, w_kv,
              cmp_pe, cmp_w1, cmp_b1, cmp_w2, cmp_b2,
              ffn_w_in, ffn_w_out, ln_mix_g, ln_mix_b, ln_ffn_g, ln_ffn_b):

    def project_kv(h):
        kv = (h @ w_kv).reshape(h.shape[0], h.shape[1], 6, N_KV, HEAD_DIM)
        return [kv[:, :, i] for i in range(6)]

    def cmp_ctx(kc_rows, vc_rows):
        kc = compress(kc_rows, cmp_pe[0], cmp_w1[0], cmp_b1[0], cmp_w2[0], cmp_b2[0])
        vc = compress(vc_rows, cmp_pe[1], cmp_w1[1], cmp_b1[1], cmp_w2[1], cmp_b2[1])
        ckpos = jnp.arange(kc.shape[1], dtype=jnp.int32) * STRIDE + (L_CMP - 1)
        return kc, vc, ckpos

    def prompt_kv(h):
        kcr, vcr, ksr, vsr, kwr, vwr = project_kv(h)
        kc, vc, ckpos = cmp_ctx(kcr, vcr)
        t = h.shape[1]
        wb = min(WINDOW, t)
        ctx = (kc, vc, ckpos, pad_to_block(ksr), pad_to_block(vsr), kwr, vwr, None)
        return ctx, (kcr, vcr, ksr, vsr, kwr[:, t - wb:], vwr[:, t - wb:])

    def sample_kv(h):
        kcr, vcr, ksr, vsr, kwr, vwr = project_kv(h)
        kc_all = jnp.concatenate([gather_pages(cache_k_cmp, page_table).astype(kcr.dtype), kcr], axis=1)
        vc_all = jnp.concatenate([gather_pages(cache_v_cmp, page_table).astype(vcr.dtype), vcr], axis=1)
        ks_all = jnp.concatenate([gather_pages(cache_k_slc, page_table).astype(ksr.dtype), ksr], axis=1)
        vs_all = jnp.concatenate([gather_pages(cache_v_slc, page_table).astype(vsr.dtype), vsr], axis=1)
        kc, vc, ckpos = cmp_ctx(kc_all, vc_all)
        kw_all = jnp.concatenate([state_k_win.astype(kwr.dtype), kwr], axis=1)
        vw_all = jnp.concatenate([state_v_win.astype(vwr.dtype), vwr], axis=1)
        wb_old = state_k_win.shape[1]
        wkpos = PAST_LEN - wb_old + jnp.arange(wb_old + h.shape[1], dtype=jnp.int32)
        wb_new = min(WINDOW, wb_old + h.shape[1])
        ctx = (kc, vc, ckpos, pad_to_block(ks_all), pad_to_block(vs_all), kw_all, vw_all, wkpos)
        return ctx, (kcr, vcr, ksr, vsr, kw_all[:, -wb_new:], vw_all[:, -wb_new:])

    def run_trunk(x, qpos, conv_state, build_kv, banded):
        new_conv = []
        ctx = None
        new_kv = None
        for l in range(DEPTH):
            if l < N_A:
                y, st = conv_mixer(x, conv_state[l], conv_w_pw1[l], conv_b_pw1[l], conv_w_dw[l], conv_b_dw[l],
                                   conv_ln_g[l], conv_ln_b[l], conv_w_pw2[l], conv_b_pw2[l])
                new_conv.append(st)
            else:
                j = l - N_A
                y = nsa_mixer(x, qpos, ctx, nsa_w_qg[j], nsa_w_o[j], rel_bias, banded)
            x = layer_norm(ALPHA * x + y, ln_mix_g[l], ln_mix_b[l])
            x = layer_norm(ALPHA * x + swiglu(x, ffn_w_in[l], ffn_w_out[l]), ln_ffn_g[l], ln_ffn_b[l])
            if l == N_A - 1:
                ctx, new_kv = build_kv(x)
        return x, jnp.stack(new_conv), new_kv

    conv_zero = jnp.zeros((N_A, x_prompt.shape[0], CONV_W - 1, D_MODEL), x_prompt.dtype)
    qpos_p = jnp.arange(x_prompt.shape[1], dtype=jnp.int32)
    y_prompt, p_conv, p_kv = run_trunk(x_prompt, qpos_p, conv_zero, prompt_kv, True)
    pk_cmp, pv_cmp, pk_slc, pv_slc, pk_win, pv_win = p_kv
    qpos_s = PAST_LEN + jnp.arange(x_sample.shape[1], dtype=jnp.int32)
    y_sample, s_conv, s_kv = run_trunk(x_sample, qpos_s, state_conv, sample_kv, False)
    sk_cmp, sv_cmp, sk_slc, sv_slc, sk_win, sv_win = s_kv
    return (y_prompt, y_sample, pk_cmp, pv_cmp, pk_slc, pv_slc, pk_win, pv_win, p_conv,
            sk_cmp, sv_cmp, sk_slc, sv_slc, sk_win, sv_win, s_conv)
```

```python
import functools
import math

import jax
import jax.numpy as jnp
from jax import lax
from jax.experimental import pallas as pl
from jax.experimental.pallas import tpu as pltpu

D_MODEL = 2048
PAST_LEN = 2048
PAGE_SIZE = 128
N_HEADS = 16
HEAD_DIM = D_MODEL // N_HEADS
N_KV = 4
HG = N_HEADS // N_KV
L_CMP = 32
STRIDE = 16
CMP_HID = HEAD_DIM
L_SLC = 64
N_SEL = 16
WINDOW = 512
Q_BLOCK = 128
CONV_W = 31
N_BUCKETS = 32
MAX_DIST = 128
DEPTH = 2
N_A = DEPTH // 2
ALPHA = (2 * DEPTH) ** 0.25
LN_EPS = 1e-5
NEG = -1e30
FORCE_BONUS = 1e4

KV_W = N_KV * HEAD_DIM
CHUNK_W = STRIDE * KV_W
VMEM_LIMIT = 56 * 1024 * 1024

F32 = jnp.float32
BF16 = jnp.bfloat16


def _params(*sem):
    return pltpu.CompilerParams(dimension_semantics=sem, vmem_limit_bytes=VMEM_LIMIT)


def _dot(a, b):
    return jnp.dot(a, b, preferred_element_type=F32)


def _dot_nt(a, b):
    return lax.dot_general(a, b, (((1,), (1,)), ((), ())), preferred_element_type=F32)


def _layer_norm(x, g, b):
    mu = jnp.mean(x, axis=-1, keepdims=True)
    xc = x - mu
    var = jnp.mean(xc * xc, axis=-1, keepdims=True)
    return xc * lax.rsqrt(var + LN_EPS) * g + b


def _silu(x):
    return x * jax.nn.sigmoid(x)


def _gelu_tanh(x):
    c = math.sqrt(2.0 / math.pi)
    return 0.5 * x * (1.0 + jnp.tanh(c * (x + 0.044715 * (x * x * x))))


def _mm_pair_kernel(x_ref, wa_ref, wb_ref, ba_ref, bb_ref, o_ref, *, act):
    x = x_ref[...]
    a = _dot(x, wa_ref[...]) + ba_ref[...]
    b = _dot(x, wb_ref[...]) + bb_ref[...]
    if act == "glu":
        o = a * jax.nn.sigmoid(b)
    else:
        o = _silu(a) * b
    o_ref[...] = o.astype(o_ref.dtype)


def _mm_pair(x, w, bias, *, act, out_dtype, tm, tn):
    m, k = x.shape
    n = w.shape[1] // 2
    nj = n // tn
    return pl.pallas_call(
        functools.partial(_mm_pair_kernel, act=act),
        grid=(m // tm, nj),
        in_specs=[
            pl.BlockSpec((tm, k), lambda i, j: (i, 0)),
            pl.BlockSpec((k, tn), lambda i, j: (0, j)),
            pl.BlockSpec((k, tn), lambda i, j: (0, j + nj)),
            pl.BlockSpec((1, tn), lambda i, j: (0, j)),
            pl.BlockSpec((1, tn), lambda i, j: (0, j + nj)),
        ],
        out_specs=pl.BlockSpec((tm, tn), lambda i, j: (i, j)),
        out_shape=jax.ShapeDtypeStruct((m, n), out_dtype),
        compiler_params=_params("parallel", "parallel"),
        name="mm_pair_" + act,
    )(x, w, w, bias, bias)


def _mm_res_ln_kernel(h_ref, w_ref, bias_ref, xres_ref, g_ref, b_ref, of_ref, ob_ref, acc_ref):
    kk = pl.program_id(1)

    @pl.when(kk == 0)
    def _():
        acc_ref[...] = jnp.zeros_like(acc_ref)

    acc_ref[...] += _dot(h_ref[...], w_ref[...])

    @pl.when(kk == pl.num_programs(1) - 1)
    def _():
        y = ALPHA * xres_ref[...] + (acc_ref[...] + bias_ref[...])
        o = _layer_norm(y, g_ref[...], b_ref[...])
        of_ref[...] = o
        ob_ref[...] = o.astype(BF16)


def _mm_res_ln(h, w, bias, xres, g, b, *, tm, tk):
    m, k = h.shape
    d = w.shape[1]
    row = lambda i, kk: (i, 0)
    vec = lambda i, kk: (0, 0)
    return pl.pallas_call(
        _mm_res_ln_kernel,
        grid=(m // tm, k // tk),
        in_specs=[
            pl.BlockSpec((tm, tk), lambda i, kk: (i, kk)),
            pl.BlockSpec((tk, d), lambda i, kk: (kk, 0)),
            pl.BlockSpec((1, d), vec),
            pl.BlockSpec((tm, d), row),
            pl.BlockSpec((1, d), vec),
            pl.BlockSpec((1, d), vec),
        ],
        out_specs=[pl.BlockSpec((tm, d), row), pl.BlockSpec((tm, d), row)],
        out_shape=[jax.ShapeDtypeStruct((m, d), F32), jax.ShapeDtypeStruct((m, d), BF16)],
        scratch_shapes=[pltpu.VMEM((tm, d), F32)],
        compiler_params=_params("parallel", "arbitrary"),
        name="mm_res_ln",
    )(h, w, bias, xres, g, b)


def _mm_kv_kernel(x_ref, w_ref, *out_refs):
    x = x_ref[...]
    for idx in range(6):
        acc = _dot(x, w_ref[:, idx * KV_W:(idx + 1) * KV_W])
        out_refs[idx][...] = acc
        out_refs[6 + idx][...] = acc.astype(BF16)


def _mm_kv(x, w, *, tm):
    m, k = x.shape
    row = lambda i: (i, 0)
    return pl.pallas_call(
        _mm_kv_kernel,
        grid=(m // tm,),
        in_specs=[pl.BlockSpec((tm, k), row), pl.BlockSpec((k, 6 * KV_W), lambda i: (0, 0))],
        out_specs=[pl.BlockSpec((tm, KV_W), row)] * 12,
        out_shape=[jax.ShapeDtypeStruct((m, KV_W), F32)] * 6 + [jax.ShapeDtypeStruct((m, KV_W), BF16)] * 6,
        compiler_params=_params("parallel"),
        name="mm_kv",
    )(x, w)


def _mm_qg_kernel(x_ref, w_ref, q_ref, g_ref):
    x = x_ref[...]
    scale = HEAD_DIM ** -0.5
    for c in range(D_MODEL // 512):
        acc = _dot(x, w_ref[:, c * 512:(c + 1) * 512])
        q_ref[:, c * 512:(c + 1) * 512] = (acc * scale).astype(BF16)
    g_ref[...] = _dot(x, w_ref[:, D_MODEL:])


def _mm_qg(x, w, *, tm):
    m, k = x.shape
    row = lambda i: (i, 0)
    return pl.pallas_call(
        _mm_qg_kernel,
        grid=(m // tm,),
        in_specs=[pl.BlockSpec((tm, k), row), pl.BlockSpec((k, D_MODEL + 128), lambda i: (0, 0))],
        out_specs=[pl.BlockSpec((tm, D_MODEL), row), pl.BlockSpec((tm, 128), row)],
        out_shape=[jax.ShapeDtypeStruct((m, D_MODEL), BF16), jax.ShapeDtypeStruct((m, 128), F32)],
        compiler_params=_params("parallel"),
        name="mm_qg",
    )(x, w)


CONV_HEAD = 32
CONV_RB = 32
CONV_CB = 512


def _conv_kernel(state_ref, uprev_ref, ucur_ref, w_ref, bdw_ref, g_ref, b_ref, o_ref, z_ref, y_ref, *, tm):
    i = pl.program_id(1)
    z_ref[0:CONV_HEAD, :] = jnp.where(i == 0, state_ref[0], uprev_ref[0])
    z_ref[CONV_HEAD:CONV_HEAD + tm, :] = ucur_ref[0]
    pad = CONV_HEAD - (CONV_W - 1)
    rb = min(CONV_RB, tm)
    for r0 in range(0, tm, rb):
        for c0 in range(0, D_MODEL, CONV_CB):
            acc = jnp.zeros((rb, CONV_CB), F32)
            for w in range(CONV_W):
                acc = acc + z_ref[r0 + pad + w:r0 + pad + w + rb, c0:c0 + CONV_CB] * w_ref[w:w + 1, c0:c0 + CONV_CB]
            y_ref[r0:r0 + rb, c0:c0 + CONV_CB] = acc
    y = _layer_norm(y_ref[...] + bdw_ref[...], g_ref[...], b_ref[...])
    o_ref[0] = _silu(y).astype(BF16)


def _conv_ln_silu(state32, u, w_dw, b_dw, g, b, *, tm):
    bsz, t, d = u.shape
    per = tm // CONV_HEAD if tm >= CONV_HEAD else 0
    uprev = u if t >= CONV_HEAD else state32
    vec = lambda bb, i: (0, 0)
    return pl.pallas_call(
        functools.partial(_conv_kernel, tm=tm),
        grid=(bsz, t // tm),
        in_specs=[
            pl.BlockSpec((1, CONV_HEAD, d), lambda bb, i: (bb, 0, 0)),
            pl.BlockSpec((1, CONV_HEAD, d), lambda bb, i: (bb, jnp.maximum(i * per - 1, 0), 0)),
            pl.BlockSpec((1, tm, d), lambda bb, i: (bb, i, 0)),
            pl.BlockSpec((CONV_HEAD, d), vec),
            pl.BlockSpec((1, d), vec),
            pl.BlockSpec((1, d), vec),
            pl.BlockSpec((1, d), vec),
        ],
        out_specs=pl.BlockSpec((1, tm, d), lambda bb, i: (bb, i, 0)),
        out_shape=jax.ShapeDtypeStruct((bsz, t, d), BF16),
        scratch_shapes=[pltpu.VMEM((CONV_HEAD + tm, d), F32), pltpu.VMEM((tm, d), F32)],
        compiler_params=_params("parallel", "parallel"),
        name="conv_ln_silu",
    )(state32, uprev, u, w_dw, b_dw, g, b)


def _compress_kernel(*refs, n_in):
    x_refs = refs[1:1 + n_in]
    wcat_ref, bias1_ref, w2_ref, b2_ref, o_ref = refs[1 + n_in:]
    x = jnp.concatenate([r[0] for r in x_refs], axis=0) if n_in > 1 else x_refs[0][0]
    nch = x.shape[0]
    xg = [jnp.concatenate([x[:, s * KV_W + g * HEAD_DIM:s * KV_W + (g + 1) * HEAD_DIM] for s in range(STRIDE)], axis=1)
          for g in range(N_KV)]
    xall = jnp.concatenate(xg, axis=0).astype(BF16)
    r = _dot(xall, wcat_ref[...])
    lo = r[:, :CMP_HID]
    hi = r[:, CMP_HID:]
    hi_next = pltpu.roll(hi, N_KV * nch - 1, 0)
    h = _gelu_tanh(lo + hi_next + bias1_ref[...])
    out = _dot(h.astype(BF16), w2_ref[...]) + b2_ref[...]
    for g in range(N_KV):
        o_ref[0, :, g * HEAD_DIM:(g + 1) * HEAD_DIM] = out[g * nch:(g + 1) * nch, :].astype(BF16)


def _compress(chunks, page_table, wcat, bias1, w2, b2, *, pages_per_row, ch_per_page):
    bsz = page_table.shape[0]
    nch = pages_per_row * ch_per_page
    const = lambda bb, pt: (0, 0)
    in_specs = [pl.BlockSpec((1, ch_per_page, CHUNK_W), functools.partial(lambda bb, pt, i: (pt[bb, i], 0, 0), i=i))
                for i in range(pages_per_row)]
    in_specs += [
        pl.BlockSpec((STRIDE * HEAD_DIM, 2 * CMP_HID), const),
        pl.BlockSpec((1, CMP_HID), const),
        pl.BlockSpec((CMP_HID, HEAD_DIM), const),
        pl.BlockSpec((1, HEAD_DIM), const),
    ]
    return pl.pallas_call(
        functools.partial(_compress_kernel, n_in=pages_per_row),
        grid_spec=pltpu.PrefetchScalarGridSpec(
            num_scalar_prefetch=1,
            grid=(bsz,),
            in_specs=in_specs,
            out_specs=pl.BlockSpec((1, nch, KV_W), lambda bb, pt: (bb, 0, 0)),
        ),
        out_shape=jax.ShapeDtypeStruct((bsz, nch, KV_W), BF16),
        compiler_params=_params("parallel"),
        name="compress",
    )(page_table, *([chunks] * pages_per_row), wcat, bias1, w2, b2)


def _select_topk(score, n_sel):
    jn = float(score.shape[0])
    j = lax.broadcasted_iota(jnp.int32, score.shape, 0).astype(F32)
    sel = jnp.zeros(score.shape, F32)
    for _ in range(n_sel):
        mx = jnp.max(score, axis=0, keepdims=True)
        idx = jnp.min(jnp.where(score == mx, j, jn), axis=0, keepdims=True)
        hit = j == idx
        sel = jnp.where(hit, 1.0, sel)
        score = jnp.where(hit, -jnp.inf, score)
    return sel


def _lanes4(x):
    return jnp.concatenate([x] * HG, axis=1)


def _heads_to_lanes(ref, *lead):
    return jnp.concatenate([ref[(*lead, h)] for h in range(HG)], axis=1)


def _attn_prompt_kernel(q_ref, graw_ref, kc_ref, vct_ref, ks_ref, vst_ref, kw_ref, vwt_ref,
                        strip_ref, t0_ref, t1_ref, t4_ref, c2st_ref, o_ref,
                        acc_s, m_s, l_s, acc_w, m_w, l_w, sel_ref, *, n_sel, nc_pad, strip_zero):
    qb = pl.program_id(2)
    q = q_ref[...]
    qs = jnp.concatenate([q[:, h * HEAD_DIM:(h + 1) * HEAD_DIM] for h in range(HG)], axis=0)
    lanes = HG * Q_BLOCK

    off = pl.multiple_of(strip_zero - (Q_BLOCK // STRIDE) * qb, 8)
    st = _dot_nt(kc_ref[0], qs)
    st = st + jnp.concatenate([strip_ref[h, pl.ds(off, nc_pad), :] for h in range(HG)], axis=1)
    m = jnp.max(st, axis=0, keepdims=True)
    m = jnp.where(m < 0.1 * NEG, 0.0, m)
    e = jnp.exp(st - m)
    p = e * (1.0 / jnp.maximum(jnp.sum(e, axis=0, keepdims=True), 1e-30))
    o_cmp = _dot(vct_ref[0], p.astype(BF16))
    psum = p[:, 0:Q_BLOCK]
    for h in range(1, HG):
        psum = psum + p[:, h * Q_BLOCK:(h + 1) * Q_BLOCK]
    p_hi = psum.astype(BF16)
    p_lo = (psum - p_hi.astype(F32)).astype(BF16)
    imp = _dot(c2st_ref[...], p_hi) + _dot(c2st_ref[...], p_lo)

    ns = imp.shape[0]
    j = lax.broadcasted_iota(jnp.int32, imp.shape, 0)
    cur = (qb * Q_BLOCK + lax.broadcasted_iota(jnp.int32, imp.shape, 1)) // L_SLC
    forced = (j == 0) | (j == cur) | (j == cur - 1)
    score = jnp.where(j <= cur, imp + jnp.where(forced, FORCE_BONUS, 0.0), NEG)
    sel = _select_topk(score, n_sel)
    per = Q_BLOCK // L_SLC
    for kb in range(ns // per):
        sel_ref[kb] = sel[kb * per:(kb + 1) * per, :]

    def tile_update(k_ref, vt_ref, kb, addend, acc_ref, m_ref, l_ref):
        stt = _dot_nt(k_ref[0, 0, kb], qs)
        if addend is not None:
            stt = stt + addend
        m_old = m_ref[...]
        m_new = jnp.maximum(m_old, jnp.max(stt, axis=0, keepdims=True))
        alpha = jnp.exp(m_old - m_new)
        pt = jnp.exp(stt - m_new)
        l_ref[...] = alpha * l_ref[...] + jnp.sum(pt, axis=0, keepdims=True)
        acc_ref[...] = alpha * acc_ref[...] + _dot(vt_ref[0, 0, kb], pt.astype(BF16))
        m_ref[...] = m_new

    def sel_addend(kb):
        rows = jnp.where(sel_ref[kb] > 0.5, 0.0, NEG)
        rows = jnp.broadcast_to(rows[:, None, :], (per, L_SLC, Q_BLOCK)).reshape(Q_BLOCK, Q_BLOCK)
        return _lanes4(rows)

    for m_ref, l_ref, acc_ref in ((m_s, l_s, acc_s), (m_w, l_w, acc_w)):
        m_ref[...] = jnp.full((1, lanes), 2.0 * NEG, F32)
        l_ref[...] = jnp.zeros((1, lanes), F32)
        acc_ref[...] = jnp.zeros((HEAD_DIM, lanes), F32)

    def far_body(kb, carry):
        tile_update(ks_ref, vst_ref, kb, sel_addend(kb), acc_s, m_s, l_s)
        return carry

    lax.fori_loop(0, jnp.maximum(qb - 1, 0), far_body, 0)

    @pl.when(qb >= 1)
    def _():
        tile_update(ks_ref, vst_ref, qb - 1, sel_addend(qb - 1) + _heads_to_lanes(t1_ref), acc_s, m_s, l_s)

    tile_update(ks_ref, vst_ref, qb, sel_addend(qb) + _heads_to_lanes(t0_ref), acc_s, m_s, l_s)

    nwin = WINDOW // Q_BLOCK

    @pl.when(qb >= nwin)
    def _():
        tile_update(kw_ref, vwt_ref, qb - nwin, _lanes4(t4_ref[...]), acc_w, m_w, l_w)

    for dd in range(nwin - 1, 1, -1):
        @pl.when(qb >= dd)
        def _(dd=dd):
            tile_update(kw_ref, vwt_ref, qb - dd, None, acc_w, m_w, l_w)

    @pl.when(qb >= 1)
    def _():
        tile_update(kw_ref, vwt_ref, qb - 1, _heads_to_lanes(t1_ref), acc_w, m_w, l_w)

    tile_update(kw_ref, vwt_ref, qb, _heads_to_lanes(t0_ref), acc_w, m_w, l_w)

    gates = jax.nn.sigmoid(graw_ref[0, 0])
    gl = [jnp.concatenate([gates[i, h:h + 1, :] for h in range(HG)], axis=1) for i in range(3)]
    o_slc = acc_s[...] * (1.0 / jnp.maximum(l_s[...], 1e-30))
    o_win = acc_w[...] * (1.0 / jnp.maximum(l_w[...], 1e-30))
    o_t = gl[0] * o_cmp + gl[1] * o_slc + gl[2] * o_win
    for h in range(HG):
        o_ref[:, h * HEAD_DIM:(h + 1) * HEAD_DIM] = o_t[:, h * Q_BLOCK:(h + 1) * Q_BLOCK].T.astype(BF16)


def _attn_prompt(q, graw_t, kc, vct, ks5, vst5, kw5, vwt5, strip, t0, t1, t4, c2st, *, bsz, t):
    nq = t // Q_BLOCK
    nc_pad = kc.shape[1]
    ns = c2st.shape[0]
    grid = (bsz, N_KV, nq)
    kv_spec = pl.BlockSpec((1, 1, nq, Q_BLOCK, HEAD_DIM), lambda b, g, i: (b, g, 0, 0, 0))
    head_tiles = pl.BlockSpec((HG, Q_BLOCK, Q_BLOCK), lambda b, g, i: (g, 0, 0))
    lanes = HG * Q_BLOCK
    return pl.pallas_call(
        functools.partial(_attn_prompt_kernel, n_sel=min(N_SEL, ns), nc_pad=nc_pad,
                          strip_zero=strip.shape[1] - nc_pad),
        grid=grid,
        in_specs=[
            pl.BlockSpec((Q_BLOCK, HG * HEAD_DIM), lambda b, g, i: (b * nq + i, g)),
            pl.BlockSpec((1, 1, 3, HG, Q_BLOCK), lambda b, g, i: (b, g, 0, 0, i)),
            pl.BlockSpec((1, nc_pad, HEAD_DIM), lambda b, g, i: (b, 0, g)),
            pl.BlockSpec((1, HEAD_DIM, nc_pad), lambda b, g, i: (b, g, 0)),
            kv_spec, kv_spec, kv_spec, kv_spec,
            pl.BlockSpec((HG,) + strip.shape[1:], lambda b, g, i: (g, 0, 0)),
            head_tiles, head_tiles,
            pl.BlockSpec((Q_BLOCK, Q_BLOCK), lambda b, g, i: (0, 0)),
            pl.BlockSpec(c2st.shape, lambda b, g, i: (0, 0)),
        ],
        out_specs=pl.BlockSpec((Q_BLOCK, HG * HEAD_DIM), lambda b, g, i: (b * nq + i, g)),
        out_shape=jax.ShapeDtypeStruct((bsz * t, D_MODEL), BF16),
        scratch_shapes=[
            pltpu.VMEM((HEAD_DIM, lanes), F32), pltpu.VMEM((1, lanes), F32), pltpu.VMEM((1, lanes), F32),
            pltpu.VMEM((HEAD_DIM, lanes), F32), pltpu.VMEM((1, lanes), F32), pltpu.VMEM((1, lanes), F32),
            pltpu.VMEM((nq, Q_BLOCK // L_SLC, Q_BLOCK), F32),
        ],
        compiler_params=_params("parallel", "parallel", "arbitrary"),
        name="attn_prompt",
    )(q, graw_t, kc, vct, ks5, vst5, kw5, vwt5, strip, t0, t1, t4, c2st)


def _softmax_rows(s):
    m = jnp.max(s, axis=1, keepdims=True)
    m = jnp.where(m < 0.1 * NEG, 0.0, m)
    e = jnp.exp(s - m)
    return e * (1.0 / jnp.maximum(jnp.sum(e, axis=1, keepdims=True), 1e-30))


def _attn_sample_kernel(*refs, n_pages, tq, n_sel, ns, n_state):
    pt_ref = refs[0]
    del pt_ref
    kpages = refs[1:1 + n_pages]
    vpages = refs[1 + n_pages:1 + 2 * n_pages]
    (q_ref, g_ref, kc_ref, vc_ref, ksn_ref, vsn_ref, kwn_ref, vwn_ref, kst_ref, vst_ref,
     bc_ref, bs_ref, bw_ref, c2st_ref, e_ref,
     o_ref, okw_ref, ovw_ref, kbuf, vbuf, wkbuf, wvbuf) = refs[1 + 2 * n_pages:]
    rows = N_HEADS * tq
    n_cache = n_pages * PAGE_SIZE
    tail = kbuf.shape[0] - n_cache
    wtail = wkbuf.shape[0] - n_state

    def pad_rows(x, n):
        return jnp.concatenate([x, jnp.zeros((n - x.shape[0], x.shape[1]), x.dtype)], axis=0)

    for i in range(n_pages):
        kbuf[i * PAGE_SIZE:(i + 1) * PAGE_SIZE, :] = kpages[i][0].astype(BF16)
        vbuf[i * PAGE_SIZE:(i + 1) * PAGE_SIZE, :] = vpages[i][0].astype(BF16)
    kbuf[n_cache:, :] = pad_rows(ksn_ref[0], tail).astype(BF16)
    vbuf[n_cache:, :] = pad_rows(vsn_ref[0], tail).astype(BF16)
    kst = kst_ref[0]
    vst = vst_ref[0]
    kwn = kwn_ref[0]
    vwn = vwn_ref[0]
    wkbuf[0:n_state, :] = kst.astype(BF16)
    wvbuf[0:n_state, :] = vst.astype(BF16)
    wkbuf[n_state:, :] = pad_rows(kwn, wtail).astype(BF16)
    wvbuf[n_state:, :] = pad_rows(vwn, wtail).astype(BF16)
    keep = okw_ref.shape[1] - tq
    okw_ref[0, 0:keep, :] = kst[n_state - keep:, :]
    okw_ref[0, keep:, :] = kwn
    ovw_ref[0, 0:keep, :] = vst[n_state - keep:, :]
    ovw_ref[0, keep:, :] = vwn

    q = q_ref[0].astype(F32)
    zero = jnp.zeros((tq, HEAD_DIM), F32)
    qrows = []
    for g in range(N_KV):
        for h in range(HG):
            piece = q[:, (g * HG + h) * HEAD_DIM:(g * HG + h + 1) * HEAD_DIM]
            qrows.append(jnp.concatenate([piece if gg == g else zero for gg in range(N_KV)], axis=1))
    qbd = jnp.concatenate(qrows, axis=0).astype(BF16)

    p_c = _softmax_rows(_dot_nt(qbd, kc_ref[0]) + bc_ref[...])
    o_cmp = _dot(p_c.astype(BF16), vc_ref[0])
    reps = []
    for g in range(N_KV):
        s = p_c[g * HG * tq:g * HG * tq + tq, :]
        for h in range(1, HG):
            s = s + p_c[(g * HG + h) * tq:(g * HG + h + 1) * tq, :]
        reps.extend([s] * HG)
    psum = jnp.concatenate(reps, axis=0)
    p_hi = psum.astype(BF16)
    p_lo = (psum - p_hi.astype(F32)).astype(BF16)
    imp_t = _dot_nt(c2st_ref[...], p_hi) + _dot_nt(c2st_ref[...], p_lo)

    nsp = -(-ns // 8) * 8
    imp_t = imp_t[0:nsp, :]
    j = lax.broadcasted_iota(jnp.int32, imp_t.shape, 0)
    tt = lax.broadcasted_iota(jnp.int32, imp_t.shape, 1) % tq
    cur = (PAST_LEN + tt) // L_SLC
    forced = (j == 0) | (j == cur) | (j == cur - 1)
    score = jnp.where(j <= cur, imp_t + jnp.where(forced, FORCE_BONUS, 0.0), NEG)
    score = jnp.where(j < ns, score, -jnp.inf)
    sel_t = _select_topk(score, n_sel)
    sel = pad_rows(sel_t, e_ref.shape[0]).T.astype(BF16)
    mask = _dot(sel, e_ref[...])

    s_s = _dot_nt(qbd, kbuf[...]) + bs_ref[...] + jnp.where(mask > 0.5, 0.0, NEG)
    o_slc = _dot(_softmax_rows(s_s).astype(BF16), vbuf[...])

    s_w = _dot_nt(qbd, wkbuf[...]) + bw_ref[...]
    o_win = _dot(_softmax_rows(s_w).astype(BF16), wvbuf[...])

    gates = jax.nn.sigmoid(g_ref[0])
    o_full = gates[:, 0:1] * o_cmp + gates[:, 1:2] * o_slc + gates[:, 2:3] * o_win
    pieces = []
    for g in range(N_KV):
        for h in range(HG):
            r0 = (g * HG + h) * tq
            pieces.append(o_full[r0:r0 + tq, g * HEAD_DIM:(g + 1) * HEAD_DIM])
    o_ref[0] = jnp.concatenate(pieces, axis=1).astype(BF16)


def _attn_sample(page_table, pool_k, pool_v, q3, gcol, kc, vc, ksn, vsn, kwn, vwn, kst, vst,
                 bc, bs, bw, c2st, expand, *, ns):
    bsz, n_pages = page_table.shape
    tq = q3.shape[1]
    n_state = kst.shape[1]
    n_keys = bs.shape[1]
    n_wkeys = bw.shape[1]
    rows = N_HEADS * tq
    wb_new = min(WINDOW, n_state + tq)
    page_specs = [pl.BlockSpec((1, PAGE_SIZE, KV_W), functools.partial(lambda b, pt, i: (pt[b, i], 0, 0), i=i))
                  for i in range(n_pages)]
    per_b = lambda shape: pl.BlockSpec((1,) + shape, lambda b, pt: (b, 0, 0))
    const = lambda arr: pl.BlockSpec(arr.shape, lambda b, pt: (0, 0))
    in_specs = page_specs + page_specs + [
        per_b((tq, D_MODEL)), per_b((rows, 3)),
        per_b(kc.shape[1:]), per_b(vc.shape[1:]),
        per_b((tq, KV_W)), per_b((tq, KV_W)), per_b((tq, KV_W)), per_b((tq, KV_W)),
        per_b((n_state, KV_W)), per_b((n_state, KV_W)),
        const(bc), const(bs), const(bw), const(c2st), const(expand),
    ]
    return pl.pallas_call(
        functools.partial(_attn_sample_kernel, n_pages=n_pages, tq=tq, n_sel=min(N_SEL, ns), ns=ns, n_state=n_state),
        grid_spec=pltpu.PrefetchScalarGridSpec(
            num_scalar_prefetch=1,
            grid=(bsz,),
            in_specs=in_specs,
            out_specs=[per_b((tq, D_MODEL)), per_b((wb_new, KV_W)), per_b((wb_new, KV_W))],
            scratch_shapes=[
                pltpu.VMEM((n_keys, KV_W), BF16), pltpu.VMEM((n_keys, KV_W), BF16),
                pltpu.VMEM((n_wkeys, KV_W), BF16), pltpu.VMEM((n_wkeys, KV_W), BF16),
            ],
        ),
        out_shape=[
            jax.ShapeDtypeStruct((bsz, tq, D_MODEL), BF16),
            jax.ShapeDtypeStruct((bsz, wb_new, KV_W), F32),
            jax.ShapeDtypeStruct((bsz, wb_new, KV_W), F32),
        ],
        compiler_params=_params("parallel"),
        name="attn_sample",
    )(page_table, *([pool_k] * n_pages), *([pool_v] * n_pages), q3, gcol, kc, vc, ksn, vsn, kwn, vwn, kst, vst,
      bc, bs, bw, c2st, expand)


def _t5_bucket(dist):
    n = jnp.maximum(dist, 0)
    max_exact = N_BUCKETS // 2
    nf = jnp.maximum(n, 1).astype(F32)
    large = max_exact + (jnp.log(nf / max_exact) / math.log(MAX_DIST / max_exact) * (N_BUCKETS - max_exact)).astype(jnp.int32)
    large = jnp.minimum(large, N_BUCKETS - 1)
    return jnp.where(n < max_exact, n, large)


def _bias_by_dist(rel_bias):
    return rel_bias.astype(F32)[_t5_bucket(jnp.arange(MAX_DIST + 1))]


def _bias_tile(btab, dist, valid, shift):
    vals = btab[jnp.clip(dist, 0, MAX_DIST)]
    vals = jnp.moveaxis(vals, -1, 0) - shift.reshape((-1,) + (1,) * dist.ndim)
    return jnp.where(valid[None], vals, NEG)


def _cmp_to_slc_t(nc_pad, nc, ns, ns_pad):
    cs = jnp.arange(nc_pad)[None, :] * STRIDE
    js = jnp.arange(ns_pad)[:, None] * L_SLC
    ov = jnp.clip(jnp.minimum(cs + L_CMP, js + L_SLC) - jnp.maximum(cs, js), 0, None)
    ok = (jnp.arange(nc_pad)[None, :] < nc) & (jnp.arange(ns_pad)[:, None] < ns)
    return jnp.where(ok, ov.astype(F32) / L_CMP, 0.0).astype(BF16)


def _row(v):
    return v.reshape(1, -1).astype(F32)


def kernel(x_prompt, x_sample, cache_k_cmp, cache_v_cmp, cache_k_slc, cache_v_slc, state_k_win, state_v_win, state_conv, page_table, rel_bias, conv_w_pw1, conv_b_pw1, conv_w_dw, conv_b_dw, conv_ln_g, conv_ln_b, conv_w_pw2, conv_b_pw2, nsa_w_qg, nsa_w_o, w_kv, cmp_pe, cmp_w1, cmp_b1, cmp_w2, cmp_b2, ffn_w_in, ffn_w_out, ln_mix_g, ln_mix_b, ln_ffn_g, ln_ffn_b):
    d = D_MODEL
    d_ff = ffn_w_out.shape[1]
    bsz, t, _ = x_prompt.shape
    dbsz, dt, _ = x_sample.shape
    zeros_d = jnp.zeros((1, d), F32)
    zeros_ff = jnp.zeros((1, 2 * d_ff), F32)

    w_pw1 = conv_w_pw1[0].astype(BF16)
    w_pw2 = conv_w_pw2[0].astype(BF16)
    w_in = ffn_w_in.astype(BF16)
    w_out = ffn_w_out.astype(BF16)
    w_kvb = w_kv.astype(BF16)
    w_qg = jnp.pad(nsa_w_qg[0], ((0, 0), (0, d + 128 - nsa_w_qg.shape[2]))).astype(BF16)
    w_o = nsa_w_o[0].astype(BF16)
    w_dw = jnp.pad(conv_w_dw[0], ((0, CONV_HEAD - CONV_W), (0, 0)))

    half = STRIDE * HEAD_DIM
    cmp_ops = []
    for i in range(2):
        w1b = cmp_w1[i].astype(BF16)
        wcat = jnp.concatenate([w1b[:half], w1b[half:]], axis=1)
        pe8 = jnp.broadcast_to(cmp_pe[i].reshape(1, -1), (8, L_CMP * HEAD_DIM)).astype(BF16)
        pe_w1 = _mm_plain(pe8, w1b)[0:1]
        cmp_ops.append((wcat, pe_w1 + _row(cmp_b1[i]), cmp_w2[i].astype(BF16), _row(cmp_b2[i])))

    btab = _bias_by_dist(rel_bias)
    far = btab[MAX_DIST]
    none = jnp.zeros_like(far)

    def trunk_front(x, state32, tm_conv):
        b_, t_, _ = x.shape
        m = b_ * t_
        tm = min(m, 1024)
        xb = x.reshape(m, d).astype(BF16)
        u = _mm_pair(xb, w_pw1, _row(conv_b_pw1[0]), act="glu", out_dtype=F32, tm=tm, tn=512)
        c = _conv_ln_silu(state32, u.reshape(b_, t_, d), w_dw, _row(conv_b_dw[0]), _row(conv_ln_g[0]),
                          _row(conv_ln_b[0]), tm=tm_conv)
        x1, x1b = _mm_res_ln(c.reshape(m, d), w_pw2, _row(conv_b_pw2[0]), x.reshape(m, d),
                             _row(ln_mix_g[0]), _row(ln_mix_b[0]), tm=512, tk=d // 2)
        x2, x2b = _ffn(x1, x1b, 0)
        return x2, x2b, u

    def _ffn(xf, xb, layer):
        m = xf.shape[0]
        h = _mm_pair(xb, w_in[layer], zeros_ff, act="swiglu", out_dtype=BF16, tm=min(m, 1024), tn=512)
        return _mm_res_ln(h, w_out[layer], zeros_d, xf, _row(ln_ffn_g[layer]), _row(ln_ffn_b[layer]),
                          tm=512, tk=d_ff // 4)

    def trunk_back(x2, x2b, o):
        x3, x3b = _mm_res_ln(o, w_o, zeros_d, x2, _row(ln_mix_g[1]), _row(ln_mix_b[1]), tm=512, tk=d // 2)
        y, _ = _ffn(x3, x3b, 1)
        return y

    m_p = bsz * t
    x2, x2b, u_p = trunk_front(x_prompt, jnp.zeros((bsz, CONV_HEAD, d), F32), 64)
    kv = _mm_kv(x2b, w_kvb, tm=256)
    kcr, vcr, ksr, vsr, kwr, vwr = kv[:6]
    kcb, vcb, ksb, vsb, kwb, vwb = kv[6:]
    del kcb, vcb
    nch_p = t // STRIDE
    nc_p = (t - L_CMP) // STRIDE + 1
    ident = jnp.arange(bsz, dtype=jnp.int32).reshape(bsz, 1)
    kc_p = _compress(kcr.reshape(bsz, nch_p, CHUNK_W), ident, *cmp_ops[0], pages_per_row=1, ch_per_page=nch_p)
    vc_p = _compress(vcr.reshape(bsz, nch_p, CHUNK_W), ident, *cmp_ops[1], pages_per_row=1, ch_per_page=nch_p)
    vct_p = jnp.swapaxes(vc_p, 1, 2)

    q_p, graw_p = _mm_qg(x2b, w_qg, tm=512)
    graw_t = graw_p[:, :3 * N_HEADS].reshape(bsz, t, N_KV, HG, 3).transpose(0, 2, 4, 3, 1)
    nq = t // Q_BLOCK

    def tiles_k(a):
        return a.reshape(bsz, nq, Q_BLOCK, N_KV, HEAD_DIM).transpose(0, 3, 1, 2, 4)

    def tiles_vt(a):
        return a.reshape(bsz, nq, Q_BLOCK, N_KV, HEAD_DIM).transpose(0, 3, 1, 4, 2)

    kk = jnp.arange(Q_BLOCK)[:, None]
    qq = jnp.arange(Q_BLOCK)[None, :]
    t0 = _bias_tile(btab, qq - kk, qq - kk >= 0, far)
    t1 = _bias_tile(btab, Q_BLOCK + qq - kk, jnp.ones((Q_BLOCK, Q_BLOCK), bool), far)
    t4 = jnp.where(kk > qq, 0.0, NEG).astype(F32)
    nshift = (Q_BLOCK // STRIDE) * (nq - 1)
    cprime = jnp.arange(nshift + nch_p)[:, None] - nshift
    dist_c = qq - STRIDE * cprime - (L_CMP - 1)
    strip = _bias_tile(btab, dist_c, dist_c >= 0, none)
    ns_p = -(-t // L_SLC)
    c2st_p = _cmp_to_slc_t(nch_p, nc_p, ns_p, ns_p)
    o_p = _attn_prompt(q_p, graw_t, kc_p, vct_p, tiles_k(ksb), tiles_vt(vsb), tiles_k(kwb), tiles_vt(vwb),
                       strip, t0, t1, t4, c2st_p, bsz=bsz, t=t)
    y_prompt = trunk_back(x2, x2b, o_p).reshape(bsz, t, d)

    as4 = lambda a, b_, t_: a.reshape(b_, t_, N_KV, HEAD_DIM)
    wb_p = min(WINDOW, t)
    pk_cmp, pv_cmp, pk_slc, pv_slc = (as4(a, bsz, t) for a in (kcr, vcr, ksr, vsr))
    pk_win = as4(kwr, bsz, t)[:, t - wb_p:]
    pv_win = as4(vwr, bsz, t)[:, t - wb_p:]
    p_conv = u_p.reshape(bsz, t, d)[:, t - (CONV_W - 1):][None]

    m_s = dbsz * dt
    st0 = state_conv[0]
    state32 = jnp.pad(st0, ((0, 0), (CONV_HEAD - (CONV_W - 1), 0), (0, 0)))
    s2, s2b, u_s = trunk_front(x_sample, state32, dt)
    kv = _mm_kv(s2b, w_kvb, tm=256)
    skc, svc, sks, svs, skw, svw = kv[:6]
    n_pages = page_table.shape[1]
    n_pool = cache_k_cmp.shape[0]
    ch_pp = PAGE_SIZE // STRIDE
    nch_s = n_pages * ch_pp
    t_all = PAST_LEN + dt
    nc_s = (t_all - L_CMP) // STRIDE + 1
    kc_s = _compress(cache_k_cmp.reshape(n_pool, ch_pp, CHUNK_W), page_table, *cmp_ops[0],
                     pages_per_row=n_pages, ch_per_page=ch_pp)
    vc_s = _compress(cache_v_cmp.reshape(n_pool, ch_pp, CHUNK_W), page_table, *cmp_ops[1],
                     pages_per_row=n_pages, ch_per_page=ch_pp)
    q_s, graw_s = _mm_qg(s2b, w_qg, tm=512)
    gcol = graw_s[:, :3 * N_HEADS].reshape(dbsz, dt, N_HEADS, 3).transpose(0, 2, 1, 3).reshape(dbsz, N_HEADS * dt, 3)

    ns_s = -(-t_all // L_SLC)
    n_keys = -(-(ns_s * L_SLC) // 128) * 128
    n_state = state_k_win.shape[1]
    n_wkeys = -(-(n_state + dt) // 128) * 128
    qpos = (PAST_LEN + jnp.arange(dt))[:, None]
    cc = jnp.arange(nch_s)[None, :]
    dist = qpos - (cc * STRIDE + L_CMP - 1)
    bc = _bias_tile(btab, dist, (dist >= 0) & (cc < nc_s), none).reshape(N_HEADS * dt, nch_s)
    kpos = jnp.arange(n_keys)[None, :]
    dist = qpos - kpos
    bs = _bias_tile(btab, dist, dist >= 0, far).reshape(N_HEADS * dt, n_keys)
    wi = jnp.arange(n_wkeys)[None, :]
    dist = qpos - (PAST_LEN - n_state + wi)
    bw = _bias_tile(btab, dist, (dist >= 0) & (dist < WINDOW) & (wi < n_state + dt), none).reshape(N_HEADS * dt, n_wkeys)
    c2st_s = _cmp_to_slc_t(nch_s, nc_s, ns_s, 128)
    expand = (jnp.arange(128)[:, None] == (jnp.arange(n_keys)[None, :] // L_SLC)).astype(BF16)

    o_s, sk_win, sv_win = _attn_sample(
        page_table, cache_k_slc.reshape(n_pool, PAGE_SIZE, KV_W), cache_v_slc.reshape(n_pool, PAGE_SIZE, KV_W),
        q_s.reshape(dbsz, dt, d), gcol, kc_s, vc_s,
        sks.reshape(dbsz, dt, KV_W), svs.reshape(dbsz, dt, KV_W), skw.reshape(dbsz, dt, KV_W), svw.reshape(dbsz, dt, KV_W),
        state_k_win.reshape(dbsz, n_state, KV_W), state_v_win.reshape(dbsz, n_state, KV_W),
        bc, bs, bw, c2st_s, expand, ns=ns_s)
    y_sample = trunk_back(s2, s2b, o_s.reshape(m_s, d)).reshape(dbsz, dt, d)

    sk_cmp, sv_cmp, sk_slc, sv_slc = (as4(a, dbsz, dt) for a in (skc, svc, sks, svs))
    wb_s = sk_win.shape[1]
    sk_win = sk_win.reshape(dbsz, wb_s, N_KV, HEAD_DIM)
    sv_win = sv_win.reshape(dbsz, wb_s, N_KV, HEAD_DIM)
    s_conv = jnp.concatenate([st0, u_s.reshape(dbsz, dt, d)], axis=1)[:, -(CONV_W - 1):][None]

    return (y_prompt, y_sample, pk_cmp, pv_cmp, pk_slc, pv_slc, pk_win, pv_win, p_conv,
            sk_cmp, sv_cmp, sk_slc, sv_slc, sk_win, sv_win, s_conv)


def _mm_plain_kernel(x_ref, w_ref, o_ref):
    o_ref[...] = _dot(x_ref[...], w_ref[...])


def _mm_plain(x, w):
    return pl.pallas_call(
        _mm_plain_kernel,
        out_shape=jax.ShapeDtypeStruct((x.shape[0], w.shape[1]), F32),
        compiler_params=pltpu.CompilerParams(vmem_limit_bytes=VMEM_LIMIT),
        name="mm_plain",
    )(x, w)
```

```python
import functools
import math

import jax
import jax.numpy as jnp
from jax import lax
from jax.experimental import pallas as pl
from jax.experimental.pallas import tpu as pltpu

D_MODEL = 2048
PAST_LEN = 2048
PAGE_SIZE = 128
N_HEADS = 16
HEAD_DIM = D_MODEL // N_HEADS
N_KV = 4
HG = N_HEADS // N_KV
L_CMP = 32
STRIDE = 16
CMP_HID = HEAD_DIM
L_SLC = 64
N_SEL = 16
WINDOW = 512
Q_BLOCK = 128
CONV_W = 31
N_BUCKETS = 32
MAX_DIST = 128
DEPTH = 2
ALPHA = (2 * DEPTH) ** 0.25
LN_EPS = 1e-5
NEG = -1e30
FORCE_BONUS = 1e4
LOG2E = math.log2(math.e)

KV_W = N_KV * HEAD_DIM
LANES = HG * Q_BLOCK
FAR_CHUNK = 8 * Q_BLOCK
VMEM_LIMIT = 56 * 1024 * 1024

F32 = jnp.float32
BF16 = jnp.bfloat16


def _params(*sem):
    return pltpu.CompilerParams(dimension_semantics=sem, vmem_limit_bytes=VMEM_LIMIT)


def _dot(a, b):
    return jnp.dot(a, b, preferred_element_type=F32)


def _dot_nt(a, b):
    return lax.dot_general(a, b, (((1,), (1,)), ((), ())), preferred_element_type=F32)


def _layer_norm(x, g, b):
    mu = jnp.mean(x, axis=-1, keepdims=True)
    xc = x - mu
    var = jnp.mean(xc * xc, axis=-1, keepdims=True)
    return xc * lax.rsqrt(var + LN_EPS) * g + b


def _silu(x):
    return x * jax.nn.sigmoid(x)


def _gelu_tanh(x):
    c = math.sqrt(2.0 / math.pi)
    return 0.5 * x * (1.0 + jnp.tanh(c * (x + 0.044715 * (x * x * x))))


def _pad_rows(x, n):
    if x.shape[0] == n:
        return x
    return jnp.concatenate([x, jnp.zeros((n - x.shape[0], x.shape[1]), x.dtype)], axis=0)


def _mm_pair_kernel(x_ref, wa_ref, wb_ref, ba_ref, bb_ref, o_ref, *, act):
    x = x_ref[...]
    a = _dot(x, wa_ref[...]) + ba_ref[...]
    b = _dot(x, wb_ref[...]) + bb_ref[...]
    if act == "glu":
        o = a * jax.nn.sigmoid(b)
    else:
        o = _silu(a) * b
    o_ref[...] = o.astype(o_ref.dtype)


def _mm_pair(x, w, bias, *, act, out_dtype, tm, tn):
    m, k = x.shape
    n = w.shape[1] // 2
    nj = n // tn
    return pl.pallas_call(
        functools.partial(_mm_pair_kernel, act=act),
        grid=(m // tm, nj),
        in_specs=[
            pl.BlockSpec((tm, k), lambda i, j: (i, 0)),
            pl.BlockSpec((k, tn), lambda i, j: (0, j)),
            pl.BlockSpec((k, tn), lambda i, j: (0, j + nj)),
            pl.BlockSpec((1, tn), lambda i, j: (0, j)),
            pl.BlockSpec((1, tn), lambda i, j: (0, j + nj)),
        ],
        out_specs=pl.BlockSpec((tm, tn), lambda i, j: (i, j)),
        out_shape=jax.ShapeDtypeStruct((m, n), out_dtype),
        compiler_params=_params("parallel", "parallel"),
        name="mm_pair_" + act,
    )(x, w, w, bias, bias)


def _mm_res_ln_kernel(h_ref, w_ref, bias_ref, xres_ref, g_ref, b_ref, of_ref, ob_ref, acc_ref):
    kk = pl.program_id(1)

    @pl.when(kk == 0)
    def _():
        acc_ref[...] = jnp.zeros_like(acc_ref)

    acc_ref[...] += _dot(h_ref[...], w_ref[...])

    @pl.when(kk == pl.num_programs(1) - 1)
    def _():
        y = ALPHA * xres_ref[...] + (acc_ref[...] + bias_ref[...])
        o = _layer_norm(y, g_ref[...], b_ref[...])
        of_ref[...] = o
        ob_ref[...] = o.astype(BF16)


def _mm_res_ln(h, w, bias, xres, g, b, *, tm, tk):
    m, k = h.shape
    d = w.shape[1]
    row = lambda i, kk: (i, 0)
    vec = lambda i, kk: (0, 0)
    return pl.pallas_call(
        _mm_res_ln_kernel,
        grid=(m // tm, k // tk),
        in_specs=[
            pl.BlockSpec((tm, tk), lambda i, kk: (i, kk)),
            pl.BlockSpec((tk, d), lambda i, kk: (kk, 0)),
            pl.BlockSpec((1, d), vec),
            pl.BlockSpec((tm, d), row),
            pl.BlockSpec((1, d), vec),
            pl.BlockSpec((1, d), vec),
        ],
        out_specs=[pl.BlockSpec((tm, d), row), pl.BlockSpec((tm, d), row)],
        out_shape=[jax.ShapeDtypeStruct((m, d), F32), jax.ShapeDtypeStruct((m, d), BF16)],
        scratch_shapes=[pltpu.VMEM((tm, d), F32)],
        compiler_params=_params("parallel", "arbitrary"),
        name="mm_res_ln",
    )(h, w, bias, xres, g, b)


N_KV_OUT = 6
N_KV_BF16 = 4


def _mm_kv_kernel(x_ref, w_ref, *out_refs, tm):
    x = x_ref[...]
    for idx in range(N_KV_OUT):
        acc = _dot(x, w_ref[:, idx * KV_W:(idx + 1) * KV_W])
        for g in range(N_KV):
            out_refs[idx][pl.ds(g, tm, stride=N_KV), :] = acc[:, g * HEAD_DIM:(g + 1) * HEAD_DIM]
        first_bf16 = N_KV_OUT - N_KV_BF16
        if idx >= first_bf16:
            out_refs[N_KV_OUT + idx - first_bf16][...] = acc.astype(BF16)


def _mm_kv(x, w, *, tm):
    m, k = x.shape
    row = lambda i: (i, 0)
    return pl.pallas_call(
        functools.partial(_mm_kv_kernel, tm=tm),
        grid=(m // tm,),
        in_specs=[pl.BlockSpec((tm, k), row), pl.BlockSpec((k, N_KV_OUT * KV_W), lambda i: (0, 0))],
        out_specs=[pl.BlockSpec((tm * N_KV, HEAD_DIM), row)] * N_KV_OUT + [pl.BlockSpec((tm, KV_W), row)] * N_KV_BF16,
        out_shape=([jax.ShapeDtypeStruct((m * N_KV, HEAD_DIM), F32)] * N_KV_OUT
                   + [jax.ShapeDtypeStruct((m, KV_W), BF16)] * N_KV_BF16),
        compiler_params=_params("parallel"),
        name="mm_kv",
    )(x, w)


def _mm_qg_kernel(x_ref, w_ref, q_ref, g_ref):
    x = x_ref[...]
    scale = HEAD_DIM ** -0.5 * LOG2E
    for c in range(D_MODEL // 512):
        acc = _dot(x, w_ref[:, c * 512:(c + 1) * 512])
        q_ref[:, c * 512:(c + 1) * 512] = (acc * scale).astype(BF16)
    g_ref[...] = _dot(x, w_ref[:, D_MODEL:])


def _mm_qg(x, w, *, tm):
    m, k = x.shape
    row = lambda i: (i, 0)
    return pl.pallas_call(
        _mm_qg_kernel,
        grid=(m // tm,),
        in_specs=[pl.BlockSpec((tm, k), row), pl.BlockSpec((k, D_MODEL + 128), lambda i: (0, 0))],
        out_specs=[pl.BlockSpec((tm, D_MODEL), row), pl.BlockSpec((tm, 128), row)],
        out_shape=[jax.ShapeDtypeStruct((m, D_MODEL), BF16), jax.ShapeDtypeStruct((m, 128), F32)],
        compiler_params=_params("parallel"),
        name="mm_qg",
    )(x, w)


def _mm_plain_kernel(x_ref, w_ref, o_ref):
    o_ref[...] = _dot(x_ref[...], w_ref[...])


def _mm_plain(x, w):
    return pl.pallas_call(
        _mm_plain_kernel,
        out_shape=jax.ShapeDtypeStruct((x.shape[0], w.shape[1]), F32),
        compiler_params=pltpu.CompilerParams(vmem_limit_bytes=VMEM_LIMIT),
        name="mm_plain",
    )(x, w)


CONV_HEAD = 32
CONV_RB = 32
CONV_CB = 512


def _conv_kernel(state_ref, uprev_ref, ucur_ref, w_ref, bdw_ref, g_ref, b_ref, o_ref, z_ref, y_ref, *, tm):
    i = pl.program_id(1)
    z_ref[0:CONV_HEAD, :] = jnp.where(i == 0, state_ref[0], uprev_ref[0])
    z_ref[CONV_HEAD:CONV_HEAD + tm, :] = ucur_ref[0]
    pad = CONV_HEAD - (CONV_W - 1)
    rb = min(CONV_RB, tm)
    for r0 in range(0, tm, rb):
        for c0 in range(0, D_MODEL, CONV_CB):
            acc = jnp.zeros((rb, CONV_CB), F32)
            for w in range(CONV_W):
                acc = acc + z_ref[r0 + pad + w:r0 + pad + w + rb, c0:c0 + CONV_CB] * w_ref[w:w + 1, c0:c0 + CONV_CB]
            y_ref[r0:r0 + rb, c0:c0 + CONV_CB] = acc
    y = _layer_norm(y_ref[...] + bdw_ref[...], g_ref[...], b_ref[...])
    o_ref[0] = _silu(y).astype(BF16)


def _conv_ln_silu(state32, u, w_dw, b_dw, g, b, *, tm):
    bsz, t, d = u.shape
    per = tm // CONV_HEAD if tm >= CONV_HEAD else 0
    uprev = u if t >= CONV_HEAD else state32
    vec = lambda bb, i: (0, 0)
    return pl.pallas_call(
        functools.partial(_conv_kernel, tm=tm),
        grid=(bsz, t // tm),
        in_specs=[
            pl.BlockSpec((1, CONV_HEAD, d), lambda bb, i: (bb, 0, 0)),
            pl.BlockSpec((1, CONV_HEAD, d), lambda bb, i: (bb, jnp.maximum(i * per - 1, 0), 0)),
            pl.BlockSpec((1, tm, d), lambda bb, i: (bb, i, 0)),
            pl.BlockSpec((CONV_HEAD, d), vec),
            pl.BlockSpec((1, d), vec),
            pl.BlockSpec((1, d), vec),
            pl.BlockSpec((1, d), vec),
        ],
        out_specs=pl.BlockSpec((1, tm, d), lambda bb, i: (bb, i, 0)),
        out_shape=jax.ShapeDtypeStruct((bsz, t, d), BF16),
        scratch_shapes=[pltpu.VMEM((CONV_HEAD + tm, d), F32), pltpu.VMEM((tm, d), F32)],
        compiler_params=_params("parallel", "parallel"),
        name="conv_ln_silu",
    )(state32, uprev, u, w_dw, b_dw, g, b)


CHUNK_ROWS = STRIDE * N_KV


def _compress_kernel(*refs, n_in, ch_pp):
    x_refs = refs[1:1 + n_in]
    wcat_ref, bias1_ref, w2_ref, b2_ref, o_ref, xall_ref = refs[1 + n_in:]
    nch = n_in * ch_pp
    for i in range(n_in):
        for s in range(STRIDE):
            for g in range(N_KV):
                r0 = g * nch + i * ch_pp
                xall_ref[r0:r0 + ch_pp, s * HEAD_DIM:(s + 1) * HEAD_DIM] = (
                    x_refs[i][0, pl.ds(s * N_KV + g, ch_pp, stride=CHUNK_ROWS), :])
    r = _dot(xall_ref[...].astype(BF16), wcat_ref[...])
    lo = r[:, :CMP_HID]
    hi = r[:, CMP_HID:]
    hi_next = pltpu.roll(hi, N_KV * nch - 1, 0)
    h = _gelu_tanh(lo + hi_next + bias1_ref[...])
    out = _dot(h.astype(BF16), w2_ref[...]) + b2_ref[...]
    for g in range(N_KV):
        o_ref[0, :, g * HEAD_DIM:(g + 1) * HEAD_DIM] = out[g * nch:(g + 1) * nch, :].astype(BF16)


def _compress(rows2d, page_table, wcat, bias1, w2, b2, *, pages_per_row, ch_per_page):
    bsz = page_table.shape[0]
    nch = pages_per_row * ch_per_page
    const = lambda bb, pt: (0, 0)
    page_rows = ch_per_page * CHUNK_ROWS
    in_specs = [pl.BlockSpec((1, page_rows, HEAD_DIM), functools.partial(lambda bb, pt, i: (pt[bb, i], 0, 0), i=i))
                for i in range(pages_per_row)]
    in_specs += [
        pl.BlockSpec((STRIDE * HEAD_DIM, 2 * CMP_HID), const),
        pl.BlockSpec((1, CMP_HID), const),
        pl.BlockSpec((CMP_HID, HEAD_DIM), const),
        pl.BlockSpec((1, HEAD_DIM), const),
    ]
    return pl.pallas_call(
        functools.partial(_compress_kernel, n_in=pages_per_row, ch_pp=ch_per_page),
        grid_spec=pltpu.PrefetchScalarGridSpec(
            num_scalar_prefetch=1,
            grid=(bsz,),
            in_specs=in_specs,
            out_specs=pl.BlockSpec((1, nch, KV_W), lambda bb, pt: (bb, 0, 0)),
            scratch_shapes=[pltpu.VMEM((N_KV * nch, STRIDE * HEAD_DIM), F32)],
        ),
        out_shape=jax.ShapeDtypeStruct((bsz, nch, KV_W), BF16),
        compiler_params=_params("parallel"),
        name="compress",
    )(page_table, *([rows2d] * pages_per_row), wcat, bias1, w2, b2)


def _select_topk(score, n_sel):
    jn = score.shape[0]
    sub = lax.broadcasted_iota(jnp.int32, (8, score.shape[1]), 0)
    blks = [score[8 * v:8 * v + 8, :] for v in range(jn // 8)]
    cnt = [jnp.zeros(b.shape, F32) for b in blks]
    for jp in range(jn):
        row = score[jp:jp + 1, :]
        for v, blk in enumerate(blks):
            gt = jnp.where(row > blk, 1.0, 0.0)
            ge = jnp.where(row >= blk, 1.0, 0.0)
            if 8 * v + 8 <= jp:
                beats = gt
            elif 8 * v > jp:
                beats = ge
            else:
                beats = jnp.where(sub > jp % 8, ge, gt)
            cnt[v] = cnt[v] + beats
    return jnp.concatenate(cnt, axis=0) < n_sel


def _col_max(x):
    return jnp.max(x, axis=0, keepdims=True)


def _col_sum(x):
    return jnp.sum(x, axis=0, keepdims=True)


def _attn_prompt_kernel(q_ref, graw_ref, kc_ref, vct_ref, kaug_ref, vst1_ref, vst4_ref, kwp_ref, vwt_ref,
                        strip_ref, t0_ref, t1_ref, wadd_ref, c2st_ref, o_ref, sfar_ref,
                        *, n_sel, nc_pad, strip_zero, nwin):
    qb = pl.program_id(2)
    q = q_ref[...]
    qs = jnp.concatenate([q[:, h * HEAD_DIM:(h + 1) * HEAD_DIM] for h in range(HG)], axis=0)

    kw = kwp_ref[0, 0, pl.ds(pl.multiple_of(qb * Q_BLOCK, Q_BLOCK), (nwin + 1) * Q_BLOCK), :]
    sw = _dot_nt(kw, qs)
    wadd = wadd_ref[0]
    pieces = [jnp.where(qb >= nwin - dd, wadd[dd * Q_BLOCK:(dd + 1) * Q_BLOCK, :], NEG) for dd in range(nwin)]
    sw = sw + jnp.concatenate(pieces + [wadd[nwin * Q_BLOCK:, :]], axis=0)
    pw = jnp.exp2(sw - _col_max(sw))
    acc_w = _dot(vwt_ref[0, 0, qb], pw[0:Q_BLOCK, :].astype(BF16))
    for dd in range(1, nwin + 1):
        acc_w = acc_w + _dot(vwt_ref[0, 0, qb + dd], pw[dd * Q_BLOCK:(dd + 1) * Q_BLOCK, :].astype(BF16))
    o_win = acc_w * (1.0 / jnp.maximum(_col_sum(pw), 1e-30))

    off = pl.multiple_of(strip_zero - (Q_BLOCK // STRIDE) * qb, 8)
    st = _dot_nt(kc_ref[0], qs) + strip_ref[0, pl.ds(off, nc_pad), :]
    m = _col_max(st)
    m = jnp.where(m < 0.1 * NEG, 0.0, m)
    e = jnp.exp2(st - m)
    p = e * (1.0 / jnp.maximum(_col_sum(e), 1e-30))
    o_cmp = _dot(vct_ref[0], p.astype(BF16))
    psum = p[:, 0:Q_BLOCK]
    for h in range(1, HG):
        psum = psum + p[:, h * Q_BLOCK:(h + 1) * Q_BLOCK]
    p_hi = psum.astype(BF16)
    p_lo = (psum - p_hi.astype(F32)).astype(BF16)
    imp = _dot(c2st_ref[...], p_hi) + _dot(c2st_ref[...], p_lo)

    ns = imp.shape[0]
    j = lax.broadcasted_iota(jnp.int32, imp.shape, 0)
    cur = (qb * Q_BLOCK + lax.broadcasted_iota(jnp.int32, imp.shape, 1)) // L_SLC
    forced = (j == 0) | (j == cur) | (j == cur - 1)
    score = jnp.where(j <= cur, imp + jnp.where(forced, FORCE_BONUS, 0.0), NEG)
    sel = _select_topk(score, n_sel)

    per = Q_BLOCK // L_SLC

    def with_mask(keep):
        mk = _pad_rows(jnp.where(keep, 0.0, NEG), Q_BLOCK).T
        return jnp.concatenate([qs, jnp.concatenate([mk] * HG, axis=0).astype(BF16)], axis=1)

    qa_near = with_mask(sel)
    qa_far = with_mask(sel & (j < per * (qb - 1)))

    n_far = (jnp.maximum(qb - 1, 0) + FAR_CHUNK // Q_BLOCK - 1) // (FAR_CHUNK // Q_BLOCK)

    def far_scores(c, mx):
        r0 = pl.multiple_of(c * FAR_CHUNK, FAR_CHUNK)
        s = _dot_nt(kaug_ref[0, 0, pl.ds(r0, FAR_CHUNK), :], qa_far)
        sfar_ref[pl.ds(r0, FAR_CHUNK), :] = s
        return jnp.maximum(mx, _col_max(s))

    mx = lax.fori_loop(0, n_far, far_scores, jnp.full((1, LANES), 2.0 * NEG, F32))
    kb1 = jnp.maximum(qb - 1, 0)
    s1 = _dot_nt(kaug_ref[0, 0, pl.ds(pl.multiple_of(kb1 * Q_BLOCK, Q_BLOCK), Q_BLOCK), :], qa_near)
    s1 = s1 + jnp.where(qb >= 1, t1_ref[0], NEG)
    s0 = _dot_nt(kaug_ref[0, 0, pl.ds(pl.multiple_of(qb * Q_BLOCK, Q_BLOCK), Q_BLOCK), :], qa_near) + t0_ref[0]
    mx = jnp.maximum(mx, jnp.maximum(_col_max(s1), _col_max(s0)))

    def far_pv(c, carry):
        l, acc = carry
        r0 = pl.multiple_of(c * FAR_CHUNK, FAR_CHUNK)
        pc = jnp.exp2(sfar_ref[pl.ds(r0, FAR_CHUNK), :] - mx)
        return l + _col_sum(pc), acc + _dot(vst4_ref[0, 0, c], pc.astype(BF16))

    l_s, acc_s = lax.fori_loop(0, n_far, far_pv, (jnp.zeros((1, LANES), F32), jnp.zeros((HEAD_DIM, LANES), F32)))
    p1 = jnp.exp2(s1 - mx)
    p0 = jnp.exp2(s0 - mx)
    l_s = l_s + _col_sum(p1) + _col_sum(p0)
    acc_s = acc_s + _dot(vst1_ref[0, 0, kb1], p1.astype(BF16)) + _dot(vst1_ref[0, 0, qb], p0.astype(BF16))
    o_slc = acc_s * (1.0 / jnp.maximum(l_s, 1e-30))

    gates = jax.nn.sigmoid(graw_ref[0, 0])
    gl = [jnp.concatenate([gates[i, h:h + 1, :] for h in range(HG)], axis=1) for i in range(3)]
    o_t = gl[0] * o_cmp + gl[1] * o_slc + gl[2] * o_win
    for h in range(HG):
        o_ref[:, h * HEAD_DIM:(h + 1) * HEAD_DIM] = o_t[:, h * Q_BLOCK:(h + 1) * Q_BLOCK].T.astype(BF16)


def _attn_prompt(q, graw_t, kc, vct, kaug, vst1, vst4, kwp, vwt, strip, t0, t1, wadd, c2st, *, bsz, t):
    nq = t // Q_BLOCK
    nc_pad = kc.shape[1]
    ns = c2st.shape[0]
    nwin = WINDOW // Q_BLOCK
    per_bg = lambda arr: pl.BlockSpec((1, 1) + arr.shape[2:], lambda b, g, i: (b, g) + (0,) * (arr.ndim - 2))
    per_g = lambda arr: pl.BlockSpec((1,) + arr.shape[1:], lambda b, g, i: (g,) + (0,) * (arr.ndim - 1))
    return pl.pallas_call(
        functools.partial(_attn_prompt_kernel, n_sel=min(N_SEL, ns), nc_pad=nc_pad,
                          strip_zero=strip.shape[1] - nc_pad, nwin=nwin),
        grid=(bsz, N_KV, nq),
        in_specs=[
            pl.BlockSpec((Q_BLOCK, HG * HEAD_DIM), lambda b, g, i: (b * nq + i, g)),
            pl.BlockSpec((1, 1, 3, HG, Q_BLOCK), lambda b, g, i: (b, g, 0, 0, i)),
            pl.BlockSpec((1, nc_pad, HEAD_DIM), lambda b, g, i: (b, 0, g)),
            pl.BlockSpec((1, HEAD_DIM, nc_pad), lambda b, g, i: (b, g, 0)),
            per_bg(kaug), per_bg(vst1), per_bg(vst4), per_bg(kwp), per_bg(vwt),
            per_g(strip), per_g(t0), per_g(t1), per_g(wadd),
            pl.BlockSpec(c2st.shape, lambda b, g, i: (0, 0)),
        ],
        out_specs=pl.BlockSpec((Q_BLOCK, HG * HEAD_DIM), lambda b, g, i: (b * nq + i, g)),
        out_shape=jax.ShapeDtypeStruct((bsz * t, D_MODEL), BF16),
        scratch_shapes=[pltpu.VMEM((kaug.shape[2], LANES), F32)],
        compiler_params=_params("parallel", "parallel", "arbitrary"),
        name="attn_prompt",
    )(q, graw_t, kc, vct, kaug, vst1, vst4, kwp, vwt, strip, t0, t1, wadd, c2st)


def _softmax_rows(s):
    m = jnp.max(s, axis=1, keepdims=True)
    m = jnp.where(m < 0.1 * NEG, 0.0, m)
    e = jnp.exp2(s - m)
    return e * (1.0 / jnp.maximum(jnp.sum(e, axis=1, keepdims=True), 1e-30))


def _attn_sample_kernel(*refs, n_pages, tq, n_sel, ns, n_state):
    kpages = refs[1:1 + n_pages]
    vpages = refs[1 + n_pages:1 + 2 * n_pages]
    (q_ref, g_ref, kc_ref, vc_ref, ksn_ref, vsn_ref, kwn_ref, vwn_ref, kwl_ref, vwl_ref, kst_ref, vst_ref,
     bc_ref, bs_ref, bw_ref, c2st_ref, e_ref,
     o_ref, okw_ref, ovw_ref, kbuf, vbuf, wkbuf, wvbuf) = refs[1 + 2 * n_pages:]
    n_cache = n_pages * PAGE_SIZE
    tail = kbuf.shape[0] - n_cache
    wtail = wkbuf.shape[0] - n_state

    def group_rows(ref, g, n):
        return ref[0, pl.ds(g, n, stride=N_KV), :].astype(BF16)

    def new_rows(ref, n):
        return _pad_rows(ref[0].astype(F32), n).astype(BF16)

    for i in range(n_pages):
        for g in range(N_KV):
            cols = slice(g * HEAD_DIM, (g + 1) * HEAD_DIM)
            kbuf[i * PAGE_SIZE:(i + 1) * PAGE_SIZE, cols] = group_rows(kpages[i], g, PAGE_SIZE)
            vbuf[i * PAGE_SIZE:(i + 1) * PAGE_SIZE, cols] = group_rows(vpages[i], g, PAGE_SIZE)
    kbuf[n_cache:, :] = new_rows(ksn_ref, tail)
    vbuf[n_cache:, :] = new_rows(vsn_ref, tail)
    for g in range(N_KV):
        cols = slice(g * HEAD_DIM, (g + 1) * HEAD_DIM)
        wkbuf[0:n_state, cols] = group_rows(kst_ref, g, n_state)
        wvbuf[0:n_state, cols] = group_rows(vst_ref, g, n_state)
    wkbuf[n_state:, :] = new_rows(kwn_ref, wtail)
    wvbuf[n_state:, :] = new_rows(vwn_ref, wtail)
    keep = okw_ref.shape[1] - tq * N_KV
    okw_ref[0, 0:keep, :] = kst_ref[0, n_state * N_KV - keep:, :]
    okw_ref[0, keep:, :] = kwl_ref[0]
    ovw_ref[0, 0:keep, :] = vst_ref[0, n_state * N_KV - keep:, :]
    ovw_ref[0, keep:, :] = vwl_ref[0]

    q = q_ref[0].astype(F32)
    zero = jnp.zeros((tq, HEAD_DIM), F32)
    qrows = []
    for g in range(N_KV):
        for h in range(HG):
            piece = q[:, (g * HG + h) * HEAD_DIM:(g * HG + h + 1) * HEAD_DIM]
            qrows.append(jnp.concatenate([piece if gg == g else zero for gg in range(N_KV)], axis=1))
    qbd = jnp.concatenate(qrows, axis=0).astype(BF16)

    p_c = _softmax_rows(_dot_nt(qbd, kc_ref[0]) + bc_ref[...])
    o_cmp = _dot(p_c.astype(BF16), vc_ref[0])
    reps = []
    for g in range(N_KV):
        s = p_c[g * HG * tq:g * HG * tq + tq, :]
        for h in range(1, HG):
            s = s + p_c[(g * HG + h) * tq:(g * HG + h + 1) * tq, :]
        reps.extend([s] * HG)
    psum = jnp.concatenate(reps, axis=0)
    p_hi = psum.astype(BF16)
    p_lo = (psum - p_hi.astype(F32)).astype(BF16)
    imp_t = _dot_nt(c2st_ref[...], p_hi) + _dot_nt(c2st_ref[...], p_lo)

    nsp = -(-ns // 8) * 8
    imp_t = imp_t[0:nsp, :]
    j = lax.broadcasted_iota(jnp.int32, imp_t.shape, 0)
    tt = lax.broadcasted_iota(jnp.int32, imp_t.shape, 1) % tq
    cur = (PAST_LEN + tt) // L_SLC
    forced = (j == 0) | (j == cur) | (j == cur - 1)
    score = jnp.where(j <= cur, imp_t + jnp.where(forced, FORCE_BONUS, 0.0), NEG)
    score = jnp.where(j < ns, score, 2.0 * NEG)
    sel_t = jnp.where(_select_topk(score, n_sel), 1.0, 0.0)
    sel = _pad_rows(sel_t, e_ref.shape[0]).T.astype(BF16)
    mask = _dot(sel, e_ref[...])

    s_s = _dot_nt(qbd, kbuf[...]) + bs_ref[...] + jnp.where(mask > 0.5, 0.0, NEG)
    o_slc = _dot(_softmax_rows(s_s).astype(BF16), vbuf[...])

    s_w = _dot_nt(qbd, wkbuf[...]) + bw_ref[...]
    o_win = _dot(_softmax_rows(s_w).astype(BF16), wvbuf[...])

    gates = jax.nn.sigmoid(g_ref[0])
    o_full = gates[:, 0:1] * o_cmp + gates[:, 1:2] * o_slc + gates[:, 2:3] * o_win
    pieces = []
    for g in range(N_KV):
        for h in range(HG):
            r0 = (g * HG + h) * tq
            pieces.append(o_full[r0:r0 + tq, g * HEAD_DIM:(g + 1) * HEAD_DIM])
    o_ref[0] = jnp.concatenate(pieces, axis=1).astype(BF16)


def _attn_sample(page_table, pool_k, pool_v, q3, gcol, kc, vc, ksn, vsn, kwn, vwn, kwl, vwl, kst, vst,
                 bc, bs, bw, c2st, expand, *, ns):
    bsz, n_pages = page_table.shape
    tq = q3.shape[1]
    n_state = kst.shape[1] // N_KV
    n_keys = bs.shape[1]
    n_wkeys = bw.shape[1]
    rows = N_HEADS * tq
    wb_new = min(WINDOW, n_state + tq)
    page_specs = [pl.BlockSpec((1, PAGE_SIZE * N_KV, HEAD_DIM),
                               functools.partial(lambda b, pt, i: (pt[b, i], 0, 0), i=i)) for i in range(n_pages)]
    per_b = lambda shape: pl.BlockSpec((1,) + shape, lambda b, pt: (b, 0, 0))
    const = lambda arr: pl.BlockSpec(arr.shape, lambda b, pt: (0, 0))
    in_specs = page_specs + page_specs + [
        per_b((tq, D_MODEL)), per_b((rows, 3)),
        per_b(kc.shape[1:]), per_b(vc.shape[1:]),
        per_b((tq, KV_W)), per_b((tq, KV_W)), per_b((tq, KV_W)), per_b((tq, KV_W)),
        per_b((tq * N_KV, HEAD_DIM)), per_b((tq * N_KV, HEAD_DIM)),
        per_b((n_state * N_KV, HEAD_DIM)), per_b((n_state * N_KV, HEAD_DIM)),
        const(bc), const(bs), const(bw), const(c2st), const(expand),
    ]
    return pl.pallas_call(
        functools.partial(_attn_sample_kernel, n_pages=n_pages, tq=tq, n_sel=min(N_SEL, ns), ns=ns, n_state=n_state),
        grid_spec=pltpu.PrefetchScalarGridSpec(
            num_scalar_prefetch=1,
            grid=(bsz,),
            in_specs=in_specs,
            out_specs=[per_b((tq, D_MODEL)), per_b((wb_new * N_KV, HEAD_DIM)), per_b((wb_new * N_KV, HEAD_DIM))],
            scratch_shapes=[
                pltpu.VMEM((n_keys, KV_W), BF16), pltpu.VMEM((n_keys, KV_W), BF16),
                pltpu.VMEM((n_wkeys, KV_W), BF16), pltpu.VMEM((n_wkeys, KV_W), BF16),
            ],
        ),
        out_shape=[
            jax.ShapeDtypeStruct((bsz, tq, D_MODEL), BF16),
            jax.ShapeDtypeStruct((bsz, wb_new * N_KV, HEAD_DIM), F32),
            jax.ShapeDtypeStruct((bsz, wb_new * N_KV, HEAD_DIM), F32),
        ],
        compiler_params=_params("parallel"),
        name="attn_sample",
    )(page_table, *([pool_k] * n_pages), *([pool_v] * n_pages), q3, gcol, kc, vc, ksn, vsn, kwn, vwn, kwl, vwl,
      kst, vst, bc, bs, bw, c2st, expand)


def _t5_bucket(dist):
    n = jnp.maximum(dist, 0)
    max_exact = N_BUCKETS // 2
    nf = jnp.maximum(n, 1).astype(F32)
    large = max_exact + (jnp.log(nf / max_exact) / math.log(MAX_DIST / max_exact) * (N_BUCKETS - max_exact)).astype(jnp.int32)
    large = jnp.minimum(large, N_BUCKETS - 1)
    return jnp.where(n < max_exact, n, large)


def _bias_by_dist(rel_bias):
    return rel_bias.astype(F32)[_t5_bucket(jnp.arange(MAX_DIST + 1))]


def _bias_tile(btab, dist, valid, shift):
    onehot = (jnp.clip(dist, 0, MAX_DIST)[..., None] == jnp.arange(MAX_DIST + 1)).astype(F32)
    vals = jnp.einsum("...d,dh->h...", onehot, (btab - shift[None, :]) * LOG2E, precision=lax.Precision.HIGHEST)
    return jnp.where(valid[None], vals, NEG)


def _heads_to_lanes(tile):
    h, k, q = tile.shape
    return tile.reshape(N_KV, HG, k, q).transpose(0, 2, 1, 3).reshape(N_KV, k, HG * q)


def _cmp_to_slc_t(nc_pad, nc, ns, ns_pad):
    cs = jnp.arange(nc_pad)[None, :] * STRIDE
    js = jnp.arange(ns_pad)[:, None] * L_SLC
    ov = jnp.clip(jnp.minimum(cs + L_CMP, js + L_SLC) - jnp.maximum(cs, js), 0, None)
    ok = (jnp.arange(nc_pad)[None, :] < nc) & (jnp.arange(ns_pad)[:, None] < ns)
    return jnp.where(ok, ov.astype(F32) / L_CMP, 0.0).astype(BF16)


def _prompt_attention(q, graw, kc, vc, ksb, vsb, kwb, vwb, btab, *, bsz, t):
    nq = t // Q_BLOCK
    nch = t // STRIDE
    nc = (t - L_CMP) // STRIDE + 1
    ns = -(-t // L_SLC)
    nwin = WINDOW // Q_BLOCK
    far = btab[MAX_DIST]
    none = jnp.zeros_like(far)
    graw_t = graw[:, :3 * N_HEADS].reshape(bsz, t, N_KV, HG, 3).transpose(0, 2, 4, 3, 1)

    def by_group(a):
        return a.reshape(bsz, t, N_KV, HEAD_DIM).transpose(0, 2, 1, 3)

    def transposed_tiles(a, tile):
        return a.reshape(bsz, N_KV, a.shape[2] // tile, tile, HEAD_DIM).transpose(0, 1, 2, 4, 3)

    ks, vs, kw, vw = (by_group(a) for a in (ksb, vsb, kwb, vwb))
    block_onehot = (jnp.arange(t)[:, None] // L_SLC == jnp.arange(128)[None, :]).astype(BF16)
    kaug = jnp.concatenate([ks, jnp.broadcast_to(block_onehot, (bsz, N_KV, t, 128))], axis=-1)
    t_far = -(-t // FAR_CHUNK) * FAR_CHUNK
    kaug = jnp.pad(kaug, ((0, 0), (0, 0), (0, t_far - t), (0, 0)))
    vst1 = transposed_tiles(vs, Q_BLOCK)
    vst4 = transposed_tiles(jnp.pad(vs, ((0, 0), (0, 0), (0, t_far - t), (0, 0))), FAR_CHUNK)
    front = ((0, 0), (0, 0), (WINDOW, 0), (0, 0))
    kwp = jnp.pad(kw, front)
    vwt = transposed_tiles(jnp.pad(vw, front), Q_BLOCK)
    vct = jnp.swapaxes(vc, 1, 2)

    kk = jnp.arange(Q_BLOCK)[:, None]
    qq = jnp.arange(Q_BLOCK)[None, :]
    always = jnp.ones((Q_BLOCK, Q_BLOCK), bool)
    t0 = _heads_to_lanes(_bias_tile(btab, qq - kk, qq - kk >= 0, far))
    t1 = _heads_to_lanes(_bias_tile(btab, Q_BLOCK + qq - kk, always, far))
    oldest = jnp.where(kk > qq, 0.0, NEG).astype(F32)
    wadd = jnp.concatenate([jnp.broadcast_to(jnp.tile(oldest, (1, HG)), (N_KV, Q_BLOCK, LANES)),
                            jnp.zeros((N_KV, (nwin - 2) * Q_BLOCK, LANES), F32), t1, t0], axis=1)
    nshift = (Q_BLOCK // STRIDE) * (nq - 1)
    lo = min(nshift, Q_BLOCK // STRIDE + MAX_DIST // STRIDE)
    cprime = jnp.arange(-lo, Q_BLOCK // STRIDE)[:, None]
    dist_c = qq - STRIDE * cprime - (L_CMP - 1)
    pattern = _heads_to_lanes(_bias_tile(btab, dist_c, dist_c >= 0, none))
    far_rows = jnp.broadcast_to(jnp.repeat(far * LOG2E, Q_BLOCK).reshape(N_KV, 1, LANES), (N_KV, nshift - lo, LANES))
    strip = jnp.concatenate([far_rows, pattern, jnp.full((N_KV, nch - Q_BLOCK // STRIDE, LANES), NEG, F32)], axis=1)
    c2st = _cmp_to_slc_t(nch, nc, ns, -(-ns // 8) * 8)
    return _attn_prompt(q, graw_t, kc, vct, kaug, vst1, vst4, kwp, vwt, strip, t0, t1, wadd, c2st, bsz=bsz, t=t)


def _sample_attention(q, graw, kc, vc, new_bf16, new_rows, page_table, pool_k, pool_v, state_k, state_v, btab,
                      *, dbsz, dt):
    d = D_MODEL
    n_pages = page_table.shape[1]
    n_pool = pool_k.shape[0]
    n_state = state_k.shape[1]
    nch = n_pages * (PAGE_SIZE // STRIDE)
    t_all = PAST_LEN + dt
    nc = (t_all - L_CMP) // STRIDE + 1
    ns = -(-t_all // L_SLC)
    n_keys = -(-(ns * L_SLC) // 128) * 128
    n_wkeys = -(-(n_state + dt) // 128) * 128
    far = btab[MAX_DIST]
    none = jnp.zeros_like(far)
    gcol = graw[:, :3 * N_HEADS].reshape(dbsz, dt, N_HEADS, 3).transpose(0, 2, 1, 3).reshape(dbsz, N_HEADS * dt, 3)

    qpos = (PAST_LEN + jnp.arange(dt))[:, None]
    cc = jnp.arange(nch)[None, :]
    dist = qpos - (cc * STRIDE + L_CMP - 1)
    bc = _bias_tile(btab, dist, (dist >= 0) & (cc < nc), none).reshape(N_HEADS * dt, nch)
    near0 = (PAST_LEN - MAX_DIST) // 128 * 128
    kpos = jnp.arange(near0, n_keys)[None, :]
    dist = qpos - kpos
    bs = jnp.concatenate([jnp.zeros((N_HEADS * dt, near0), F32),
                          _bias_tile(btab, dist, dist >= 0, far).reshape(N_HEADS * dt, n_keys - near0)], axis=1)
    wi = jnp.arange(n_wkeys)[None, :]
    dist = qpos - (PAST_LEN - n_state + wi)
    bw = _bias_tile(btab, dist, (dist >= 0) & (dist < WINDOW) & (wi < n_state + dt), none).reshape(N_HEADS * dt, n_wkeys)
    c2st = _cmp_to_slc_t(nch, nc, ns, 128)
    expand = (jnp.arange(128)[:, None] == (jnp.arange(n_keys)[None, :] // L_SLC)).astype(BF16)

    ksn, vsn, kwn, vwn = (a.reshape(dbsz, dt, KV_W) for a in new_bf16)
    kwl, vwl = (a.reshape(dbsz, dt * N_KV, HEAD_DIM) for a in new_rows)
    o, sk_win, sv_win = _attn_sample(
        page_table, pool_k.reshape(n_pool, PAGE_SIZE * N_KV, HEAD_DIM), pool_v.reshape(n_pool, PAGE_SIZE * N_KV, HEAD_DIM),
        q.reshape(dbsz, dt, d), gcol, kc, vc, ksn, vsn, kwn, vwn, kwl, vwl,
        state_k.reshape(dbsz, n_state * N_KV, HEAD_DIM), state_v.reshape(dbsz, n_state * N_KV, HEAD_DIM),
        bc, bs, bw, c2st, expand, ns=ns)
    wb = sk_win.shape[1] // N_KV
    return (o.reshape(dbsz * dt, d), sk_win.reshape(dbsz, wb, N_KV, HEAD_DIM), sv_win.reshape(dbsz, wb, N_KV, HEAD_DIM))


def _row(v):
    return v.reshape(1, -1).astype(F32)


def kernel(x_prompt, x_sample, cache_k_cmp, cache_v_cmp, cache_k_slc, cache_v_slc, state_k_win, state_v_win, state_conv, page_table, rel_bias, conv_w_pw1, conv_b_pw1, conv_w_dw, conv_b_dw, conv_ln_g, conv_ln_b, conv_w_pw2, conv_b_pw2, nsa_w_qg, nsa_w_o, w_kv, cmp_pe, cmp_w1, cmp_b1, cmp_w2, cmp_b2, ffn_w_in, ffn_w_out, ln_mix_g, ln_mix_b, ln_ffn_g, ln_ffn_b):
    d = D_MODEL
    d_ff = ffn_w_out.shape[1]
    bsz, t, _ = x_prompt.shape
    dbsz, dt, _ = x_sample.shape
    zeros_d = jnp.zeros((1, d), F32)
    zeros_ff = jnp.zeros((1, 2 * d_ff), F32)

    w_pw1 = conv_w_pw1[0].astype(BF16)
    w_pw2 = conv_w_pw2[0].astype(BF16)
    w_in = ffn_w_in.astype(BF16)
    w_out = ffn_w_out.astype(BF16)
    w_kvb = w_kv.astype(BF16)
    w_qg = jnp.pad(nsa_w_qg[0], ((0, 0), (0, d + 128 - nsa_w_qg.shape[2]))).astype(BF16)
    w_o = nsa_w_o[0].astype(BF16)
    w_dw = jnp.pad(conv_w_dw[0], ((0, CONV_HEAD - CONV_W), (0, 0)))

    half = STRIDE * HEAD_DIM
    cmp_ops = []
    for i in range(2):
        w1b = cmp_w1[i].astype(BF16)
        wcat = jnp.concatenate([w1b[:half], w1b[half:]], axis=1)
        pe8 = jnp.broadcast_to(cmp_pe[i].reshape(1, -1), (8, L_CMP * HEAD_DIM)).astype(BF16)
        pe_w1 = _mm_plain(pe8, w1b)[0:1]
        cmp_ops.append((wcat, pe_w1 + _row(cmp_b1[i]), cmp_w2[i].astype(BF16), _row(cmp_b2[i])))

    btab = _bias_by_dist(rel_bias)

    def ffn(xf, xb, layer):
        m = xf.shape[0]
        h = _mm_pair(xb, w_in[layer], zeros_ff, act="swiglu", out_dtype=BF16, tm=min(m, 1024), tn=512)
        return _mm_res_ln(h, w_out[layer], zeros_d, xf, _row(ln_ffn_g[layer]), _row(ln_ffn_b[layer]),
                          tm=512, tk=d_ff // 4)

    def trunk_front(x, state32, tm_conv):
        b_, t_, _ = x.shape
        m = b_ * t_
        xb = x.reshape(m, d).astype(BF16)
        u = _mm_pair(xb, w_pw1, _row(conv_b_pw1[0]), act="glu", out_dtype=F32, tm=min(m, 1024), tn=512)
        c = _conv_ln_silu(state32, u.reshape(b_, t_, d), w_dw, _row(conv_b_dw[0]), _row(conv_ln_g[0]),
                          _row(conv_ln_b[0]), tm=tm_conv)
        x1, x1b = _mm_res_ln(c.reshape(m, d), w_pw2, _row(conv_b_pw2[0]), x.reshape(m, d),
                             _row(ln_mix_g[0]), _row(ln_mix_b[0]), tm=512, tk=d // 2)
        x2, x2b = ffn(x1, x1b, 0)
        return x2, x2b, u

    def trunk_back(x2, o):
        x3, x3b = _mm_res_ln(o, w_o, zeros_d, x2, _row(ln_mix_g[1]), _row(ln_mix_b[1]), tm=512, tk=d // 2)
        y, _ = ffn(x3, x3b, 1)
        return y

    as4 = lambda a, b_, t_: a.reshape(b_, t_, N_KV, HEAD_DIM)

    x2, x2b, u_p = trunk_front(x_prompt, jnp.zeros((bsz, CONV_HEAD, d), F32), 64)
    kv = _mm_kv(x2b, w_kvb, tm=256)
    kcr, vcr, ksr, vsr, kwr, vwr = kv[:N_KV_OUT]
    ksb, vsb, kwb, vwb = kv[N_KV_OUT:]
    nch_p = t // STRIDE
    ident = jnp.arange(bsz, dtype=jnp.int32).reshape(bsz, 1)
    kc_p = _compress(kcr.reshape(bsz, t * N_KV, HEAD_DIM), ident, *cmp_ops[0], pages_per_row=1, ch_per_page=nch_p)
    vc_p = _compress(vcr.reshape(bsz, t * N_KV, HEAD_DIM), ident, *cmp_ops[1], pages_per_row=1, ch_per_page=nch_p)
    q_p, graw_p = _mm_qg(x2b, w_qg, tm=512)
    o_p = _prompt_attention(q_p, graw_p, kc_p, vc_p, ksb, vsb, kwb, vwb, btab, bsz=bsz, t=t)
    y_prompt = trunk_back(x2, o_p).reshape(bsz, t, d)

    wb_p = min(WINDOW, t)
    pk_cmp, pv_cmp, pk_slc, pv_slc = (as4(a, bsz, t) for a in (kcr, vcr, ksr, vsr))
    pk_win = as4(kwr, bsz, t)[:, t - wb_p:]
    pv_win = as4(vwr, bsz, t)[:, t - wb_p:]
    p_conv = u_p.reshape(bsz, t, d)[:, t - (CONV_W - 1):][None]

    st0 = state_conv[0]
    state32 = jnp.pad(st0, ((0, 0), (CONV_HEAD - (CONV_W - 1), 0), (0, 0)))
    s2, s2b, u_s = trunk_front(x_sample, state32, dt)
    kv = _mm_kv(s2b, w_kvb, tm=256)
    skc, svc, sks, svs, skw, svw = kv[:N_KV_OUT]
    n_pages = page_table.shape[1]
    n_pool = cache_k_cmp.shape[0]
    ch_pp = PAGE_SIZE // STRIDE
    kc_s = _compress(cache_k_cmp.reshape(n_pool, PAGE_SIZE * N_KV, HEAD_DIM), page_table, *cmp_ops[0],
                     pages_per_row=n_pages, ch_per_page=ch_pp)
    vc_s = _compress(cache_v_cmp.reshape(n_pool, PAGE_SIZE * N_KV, HEAD_DIM), page_table, *cmp_ops[1],
                     pages_per_row=n_pages, ch_per_page=ch_pp)
    q_s, graw_s = _mm_qg(s2b, w_qg, tm=512)
    o_s, sk_win, sv_win = _sample_attention(q_s, graw_s, kc_s, vc_s, kv[N_KV_OUT:], (skw, svw), page_table,
                                            cache_k_slc, cache_v_slc, state_k_win, state_v_win, btab,
                                            dbsz=dbsz, dt=dt)
    y_sample = trunk_back(s2, o_s).reshape(dbsz, dt, d)

    sk_cmp, sv_cmp, sk_slc, sv_slc = (as4(a, dbsz, dt) for a in (skc, svc, sks, svs))
    s_conv = jnp.concatenate([st0, u_s.reshape(dbsz, dt, d)], axis=1)[:, -(CONV_W - 1):][None]

    return (y_prompt, y_sample, pk_cmp, pv_cmp, pk_slc, pv_slc, pk_win, pv_win, p_conv,
            sk_cmp, sv_cmp, sk_slc, sv_slc, sk_win, sv_win, s_conv)
```

```python
import functools
import math

import jax
import jax.numpy as jnp
from jax import lax
from jax.experimental import pallas as pl
from jax.experimental.pallas import tpu as pltpu

D_MODEL = 2048
PAST_LEN = 2048
PAGE_SIZE = 128
N_HEADS = 16
HEAD_DIM = D_MODEL // N_HEADS
N_KV = 4
HG = N_HEADS // N_KV
L_CMP = 32
STRIDE = 16
CMP_HID = HEAD_DIM
L_SLC = 64
N_SEL = 16
WINDOW = 512
Q_BLOCK = 128
CONV_W = 31
N_BUCKETS = 32
MAX_DIST = 128
DEPTH = 2
ALPHA = (2 * DEPTH) ** 0.25
LN_EPS = 1e-5
NEG = -1e30
FORCE_BONUS = 1e4
LOG2E = math.log2(math.e)

KV_W = N_KV * HEAD_DIM
LANES = HG * Q_BLOCK
FAR_CHUNK = 8 * Q_BLOCK
VMEM_LIMIT = 56 * 1024 * 1024

F32 = jnp.float32
BF16 = jnp.bfloat16


def _params(*sem):
    return pltpu.CompilerParams(dimension_semantics=sem, vmem_limit_bytes=VMEM_LIMIT)


def _dot(a, b):
    return jnp.dot(a, b, preferred_element_type=F32)


def _dot_nt(a, b):
    return lax.dot_general(a, b, (((1,), (1,)), ((), ())), preferred_element_type=F32)


def _layer_norm(x, g, b):
    mu = jnp.mean(x, axis=-1, keepdims=True)
    xc = x - mu
    var = jnp.mean(xc * xc, axis=-1, keepdims=True)
    return xc * lax.rsqrt(var + LN_EPS) * g + b


def _silu(x):
    return x * jax.nn.sigmoid(x)


def _gelu_tanh(x):
    c = math.sqrt(2.0 / math.pi)
    return 0.5 * x * (1.0 + jnp.tanh(c * (x + 0.044715 * (x * x * x))))


def _pad_rows(x, n):
    if x.shape[0] == n:
        return x
    return jnp.concatenate([x, jnp.zeros((n - x.shape[0], x.shape[1]), x.dtype)], axis=0)


def _mm_pair_kernel(x_ref, wa_ref, wb_ref, ba_ref, bb_ref, o_ref, *, act):
    x = x_ref[...]
    a = _dot(x, wa_ref[...]) + ba_ref[...]
    b = _dot(x, wb_ref[...]) + bb_ref[...]
    if act == "glu":
        o = a * jax.nn.sigmoid(b)
    else:
        o = _silu(a) * b
    o_ref[...] = o.astype(o_ref.dtype)


def _mm_pair(x, w, bias, *, act, out_dtype, tm, tn):
    m, k = x.shape
    n = w.shape[1] // 2
    nj = n // tn
    return pl.pallas_call(
        functools.partial(_mm_pair_kernel, act=act),
        grid=(m // tm, nj),
        in_specs=[
            pl.BlockSpec((tm, k), lambda i, j: (i, 0)),
            pl.BlockSpec((k, tn), lambda i, j: (0, j)),
            pl.BlockSpec((k, tn), lambda i, j: (0, j + nj)),
            pl.BlockSpec((1, tn), lambda i, j: (0, j)),
            pl.BlockSpec((1, tn), lambda i, j: (0, j + nj)),
        ],
        out_specs=pl.BlockSpec((tm, tn), lambda i, j: (i, j)),
        out_shape=jax.ShapeDtypeStruct((m, n), out_dtype),
        compiler_params=_params("parallel", "parallel"),
        name="mm_pair_" + act,
    )(x, w, w, bias, bias)


LN_ROWS = 128


def _mm_res_ln_kernel(h_ref, w_ref, bias_ref, xres_ref, g_ref, b_ref, of_ref, ob_ref):
    kk = pl.program_id(1)

    @pl.when(kk == 0)
    def _():
        of_ref[...] = jnp.zeros_like(of_ref)

    of_ref[...] += _dot(h_ref[...], w_ref[...])

    @pl.when(kk == pl.num_programs(1) - 1)
    def _():
        step = min(LN_ROWS, of_ref.shape[0])
        for r0 in range(0, of_ref.shape[0], step):
            rows = slice(r0, r0 + step)
            y = ALPHA * xres_ref[rows, :] + (of_ref[rows, :] + bias_ref[...])
            o = _layer_norm(y, g_ref[...], b_ref[...])
            of_ref[rows, :] = o
            ob_ref[rows, :] = o.astype(BF16)


def _mm_res_ln(h, w, bias, xres, g, b, *, tm, tk):
    m, k = h.shape
    d = w.shape[1]
    row = lambda i, kk: (i, 0)
    vec = lambda i, kk: (0, 0)
    return pl.pallas_call(
        _mm_res_ln_kernel,
        grid=(m // tm, k // tk),
        in_specs=[
            pl.BlockSpec((tm, tk), lambda i, kk: (i, kk)),
            pl.BlockSpec((tk, d), lambda i, kk: (kk, 0)),
            pl.BlockSpec((1, d), vec),
            pl.BlockSpec((tm, d), row, pipeline_mode=pl.Buffered(1)),
            pl.BlockSpec((1, d), vec),
            pl.BlockSpec((1, d), vec),
        ],
        out_specs=[pl.BlockSpec((tm, d), row), pl.BlockSpec((tm, d), row)],
        out_shape=[jax.ShapeDtypeStruct((m, d), F32), jax.ShapeDtypeStruct((m, d), BF16)],
        compiler_params=_params("parallel", "arbitrary"),
        name="mm_res_ln",
    )(h, w, bias, xres, g, b)


N_KV_OUT = 6
N_KV_BF16 = 4


def _mm_kv_kernel(x_ref, w_ref, *out_refs, tm):
    x = x_ref[...]
    for idx in range(N_KV_OUT):
        acc = _dot(x, w_ref[:, idx * KV_W:(idx + 1) * KV_W])
        for g in range(N_KV):
            out_refs[idx][pl.ds(g, tm, stride=N_KV), :] = acc[:, g * HEAD_DIM:(g + 1) * HEAD_DIM]
        first_bf16 = N_KV_OUT - N_KV_BF16
        if idx >= first_bf16:
            out_refs[N_KV_OUT + idx - first_bf16][...] = acc.astype(BF16)


def _mm_kv(x, w, *, tm):
    m, k = x.shape
    row = lambda i: (i, 0)
    return pl.pallas_call(
        functools.partial(_mm_kv_kernel, tm=tm),
        grid=(m // tm,),
        in_specs=[pl.BlockSpec((tm, k), row), pl.BlockSpec((k, N_KV_OUT * KV_W), lambda i: (0, 0))],
        out_specs=[pl.BlockSpec((tm * N_KV, HEAD_DIM), row)] * N_KV_OUT + [pl.BlockSpec((tm, KV_W), row)] * N_KV_BF16,
        out_shape=([jax.ShapeDtypeStruct((m * N_KV, HEAD_DIM), F32)] * N_KV_OUT
                   + [jax.ShapeDtypeStruct((m, KV_W), BF16)] * N_KV_BF16),
        compiler_params=_params("parallel"),
        name="mm_kv",
    )(x, w)


def _mm_qg_kernel(x_ref, w_ref, q_ref, g_ref):
    x = x_ref[...]
    scale = HEAD_DIM ** -0.5 * LOG2E
    for c in range(D_MODEL // 512):
        acc = _dot(x, w_ref[:, c * 512:(c + 1) * 512])
        q_ref[:, c * 512:(c + 1) * 512] = (acc * scale).astype(BF16)
    g_ref[...] = _dot(x, w_ref[:, D_MODEL:])


def _mm_qg(x, w, *, tm):
    m, k = x.shape
    row = lambda i: (i, 0)
    return pl.pallas_call(
        _mm_qg_kernel,
        grid=(m // tm,),
        in_specs=[pl.BlockSpec((tm, k), row), pl.BlockSpec((k, D_MODEL + 128), lambda i: (0, 0))],
        out_specs=[pl.BlockSpec((tm, D_MODEL), row), pl.BlockSpec((tm, 128), row)],
        out_shape=[jax.ShapeDtypeStruct((m, D_MODEL), BF16), jax.ShapeDtypeStruct((m, 128), F32)],
        compiler_params=_params("parallel"),
        name="mm_qg",
    )(x, w)


def _mm_plain_kernel(x_ref, w_ref, o_ref):
    o_ref[...] = _dot(x_ref[...], w_ref[...])


def _mm_plain(x, w):
    return pl.pallas_call(
        _mm_plain_kernel,
        out_shape=jax.ShapeDtypeStruct((x.shape[0], w.shape[1]), F32),
        compiler_params=pltpu.CompilerParams(vmem_limit_bytes=VMEM_LIMIT),
        name="mm_plain",
    )(x, w)


CONV_HIST = CONV_W - 1
CONV_HEAD = 32
CONV_CB = 512
SUBLANES = 8
LANE_TILE = 128


def _conv_kernel(*refs, tm, has_prev):
    if has_prev:
        state_ref, uprev_ref, ucur_ref, w_ref, bdw_ref, g_ref, b_ref, o_ref, z_ref, zs_ref, y_ref = refs
    else:
        state_ref, ucur_ref, w_ref, bdw_ref, g_ref, b_ref, o_ref, z_ref, zs_ref, y_ref = refs
    i = pl.program_id(1)
    pad = CONV_HEAD - CONV_HIST
    rows = CONV_HEAD + tm
    total = rows + SUBLANES

    @pl.when(i == 0)
    def _():
        z_ref[0:pad, :] = jnp.zeros((pad, D_MODEL), F32)
        z_ref[pad:CONV_HEAD, :] = state_ref[0]

    if has_prev:
        @pl.when(i > 0)
        def _():
            z_ref[0:CONV_HEAD, :] = uprev_ref[0]

    z_ref[CONV_HEAD:rows, :] = ucur_ref[0]
    z_ref[rows:total, :] = jnp.zeros((SUBLANES, D_MODEL), F32)
    for c0 in range(0, D_MODEL, CONV_CB):
        zc = z_ref[:, c0:c0 + CONV_CB]
        zs_ref[0, :, c0:c0 + CONV_CB] = zc[0:rows]
        for b in range(1, SUBLANES):
            zs_ref[b, :, c0:c0 + CONV_CB] = pltpu.roll(zc, total - b, 0)[0:rows]
    for c0 in range(0, D_MODEL, LANE_TILE):
        cols = slice(c0, c0 + LANE_TILE)
        taps = [w_ref[w * SUBLANES:(w + 1) * SUBLANES, cols] for w in range(CONV_W)]
        for r0 in range(0, tm, SUBLANES):
            acc = None
            for w in range(CONV_W):
                shift, base = (pad + w) % SUBLANES, (pad + w) // SUBLANES * SUBLANES
                term = zs_ref[shift, r0 + base:r0 + base + SUBLANES, cols] * taps[w]
                acc = term if acc is None else acc + term
            y_ref[r0:r0 + SUBLANES, cols] = acc
    y = _layer_norm(y_ref[...] + bdw_ref[...], g_ref[...], b_ref[...])
    o_ref[0] = _silu(y).astype(BF16)


def _conv_ln_silu(state, u, w_dw, b_dw, g, b, *, tm):
    bsz, t, d = u.shape
    has_prev = t > tm
    per = tm // CONV_HEAD
    vec = lambda bb, i: (0, 0)
    prev = [pl.BlockSpec((1, CONV_HEAD, d), lambda bb, i: (bb, jnp.maximum(i * per - 1, 0), 0))] if has_prev else []
    rows = CONV_HEAD + tm
    return pl.pallas_call(
        functools.partial(_conv_kernel, tm=tm, has_prev=has_prev),
        grid=(bsz, t // tm),
        in_specs=[pl.BlockSpec((1, CONV_HIST, d), lambda bb, i: (bb, 0, 0))] + prev + [
            pl.BlockSpec((1, tm, d), lambda bb, i: (bb, i, 0)),
            pl.BlockSpec((CONV_W * SUBLANES, d), vec),
            pl.BlockSpec((1, d), vec),
            pl.BlockSpec((1, d), vec),
            pl.BlockSpec((1, d), vec),
        ],
        out_specs=pl.BlockSpec((1, tm, d), lambda bb, i: (bb, i, 0)),
        out_shape=jax.ShapeDtypeStruct((bsz, t, d), BF16),
        scratch_shapes=[pltpu.VMEM((rows + SUBLANES, d), F32), pltpu.VMEM((SUBLANES, rows, d), F32),
                        pltpu.VMEM((tm, d), F32)],
        compiler_params=_params("parallel", "arbitrary"),
        name="conv_ln_silu",
    )(state, *([u] if has_prev else []), u, w_dw, b_dw, g, b)


CHUNK_ROWS = STRIDE * N_KV
CHUNK_TILES = CHUNK_ROWS // SUBLANES
TOK_PER_TILE = SUBLANES // N_KV


def _compress_kernel(*refs, n_in, ch_pp):
    x_refs = refs[1:1 + n_in]
    w1k_ref, bias1_ref, w2_ref, b2_ref, o_ref = refs[1 + n_in:]
    nch = n_in * ch_pp
    rows = nch * SUBLANES
    pages = [r[0].reshape(ch_pp, CHUNK_TILES, SUBLANES, HEAD_DIM) for r in x_refs]
    lhs = []
    for k in range(CHUNK_TILES):
        tiles = [p[:, k].reshape(ch_pp * SUBLANES, HEAD_DIM) for p in pages]
        lhs.append((jnp.concatenate(tiles, axis=0) if n_in > 1 else tiles[0]).astype(BF16))
    acc = _dot(jnp.concatenate(lhs, axis=1), w1k_ref[...])
    width = 2 * CMP_HID
    comb = acc[:, 0:width]
    for jj in range(1, TOK_PER_TILE):
        comb = comb + pltpu.roll(acc[:, jj * width:(jj + 1) * width], rows - jj * N_KV, 0)
    lo = comb[:, :CMP_HID]
    hi = comb[:, CMP_HID:]
    hi_next = pltpu.roll(hi, rows - SUBLANES, 0)
    h = _gelu_tanh(lo + hi_next + bias1_ref[...])
    out = _dot(h.astype(BF16), w2_ref[...]) + b2_ref[...]
    for c in range(nch):
        o_ref[0, c * N_KV:(c + 1) * N_KV, :] = out[c * SUBLANES:c * SUBLANES + N_KV, :]


def _compress(rows2d, page_table, w1k, bias1, w2, b2, *, pages_per_row, ch_per_page):
    bsz = page_table.shape[0]
    nch = pages_per_row * ch_per_page
    const2 = lambda bb, pt: (0, 0)
    page_rows = ch_per_page * CHUNK_ROWS
    in_specs = [pl.BlockSpec((1, page_rows, HEAD_DIM), functools.partial(lambda bb, pt, i: (pt[bb, i], 0, 0), i=i))
                for i in range(pages_per_row)]
    in_specs += [
        pl.BlockSpec(w1k.shape, const2),
        pl.BlockSpec((1, CMP_HID), const2),
        pl.BlockSpec((CMP_HID, HEAD_DIM), const2),
        pl.BlockSpec((1, HEAD_DIM), const2),
    ]
    return pl.pallas_call(
        functools.partial(_compress_kernel, n_in=pages_per_row, ch_pp=ch_per_page),
        grid_spec=pltpu.PrefetchScalarGridSpec(
            num_scalar_prefetch=1,
            grid=(bsz,),
            in_specs=in_specs,
            out_specs=pl.BlockSpec((1, nch * N_KV, HEAD_DIM), lambda bb, pt: (bb, 0, 0)),
        ),
        out_shape=jax.ShapeDtypeStruct((bsz, nch * N_KV, HEAD_DIM), F32),
        compiler_params=_params("parallel"),
        name="compress",
    )(page_table, *([rows2d] * pages_per_row), w1k, bias1, w2, b2)


def _compress_weights(w1):
    w1b = w1.astype(BF16).reshape(2, STRIDE, HEAD_DIM, CMP_HID)
    per_tok = jnp.concatenate([w1b[0], w1b[1]], axis=-1)
    per_tile = per_tok.reshape(CHUNK_TILES, TOK_PER_TILE, HEAD_DIM, 2 * CMP_HID).transpose(0, 2, 1, 3)
    return per_tile.reshape(CHUNK_TILES * HEAD_DIM, TOK_PER_TILE * 2 * CMP_HID)


def _select_topk(score, n_sel):
    jn = score.shape[0]
    sub = lax.broadcasted_iota(jnp.int32, (8, score.shape[1]), 0)
    blks = [score[8 * v:8 * v + 8, :] for v in range(jn // 8)]
    cnt = [jnp.zeros(b.shape, F32) for b in blks]
    for jp in range(jn):
        row = score[jp:jp + 1, :]
        for v, blk in enumerate(blks):
            gt = jnp.where(row > blk, 1.0, 0.0)
            ge = jnp.where(row >= blk, 1.0, 0.0)
            if 8 * v + 8 <= jp:
                beats = gt
            elif 8 * v > jp:
                beats = ge
            else:
                beats = jnp.where(sub > jp % 8, ge, gt)
            cnt[v] = cnt[v] + beats
    return jnp.concatenate(cnt, axis=0) < n_sel


def _col_max(x):
    return jnp.max(x, axis=0, keepdims=True)


def _col_sum(x):
    return jnp.sum(x, axis=0, keepdims=True)


def _attn_prompt_kernel(q_ref, graw_ref, kc_ref, vct_ref, kaug_ref, vst1_ref, vst4_ref, kwp_ref, vwt_ref,
                        strip_ref, t0_ref, t1_ref, wadd_ref, c2st_ref, o_ref, sfar_ref,
                        *, n_sel, nc_pad, strip_zero, nwin):
    qb = pl.program_id(2)
    q = q_ref[...]
    qs = jnp.concatenate([q[:, h * HEAD_DIM:(h + 1) * HEAD_DIM] for h in range(HG)], axis=0)

    kw = kwp_ref[0, 0, pl.ds(pl.multiple_of(qb * Q_BLOCK, Q_BLOCK), (nwin + 1) * Q_BLOCK), :]
    sw = _dot_nt(kw, qs)
    sw = sw + wadd_ref[0, 0]
    pw = jnp.exp2(sw - _col_max(sw))
    acc_w = _dot(vwt_ref[0, 0, qb], pw[0:Q_BLOCK, :].astype(BF16))
    for dd in range(1, nwin + 1):
        acc_w = acc_w + _dot(vwt_ref[0, 0, qb + dd], pw[dd * Q_BLOCK:(dd + 1) * Q_BLOCK, :].astype(BF16))
    o_win = acc_w * (1.0 / jnp.maximum(_col_sum(pw), 1e-30))

    off = pl.multiple_of(strip_zero - (Q_BLOCK // STRIDE) * qb, 8)
    st = _dot_nt(kc_ref[0], qs) + strip_ref[0, pl.ds(off, nc_pad), :]
    m = _col_max(st)
    m = jnp.where(m < 0.1 * NEG, 0.0, m)
    e = jnp.exp2(st - m)
    p = e * (1.0 / jnp.maximum(_col_sum(e), 1e-30))
    o_cmp = _dot(vct_ref[0], p.astype(BF16))
    psum = p[:, 0:Q_BLOCK]
    for h in range(1, HG):
        psum = psum + p[:, h * Q_BLOCK:(h + 1) * Q_BLOCK]
    p_hi = psum.astype(BF16)
    p_lo = (psum - p_hi.astype(F32)).astype(BF16)
    imp = _dot(c2st_ref[...], p_hi) + _dot(c2st_ref[...], p_lo)

    ns = imp.shape[0]
    j = lax.broadcasted_iota(jnp.int32, imp.shape, 0)
    cur = (qb * Q_BLOCK + lax.broadcasted_iota(jnp.int32, imp.shape, 1)) // L_SLC
    forced = (j == 0) | (j == cur) | (j == cur - 1)
    score = jnp.where(j <= cur, imp + jnp.where(forced, FORCE_BONUS, 0.0), NEG)
    sel = _select_topk(score, n_sel)

    per = Q_BLOCK // L_SLC

    def with_mask(keep):
        mk = _pad_rows(jnp.where(keep, 0.0, NEG), Q_BLOCK).T
        return jnp.concatenate([qs, jnp.concatenate([mk] * HG, axis=0).astype(BF16)], axis=1)

    qa_near = with_mask(sel)
    qa_far = with_mask(sel & (j < per * (qb - 1)))

    n_far = (jnp.maximum(qb - 1, 0) + FAR_CHUNK // Q_BLOCK - 1) // (FAR_CHUNK // Q_BLOCK)

    def far_scores(c, mx):
        r0 = pl.multiple_of(c * FAR_CHUNK, FAR_CHUNK)
        s = _dot_nt(kaug_ref[0, 0, pl.ds(r0, FAR_CHUNK), :], qa_far)
        sfar_ref[pl.ds(r0, FAR_CHUNK), :] = s
        return jnp.maximum(mx, _col_max(s))

    mx = lax.fori_loop(0, n_far, far_scores, jnp.full((1, LANES), 2.0 * NEG, F32))
    kb1 = jnp.maximum(qb - 1, 0)
    s1 = _dot_nt(kaug_ref[0, 0, pl.ds(pl.multiple_of(kb1 * Q_BLOCK, Q_BLOCK), Q_BLOCK), :], qa_near)
    s1 = s1 + jnp.where(qb >= 1, t1_ref[0], NEG)
    s0 = _dot_nt(kaug_ref[0, 0, pl.ds(pl.multiple_of(qb * Q_BLOCK, Q_BLOCK), Q_BLOCK), :], qa_near) + t0_ref[0]
    mx = jnp.maximum(mx, jnp.maximum(_col_max(s1), _col_max(s0)))

    def far_pv(c, carry):
        l, acc = carry
        r0 = pl.multiple_of(c * FAR_CHUNK, FAR_CHUNK)
        pc = jnp.exp2(sfar_ref[pl.ds(r0, FAR_CHUNK), :] - mx)
        return l + _col_sum(pc), acc + _dot(vst4_ref[0, 0, c], pc.astype(BF16))

    l_s, acc_s = lax.fori_loop(0, n_far, far_pv, (jnp.zeros((1, LANES), F32), jnp.zeros((HEAD_DIM, LANES), F32)))
    p1 = jnp.exp2(s1 - mx)
    p0 = jnp.exp2(s0 - mx)
    l_s = l_s + _col_sum(p1) + _col_sum(p0)
    acc_s = acc_s + _dot(vst1_ref[0, 0, kb1], p1.astype(BF16)) + _dot(vst1_ref[0, 0, qb], p0.astype(BF16))
    o_slc = acc_s * (1.0 / jnp.maximum(l_s, 1e-30))

    gates = jax.nn.sigmoid(graw_ref[0, 0])
    gl = [jnp.concatenate([gates[i, h:h + 1, :] for h in range(HG)], axis=1) for i in range(3)]
    o_t = gl[0] * o_cmp + gl[1] * o_slc + gl[2] * o_win
    for h in range(HG):
        o_ref[:, h * HEAD_DIM:(h + 1) * HEAD_DIM] = o_t[:, h * Q_BLOCK:(h + 1) * Q_BLOCK].T.astype(BF16)


def _attn_prompt(q, graw_t, kc, vct, kaug, vst1, vst4, kwp, vwt, strip, t0, t1, wadd, c2st, *, bsz, t):
    nq = t // Q_BLOCK
    nc_pad = kc.shape[1]
    ns = c2st.shape[0]
    nwin = WINDOW // Q_BLOCK
    per_bg = lambda arr: pl.BlockSpec((1, 1) + arr.shape[2:], lambda b, g, i: (b, g) + (0,) * (arr.ndim - 2))
    per_g = lambda arr: pl.BlockSpec((1,) + arr.shape[1:], lambda b, g, i: (g,) + (0,) * (arr.ndim - 1))
    return pl.pallas_call(
        functools.partial(_attn_prompt_kernel, n_sel=min(N_SEL, ns), nc_pad=nc_pad,
                          strip_zero=strip.shape[1] - nc_pad, nwin=nwin),
        grid=(bsz, N_KV, nq),
        in_specs=[
            pl.BlockSpec((Q_BLOCK, HG * HEAD_DIM), lambda b, g, i: (b * nq + i, g)),
            pl.BlockSpec((1, 1, 3, HG, Q_BLOCK), lambda b, g, i: (b, g, 0, 0, i)),
            pl.BlockSpec((1, nc_pad, HEAD_DIM), lambda b, g, i: (b, 0, g)),
            pl.BlockSpec((1, HEAD_DIM, nc_pad), lambda b, g, i: (b, g, 0)),
            per_bg(kaug), per_bg(vst1), per_bg(vst4), per_bg(kwp), per_bg(vwt),
            per_g(strip), per_g(t0), per_g(t1),
            pl.BlockSpec((1, 1) + wadd.shape[2:], lambda b, g, i: (g, jnp.minimum(i, nwin), 0, 0)),
            pl.BlockSpec(c2st.shape, lambda b, g, i: (0, 0)),
        ],
        out_specs=pl.BlockSpec((Q_BLOCK, HG * HEAD_DIM), lambda b, g, i: (b * nq + i, g)),
        out_shape=jax.ShapeDtypeStruct((bsz * t, D_MODEL), BF16),
        scratch_shapes=[pltpu.VMEM((kaug.shape[2], LANES), F32)],
        compiler_params=_params("parallel", "parallel", "arbitrary"),
        name="attn_prompt",
    )(q, graw_t, kc, vct, kaug, vst1, vst4, kwp, vwt, strip, t0, t1, wadd, c2st)


def _softmax_rows(s):
    m = jnp.max(s, axis=1, keepdims=True)
    m = jnp.where(m < 0.1 * NEG, 0.0, m)
    e = jnp.exp2(s - m)
    return e * (1.0 / jnp.maximum(jnp.sum(e, axis=1, keepdims=True), 1e-30))


def _attn_sample_kernel(*refs, n_pages, tq, n_sel, ns, n_state):
    kpages = refs[1:1 + n_pages]
    vpages = refs[1 + n_pages:1 + 2 * n_pages]
    (q_ref, g_ref, kc_ref, vc_ref, ksn_ref, vsn_ref, kwn_ref, vwn_ref, kwl_ref, vwl_ref, kst_ref, vst_ref,
     bc_ref, bs_ref, bw_ref, c2st_ref, e_ref,
     o_ref, okw_ref, ovw_ref, kbuf, vbuf, wkbuf, wvbuf) = refs[1 + 2 * n_pages:]
    n_cache = n_pages * PAGE_SIZE
    tail = kbuf.shape[0] - n_cache
    wtail = wkbuf.shape[0] - n_state

    def group_rows(ref, g, n):
        return ref[0, pl.ds(g, n, stride=N_KV), :].astype(BF16)

    def new_rows(ref, n):
        return _pad_rows(ref[0].astype(F32), n).astype(BF16)

    for i in range(n_pages):
        for g in range(N_KV):
            cols = slice(g * HEAD_DIM, (g + 1) * HEAD_DIM)
            kbuf[i * PAGE_SIZE:(i + 1) * PAGE_SIZE, cols] = group_rows(kpages[i], g, PAGE_SIZE)
            vbuf[i * PAGE_SIZE:(i + 1) * PAGE_SIZE, cols] = group_rows(vpages[i], g, PAGE_SIZE)
    kbuf[n_cache:, :] = new_rows(ksn_ref, tail)
    vbuf[n_cache:, :] = new_rows(vsn_ref, tail)
    for g in range(N_KV):
        cols = slice(g * HEAD_DIM, (g + 1) * HEAD_DIM)
        wkbuf[0:n_state, cols] = group_rows(kst_ref, g, n_state)
        wvbuf[0:n_state, cols] = group_rows(vst_ref, g, n_state)
    wkbuf[n_state:, :] = new_rows(kwn_ref, wtail)
    wvbuf[n_state:, :] = new_rows(vwn_ref, wtail)
    keep = okw_ref.shape[1] - tq * N_KV
    okw_ref[0, 0:keep, :] = kst_ref[0, n_state * N_KV - keep:, :]
    okw_ref[0, keep:, :] = kwl_ref[0]
    ovw_ref[0, 0:keep, :] = vst_ref[0, n_state * N_KV - keep:, :]
    ovw_ref[0, keep:, :] = vwl_ref[0]

    q = q_ref[0].astype(F32)
    zero = jnp.zeros((tq, HEAD_DIM), F32)
    qrows = []
    for g in range(N_KV):
        for h in range(HG):
            piece = q[:, (g * HG + h) * HEAD_DIM:(g * HG + h + 1) * HEAD_DIM]
            qrows.append(jnp.concatenate([piece if gg == g else zero for gg in range(N_KV)], axis=1))
    qbd = jnp.concatenate(qrows, axis=0).astype(BF16)

    n_cmp = kc_ref.shape[1] // N_KV
    kc = jnp.concatenate([group_rows(kc_ref, g, n_cmp) for g in range(N_KV)], axis=1)
    vc = jnp.concatenate([group_rows(vc_ref, g, n_cmp) for g in range(N_KV)], axis=1)
    p_c = _softmax_rows(_dot_nt(qbd, kc) + bc_ref[...])
    o_cmp = _dot(p_c.astype(BF16), vc)
    reps = []
    for g in range(N_KV):
        s = p_c[g * HG * tq:g * HG * tq + tq, :]
        for h in range(1, HG):
            s = s + p_c[(g * HG + h) * tq:(g * HG + h + 1) * tq, :]
        reps.extend([s] * HG)
    psum = jnp.concatenate(reps, axis=0)
    p_hi = psum.astype(BF16)
    p_lo = (psum - p_hi.astype(F32)).astype(BF16)
    imp_t = _dot_nt(c2st_ref[...], p_hi) + _dot_nt(c2st_ref[...], p_lo)

    nsp = -(-ns // 8) * 8
    imp_t = imp_t[0:nsp, :]
    j = lax.broadcasted_iota(jnp.int32, imp_t.shape, 0)
    tt = lax.broadcasted_iota(jnp.int32, imp_t.shape, 1) % tq
    cur = (PAST_LEN + tt) // L_SLC
    forced = (j == 0) | (j == cur) | (j == cur - 1)
    score = jnp.where(j <= cur, imp_t + jnp.where(forced, FORCE_BONUS, 0.0), NEG)
    score = jnp.where(j < ns, score, 2.0 * NEG)
    sel_t = jnp.where(_select_topk(score, n_sel), 1.0, 0.0)
    sel = _pad_rows(sel_t, e_ref.shape[0]).T.astype(BF16)
    mask = _dot(sel, e_ref[...])

    s_s = _dot_nt(qbd, kbuf[...]) + bs_ref[...] + jnp.where(mask > 0.5, 0.0, NEG)
    o_slc = _dot(_softmax_rows(s_s).astype(BF16), vbuf[...])

    s_w = _dot_nt(qbd, wkbuf[...]) + bw_ref[...]
    o_win = _dot(_softmax_rows(s_w).astype(BF16), wvbuf[...])

    gates = jax.nn.sigmoid(g_ref[0])
    o_full = gates[:, 0:1] * o_cmp + gates[:, 1:2] * o_slc + gates[:, 2:3] * o_win
    pieces = []
    for g in range(N_KV):
        for h in range(HG):
            r0 = (g * HG + h) * tq
            pieces.append(o_full[r0:r0 + tq, g * HEAD_DIM:(g + 1) * HEAD_DIM])
    o_ref[0] = jnp.concatenate(pieces, axis=1).astype(BF16)


def _attn_sample(page_table, pool_k, pool_v, q3, gcol, kc, vc, ksn, vsn, kwn, vwn, kwl, vwl, kst, vst,
                 bc, bs, bw, c2st, expand, *, ns):
    bsz, n_pages = page_table.shape
    tq = q3.shape[1]
    n_state = kst.shape[1] // N_KV
    n_keys = bs.shape[1]
    n_wkeys = bw.shape[1]
    rows = N_HEADS * tq
    wb_new = min(WINDOW, n_state + tq)
    page_specs = [pl.BlockSpec((1, PAGE_SIZE * N_KV, HEAD_DIM),
                               functools.partial(lambda b, pt, i: (pt[b, i], 0, 0), i=i)) for i in range(n_pages)]
    per_b = lambda shape: pl.BlockSpec((1,) + shape, lambda b, pt: (b, 0, 0))
    const = lambda arr: pl.BlockSpec(arr.shape, lambda b, pt: (0, 0))
    in_specs = page_specs + page_specs + [
        per_b((tq, D_MODEL)), per_b((rows, 3)),
        per_b(kc.shape[1:]), per_b(vc.shape[1:]),
        per_b((tq, KV_W)), per_b((tq, KV_W)), per_b((tq, KV_W)), per_b((tq, KV_W)),
        per_b((tq * N_KV, HEAD_DIM)), per_b((tq * N_KV, HEAD_DIM)),
        per_b((n_state * N_KV, HEAD_DIM)), per_b((n_state * N_KV, HEAD_DIM)),
        const(bc), const(bs), const(bw), const(c2st), const(expand),
    ]
    return pl.pallas_call(
        functools.partial(_attn_sample_kernel, n_pages=n_pages, tq=tq, n_sel=min(N_SEL, ns), ns=ns, n_state=n_state),
        grid_spec=pltpu.PrefetchScalarGridSpec(
            num_scalar_prefetch=1,
            grid=(bsz,),
            in_specs=in_specs,
            out_specs=[per_b((tq, D_MODEL)), per_b((wb_new * N_KV, HEAD_DIM)), per_b((wb_new * N_KV, HEAD_DIM))],
            scratch_shapes=[
                pltpu.VMEM((n_keys, KV_W), BF16), pltpu.VMEM((n_keys, KV_W), BF16),
                pltpu.VMEM((n_wkeys, KV_W), BF16), pltpu.VMEM((n_wkeys, KV_W), BF16),
            ],
        ),
        out_shape=[
            jax.ShapeDtypeStruct((bsz, tq, D_MODEL), BF16),
            jax.ShapeDtypeStruct((bsz, wb_new * N_KV, HEAD_DIM), F32),
            jax.ShapeDtypeStruct((bsz, wb_new * N_KV, HEAD_DIM), F32),
        ],
        compiler_params=_params("parallel"),
        name="attn_sample",
    )(page_table, *([pool_k] * n_pages), *([pool_v] * n_pages), q3, gcol, kc, vc, ksn, vsn, kwn, vwn, kwl, vwl,
      kst, vst, bc, bs, bw, c2st, expand)


def _t5_bucket(dist):
    n = jnp.maximum(dist, 0)
    max_exact = N_BUCKETS // 2
    nf = jnp.maximum(n, 1).astype(F32)
    large = max_exact + (jnp.log(nf / max_exact) / math.log(MAX_DIST / max_exact) * (N_BUCKETS - max_exact)).astype(jnp.int32)
    large = jnp.minimum(large, N_BUCKETS - 1)
    return jnp.where(n < max_exact, n, large)


def _bias_by_dist(rel_bias):
    return rel_bias.astype(F32)[_t5_bucket(jnp.arange(MAX_DIST + 1))]


def _bias_tile(btab, dist, valid, shift):
    onehot = (jnp.clip(dist, 0, MAX_DIST)[..., None] == jnp.arange(MAX_DIST + 1)).astype(F32)
    vals = jnp.einsum("...d,dh->h...", onehot, (btab - shift[None, :]) * LOG2E, precision=lax.Precision.HIGHEST)
    return jnp.where(valid[None], vals, NEG)


def _heads_to_lanes(tile):
    h, k, q = tile.shape
    return tile.reshape(N_KV, HG, k, q).transpose(0, 2, 1, 3).reshape(N_KV, k, HG * q)


def _cmp_to_slc_t(nc_pad, nc, ns, ns_pad):
    cs = jnp.arange(nc_pad)[None, :] * STRIDE
    js = jnp.arange(ns_pad)[:, None] * L_SLC
    ov = jnp.clip(jnp.minimum(cs + L_CMP, js + L_SLC) - jnp.maximum(cs, js), 0, None)
    ok = (jnp.arange(nc_pad)[None, :] < nc) & (jnp.arange(ns_pad)[:, None] < ns)
    return jnp.where(ok, ov.astype(F32) / L_CMP, 0.0).astype(BF16)


def _prompt_attention(q, graw, kc, vc, ksb, vsb, kwb, vwb, btab, *, bsz, t):
    nq = t // Q_BLOCK
    nch = t // STRIDE
    nc = (t - L_CMP) // STRIDE + 1
    ns = -(-t // L_SLC)
    nwin = WINDOW // Q_BLOCK
    far = btab[MAX_DIST]
    none = jnp.zeros_like(far)
    graw_t = graw[:, :3 * N_HEADS].reshape(bsz, t, N_KV, HG, 3).transpose(0, 2, 4, 3, 1)

    def by_group(a):
        return a.reshape(bsz, t, N_KV, HEAD_DIM).transpose(0, 2, 1, 3)

    def transposed_tiles(a, tile):
        return a.reshape(bsz, N_KV, a.shape[2] // tile, tile, HEAD_DIM).transpose(0, 1, 2, 4, 3)

    ks, vs, kw, vw = (by_group(a) for a in (ksb, vsb, kwb, vwb))
    block_onehot = (jnp.arange(t)[:, None] // L_SLC == jnp.arange(128)[None, :]).astype(BF16)
    kaug = jnp.concatenate([ks, jnp.broadcast_to(block_onehot, (bsz, N_KV, t, 128))], axis=-1)
    t_far = -(-t // FAR_CHUNK) * FAR_CHUNK
    kaug = jnp.pad(kaug, ((0, 0), (0, 0), (0, t_far - t), (0, 0)))
    vst1 = transposed_tiles(vs, Q_BLOCK)
    vst4 = transposed_tiles(jnp.pad(vs, ((0, 0), (0, 0), (0, t_far - t), (0, 0))), FAR_CHUNK)
    front = ((0, 0), (0, 0), (WINDOW, 0), (0, 0))
    kwp = jnp.pad(kw, front)
    vwt = transposed_tiles(jnp.pad(vw, front), Q_BLOCK)
    vct = jnp.swapaxes(vc, 1, 2)

    kk = jnp.arange(Q_BLOCK)[:, None]
    qq = jnp.arange(Q_BLOCK)[None, :]
    always = jnp.ones((Q_BLOCK, Q_BLOCK), bool)
    t0 = _heads_to_lanes(_bias_tile(btab, qq - kk, qq - kk >= 0, far))
    t1 = _heads_to_lanes(_bias_tile(btab, Q_BLOCK + qq - kk, always, far))
    oldest = jnp.where(kk > qq, 0.0, NEG).astype(F32)
    wadd = jnp.concatenate([jnp.broadcast_to(jnp.tile(oldest, (1, HG)), (N_KV, Q_BLOCK, LANES)),
                            jnp.zeros((N_KV, (nwin - 2) * Q_BLOCK, LANES), F32), t1, t0], axis=1)
    tile_of_row = jnp.arange((nwin + 1) * Q_BLOCK) // Q_BLOCK
    is_padding = tile_of_row[None, :] < nwin - jnp.arange(nwin + 1)[:, None]
    wadd = jnp.where(is_padding[None, :, :, None], NEG, wadd[:, None])
    nshift = (Q_BLOCK // STRIDE) * (nq - 1)
    lo = min(nshift, Q_BLOCK // STRIDE + MAX_DIST // STRIDE)
    cprime = jnp.arange(-lo, Q_BLOCK // STRIDE)[:, None]
    dist_c = qq - STRIDE * cprime - (L_CMP - 1)
    pattern = _heads_to_lanes(_bias_tile(btab, dist_c, dist_c >= 0, none))
    far_rows = jnp.broadcast_to(jnp.repeat(far * LOG2E, Q_BLOCK).reshape(N_KV, 1, LANES), (N_KV, nshift - lo, LANES))
    strip = jnp.concatenate([far_rows, pattern, jnp.full((N_KV, nch - Q_BLOCK // STRIDE, LANES), NEG, F32)], axis=1)
    c2st = _cmp_to_slc_t(nch, nc, ns, -(-ns // 8) * 8)
    return _attn_prompt(q, graw_t, kc, vct, kaug, vst1, vst4, kwp, vwt, strip, t0, t1, wadd, c2st, bsz=bsz, t=t)


def _sample_attention(q, graw, kc, vc, new_bf16, new_rows, page_table, pool_k, pool_v, state_k, state_v, btab,
                      *, dbsz, dt):
    d = D_MODEL
    n_pages = page_table.shape[1]
    n_pool = pool_k.shape[0]
    n_state = state_k.shape[1]
    nch = n_pages * (PAGE_SIZE // STRIDE)
    t_all = PAST_LEN + dt
    nc = (t_all - L_CMP) // STRIDE + 1
    ns = -(-t_all // L_SLC)
    n_keys = -(-(ns * L_SLC) // 128) * 128
    n_wkeys = -(-(n_state + dt) // 128) * 128
    far = btab[MAX_DIST]
    none = jnp.zeros_like(far)
    gcol = graw[:, :3 * N_HEADS].reshape(dbsz, dt, N_HEADS, 3).transpose(0, 2, 1, 3).reshape(dbsz, N_HEADS * dt, 3)

    qpos = (PAST_LEN + jnp.arange(dt))[:, None]
    cc = jnp.arange(nch)[None, :]
    dist = qpos - (cc * STRIDE + L_CMP - 1)
    bc = _bias_tile(btab, dist, (dist >= 0) & (cc < nc), none).reshape(N_HEADS * dt, nch)
    near0 = (PAST_LEN - MAX_DIST) // 128 * 128
    kpos = jnp.arange(near0, n_keys)[None, :]
    dist = qpos - kpos
    bs = jnp.concatenate([jnp.zeros((N_HEADS * dt, near0), F32),
                          _bias_tile(btab, dist, dist >= 0, far).reshape(N_HEADS * dt, n_keys - near0)], axis=1)
    wi = jnp.arange(n_wkeys)[None, :]
    dist = qpos - (PAST_LEN - n_state + wi)
    bw = _bias_tile(btab, dist, (dist >= 0) & (dist < WINDOW) & (wi < n_state + dt), none).reshape(N_HEADS * dt, n_wkeys)
    c2st = _cmp_to_slc_t(nch, nc, ns, 128)
    expand = (jnp.arange(128)[:, None] == (jnp.arange(n_keys)[None, :] // L_SLC)).astype(BF16)

    ksn, vsn, kwn, vwn = (a.reshape(dbsz, dt, KV_W) for a in new_bf16)
    kwl, vwl = (a.reshape(dbsz, dt * N_KV, HEAD_DIM) for a in new_rows)
    o, sk_win, sv_win = _attn_sample(
        page_table, pool_k.reshape(n_pool, PAGE_SIZE * N_KV, HEAD_DIM), pool_v.reshape(n_pool, PAGE_SIZE * N_KV, HEAD_DIM),
        q.reshape(dbsz, dt, d), gcol, kc, vc, ksn, vsn, kwn, vwn, kwl, vwl,
        state_k.reshape(dbsz, n_state * N_KV, HEAD_DIM), state_v.reshape(dbsz, n_state * N_KV, HEAD_DIM),
        bc, bs, bw, c2st, expand, ns=ns)
    wb = sk_win.shape[1] // N_KV
    return (o.reshape(dbsz * dt, d), sk_win.reshape(dbsz, wb, N_KV, HEAD_DIM), sv_win.reshape(dbsz, wb, N_KV, HEAD_DIM))


def _row(v):
    return v.reshape(1, -1).astype(F32)


def kernel(x_prompt, x_sample, cache_k_cmp, cache_v_cmp, cache_k_slc, cache_v_slc, state_k_win, state_v_win, state_conv, page_table, rel_bias, conv_w_pw1, conv_b_pw1, conv_w_dw, conv_b_dw, conv_ln_g, conv_ln_b, conv_w_pw2, conv_b_pw2, nsa_w_qg, nsa_w_o, w_kv, cmp_pe, cmp_w1, cmp_b1, cmp_w2, cmp_b2, ffn_w_in, ffn_w_out, ln_mix_g, ln_mix_b, ln_ffn_g, ln_ffn_b):
    d = D_MODEL
    d_ff = ffn_w_out.shape[1]
    bsz, t, _ = x_prompt.shape
    dbsz, dt, _ = x_sample.shape
    zeros_d = jnp.zeros((1, d), F32)
    zeros_ff = jnp.zeros((1, 2 * d_ff), F32)

    w_pw1 = conv_w_pw1[0].astype(BF16)
    w_pw2 = conv_w_pw2[0].astype(BF16)
    w_in = ffn_w_in.astype(BF16)
    w_out = ffn_w_out.astype(BF16)
    w_kvb = w_kv.astype(BF16)
    w_qg = jnp.pad(nsa_w_qg[0], ((0, 0), (0, d + 128 - nsa_w_qg.shape[2]))).astype(BF16)
    w_o = nsa_w_o[0].astype(BF16)
    w_dw = jnp.repeat(conv_w_dw[0], SUBLANES, axis=0)

    cmp_ops = []
    for i in range(2):
        pe8 = jnp.broadcast_to(cmp_pe[i].reshape(1, -1), (8, L_CMP * HEAD_DIM)).astype(BF16)
        pe_w1 = _mm_plain(pe8, cmp_w1[i].astype(BF16))[0:1]
        cmp_ops.append((_compress_weights(cmp_w1[i]), pe_w1 + _row(cmp_b1[i]), cmp_w2[i].astype(BF16),
                        _row(cmp_b2[i])))

    btab = _bias_by_dist(rel_bias)

    def ffn(xf, xb, layer):
        m = xf.shape[0]
        h = _mm_pair(xb, w_in[layer], zeros_ff, act="swiglu", out_dtype=BF16, tm=min(m, 1024), tn=512)
        return _mm_res_ln(h, w_out[layer], zeros_d, xf, _row(ln_ffn_g[layer]), _row(ln_ffn_b[layer]),
                          tm=min(m, 1024), tk=d_ff // 4)

    def trunk_front(x, state, tm_conv):
        b_, t_, _ = x.shape
        m = b_ * t_
        xb = x.reshape(m, d).astype(BF16)
        u = _mm_pair(xb, w_pw1, _row(conv_b_pw1[0]), act="glu", out_dtype=F32, tm=min(m, 1024), tn=512)
        c = _conv_ln_silu(state, u.reshape(b_, t_, d), w_dw, _row(conv_b_dw[0]), _row(conv_ln_g[0]),
                          _row(conv_ln_b[0]), tm=tm_conv)
        x1, x1b = _mm_res_ln(c.reshape(m, d), w_pw2, _row(conv_b_pw2[0]), x.reshape(m, d),
                             _row(ln_mix_g[0]), _row(ln_mix_b[0]), tm=min(m, 1024), tk=d // 2)
        x2, x2b = ffn(x1, x1b, 0)
        return x2, x2b, u

    def trunk_back(x2, o):
        x3, x3b = _mm_res_ln(o, w_o, zeros_d, x2, _row(ln_mix_g[1]), _row(ln_mix_b[1]), tm=min(x2.shape[0], 1024),
                             tk=d // 2)
        y, _ = ffn(x3, x3b, 1)
        return y

    as4 = lambda a, b_, t_: a.reshape(b_, t_, N_KV, HEAD_DIM)

    x2, x2b, u_p = trunk_front(x_prompt, jnp.zeros((bsz, CONV_HIST, d), F32), 128)
    kv = _mm_kv(x2b, w_kvb, tm=256)
    kcr, vcr, ksr, vsr, kwr, vwr = kv[:N_KV_OUT]
    ksb, vsb, kwb, vwb = kv[N_KV_OUT:]
    nch_p = t // STRIDE
    ident = jnp.arange(bsz, dtype=jnp.int32).reshape(bsz, 1)
    kc_p = _compress(kcr.reshape(bsz, t * N_KV, HEAD_DIM), ident, *cmp_ops[0], pages_per_row=1, ch_per_page=nch_p)
    vc_p = _compress(vcr.reshape(bsz, t * N_KV, HEAD_DIM), ident, *cmp_ops[1], pages_per_row=1, ch_per_page=nch_p)
    q_p, graw_p = _mm_qg(x2b, w_qg, tm=512)
    natural = lambda a: a.reshape(bsz, nch_p, KV_W).astype(BF16)
    o_p = _prompt_attention(q_p, graw_p, natural(kc_p), natural(vc_p), ksb, vsb, kwb, vwb, btab, bsz=bsz, t=t)
    y_prompt = trunk_back(x2, o_p).reshape(bsz, t, d)

    wb_p = min(WINDOW, t)
    pk_cmp, pv_cmp, pk_slc, pv_slc = (as4(a, bsz, t) for a in (kcr, vcr, ksr, vsr))
    pk_win = as4(kwr, bsz, t)[:, t - wb_p:]
    pv_win = as4(vwr, bsz, t)[:, t - wb_p:]
    p_conv = u_p.reshape(bsz, t, d)[:, t - (CONV_W - 1):][None]

    st0 = state_conv[0]
    s2, s2b, u_s = trunk_front(x_sample, st0, dt)
    kv = _mm_kv(s2b, w_kvb, tm=256)
    skc, svc, sks, svs, skw, svw = kv[:N_KV_OUT]
    n_pages = page_table.shape[1]
    n_pool = cache_k_cmp.shape[0]
    ch_pp = PAGE_SIZE // STRIDE
    kc_s = _compress(cache_k_cmp.reshape(n_pool, PAGE_SIZE * N_KV, HEAD_DIM), page_table, *cmp_ops[0],
                     pages_per_row=n_pages, ch_per_page=ch_pp)
    vc_s = _compress(cache_v_cmp.reshape(n_pool, PAGE_SIZE * N_KV, HEAD_DIM), page_table, *cmp_ops[1],
                     pages_per_row=n_pages, ch_per_page=ch_pp)
    q_s, graw_s = _mm_qg(s2b, w_qg, tm=512)
    o_s, sk_win, sv_win = _sample_attention(q_s, graw_s, kc_s, vc_s, kv[N_KV_OUT:], (skw, svw), page_table,
                                            cache_k_slc, cache_v_slc, state_k_win, state_v_win, btab,
                                            dbsz=dbsz, dt=dt)
    y_sample = trunk_back(s2, o_s).reshape(dbsz, dt, d)

    sk_cmp, sv_cmp, sk_slc, sv_slc = (as4(a, dbsz, dt) for a in (skc, svc, sks, svs))
    s_conv = jnp.concatenate([st0, u_s.reshape(dbsz, dt, d)], axis=1)[:, -(CONV_W - 1):][None]

    return (y_prompt, y_sample, pk_cmp, pv_cmp, pk_slc, pv_slc, pk_win, pv_win, p_conv,
            sk_cmp, sv_cmp, sk_slc, sv_slc, sk_win, sv_win, s_conv)
```

```python
import functools
import math

import jax
import jax.numpy as jnp
from jax import lax
from jax.experimental import pallas as pl
from jax.experimental.pallas import tpu as pltpu

D_MODEL = 2048
PAST_LEN = 2048
PAGE_SIZE = 128
N_HEADS = 16
HEAD_DIM = D_MODEL // N_HEADS
N_KV = 4
HG = N_HEADS // N_KV
L_CMP = 32
STRIDE = 16
CMP_HID = HEAD_DIM
L_SLC = 64
N_SEL = 16
WINDOW = 512
Q_BLOCK = 128
CONV_W = 31
N_BUCKETS = 32
MAX_DIST = 128
DEPTH = 2
ALPHA = (2 * DEPTH) ** 0.25
LN_EPS = 1e-5
NEG = -1e30
FORCE_BONUS = 1e4
LOG2E = math.log2(math.e)

KV_W = N_KV * HEAD_DIM
LANES = HG * Q_BLOCK
FAR_CHUNK = 8 * Q_BLOCK
VMEM_LIMIT = 56 * 1024 * 1024

F32 = jnp.float32
BF16 = jnp.bfloat16


def _params(*sem):
    return pltpu.CompilerParams(dimension_semantics=sem, vmem_limit_bytes=VMEM_LIMIT)


def _dot(a, b):
    return jnp.dot(a, b, preferred_element_type=F32)


def _dot_nt(a, b):
    return lax.dot_general(a, b, (((1,), (1,)), ((), ())), preferred_element_type=F32)


def _layer_norm(x, g, b):
    mu = jnp.mean(x, axis=-1, keepdims=True)
    xc = x - mu
    var = jnp.mean(xc * xc, axis=-1, keepdims=True)
    return xc * lax.rsqrt(var + LN_EPS) * g + b


def _silu(x):
    return x * jax.nn.sigmoid(x)


def _gelu_tanh(x):
    c = math.sqrt(2.0 / math.pi)
    return 0.5 * x * (1.0 + jnp.tanh(c * (x + 0.044715 * (x * x * x))))


def _pad_rows(x, n):
    if x.shape[0] == n:
        return x
    return jnp.concatenate([x, jnp.zeros((n - x.shape[0], x.shape[1]), x.dtype)], axis=0)


def _mm_pair_kernel(x_ref, wa_ref, wb_ref, ba_ref, bb_ref, o_ref, wa_bf, wb_bf, *, act):
    @pl.when(pl.program_id(1) == 0)
    def _():
        wa_bf[...] = wa_ref[...].astype(BF16)
        wb_bf[...] = wb_ref[...].astype(BF16)

    x = x_ref[...]
    a = _dot(x, wa_bf[...]) + ba_ref[...]
    b = _dot(x, wb_bf[...]) + bb_ref[...]
    if act == "glu":
        o = a * jax.nn.sigmoid(b)
    else:
        o = _silu(a) * b
    o_ref[...] = o.astype(o_ref.dtype)


def _mm_pair(x, w, layer, bias, *, act, out_dtype, tm, tn):
    m, k = x.shape
    n = w.shape[2] // 2
    nj = n // tn
    return pl.pallas_call(
        functools.partial(_mm_pair_kernel, act=act),
        grid=(nj, m // tm),
        in_specs=[
            pl.BlockSpec((tm, k), lambda j, i: (i, 0)),
            pl.BlockSpec((None, k, tn), lambda j, i: (layer, 0, j)),
            pl.BlockSpec((None, k, tn), lambda j, i: (layer, 0, j + nj)),
            pl.BlockSpec((1, tn), lambda j, i: (0, j)),
            pl.BlockSpec((1, tn), lambda j, i: (0, j + nj)),
        ],
        out_specs=pl.BlockSpec((tm, tn), lambda j, i: (i, j)),
        out_shape=jax.ShapeDtypeStruct((m, n), out_dtype),
        scratch_shapes=[pltpu.VMEM((k, tn), BF16), pltpu.VMEM((k, tn), BF16)],
        compiler_params=_params("parallel", "arbitrary"),
        name="mm_pair_" + act,
    )(x, w, w, bias, bias)


LN_ROWS = 128


def _mm_res_ln_kernel(h_ref, w_ref, bias_ref, xres_ref, g_ref, b_ref, of_ref, ob_ref, acc_ref):
    kk = pl.program_id(1)

    @pl.when(kk == 0)
    def _():
        acc_ref[...] = jnp.zeros_like(acc_ref)

    acc_ref[...] += _dot(h_ref[...], w_ref[...])

    @pl.when(kk == pl.num_programs(1) - 1)
    def _():
        step = min(LN_ROWS, of_ref.shape[0])
        for r0 in range(0, of_ref.shape[0], step):
            rows = slice(r0, r0 + step)
            y = ALPHA * xres_ref[rows, :] + (acc_ref[rows, :] + bias_ref[...])
            o = _layer_norm(y, g_ref[...], b_ref[...])
            of_ref[rows, :] = o
            ob_ref[rows, :] = o.astype(BF16)


def _mm_res_ln(h, w, bias, xres, g, b, *, tm, tk):
    m, k = h.shape
    d = w.shape[1]
    row = lambda i, kk: (i, 0)
    vec = lambda i, kk: (0, 0)
    return pl.pallas_call(
        _mm_res_ln_kernel,
        grid=(m // tm, k // tk),
        in_specs=[
            pl.BlockSpec((tm, tk), lambda i, kk: (i, kk)),
            pl.BlockSpec((tk, d), lambda i, kk: (kk, 0)),
            pl.BlockSpec((1, d), vec),
            pl.BlockSpec((tm, d), row),
            pl.BlockSpec((1, d), vec),
            pl.BlockSpec((1, d), vec),
        ],
        out_specs=[pl.BlockSpec((tm, d), row), pl.BlockSpec((tm, d), row)],
        out_shape=[jax.ShapeDtypeStruct((m, d), F32), jax.ShapeDtypeStruct((m, d), BF16)],
        scratch_shapes=[pltpu.VMEM((tm, d), F32)],
        compiler_params=_params("parallel", "arbitrary"),
        name="mm_res_ln",
    )(h, w, bias, xres, g, b)


N_KV_OUT = 6
N_KV_BF16 = 4


def _mm_kv_kernel(x_ref, w_ref, *out_refs, tm):
    x = x_ref[...]
    for idx in range(N_KV_OUT):
        acc = _dot(x, w_ref[:, idx * KV_W:(idx + 1) * KV_W])
        for g in range(N_KV):
            out_refs[idx][pl.ds(g, tm, stride=N_KV), :] = acc[:, g * HEAD_DIM:(g + 1) * HEAD_DIM]
        first_bf16 = N_KV_OUT - N_KV_BF16
        if idx >= first_bf16:
            out_refs[N_KV_OUT + idx - first_bf16][...] = acc.astype(BF16)


def _mm_kv(x, w, *, tm):
    m, k = x.shape
    row = lambda i: (i, 0)
    return pl.pallas_call(
        functools.partial(_mm_kv_kernel, tm=tm),
        grid=(m // tm,),
        in_specs=[pl.BlockSpec((tm, k), row), pl.BlockSpec((k, N_KV_OUT * KV_W), lambda i: (0, 0))],
        out_specs=[pl.BlockSpec((tm * N_KV, HEAD_DIM), row)] * N_KV_OUT + [pl.BlockSpec((tm, KV_W), row)] * N_KV_BF16,
        out_shape=([jax.ShapeDtypeStruct((m * N_KV, HEAD_DIM), F32)] * N_KV_OUT
                   + [jax.ShapeDtypeStruct((m, KV_W), BF16)] * N_KV_BF16),
        compiler_params=_params("parallel"),
        name="mm_kv",
    )(x, w)


def _mm_qg_kernel(x_ref, w_ref, q_ref, g_ref):
    x = x_ref[...]
    scale = HEAD_DIM ** -0.5 * LOG2E
    for c in range(D_MODEL // 512):
        acc = _dot(x, w_ref[:, c * 512:(c + 1) * 512])
        q_ref[:, c * 512:(c + 1) * 512] = (acc * scale).astype(BF16)
    g_ref[...] = _dot(x, w_ref[:, D_MODEL:])


def _mm_qg(x, w, *, tm):
    m, k = x.shape
    row = lambda i: (i, 0)
    return pl.pallas_call(
        _mm_qg_kernel,
        grid=(m // tm,),
        in_specs=[pl.BlockSpec((tm, k), row), pl.BlockSpec((k, D_MODEL + 128), lambda i: (0, 0))],
        out_specs=[pl.BlockSpec((tm, D_MODEL), row), pl.BlockSpec((tm, 128), row)],
        out_shape=[jax.ShapeDtypeStruct((m, D_MODEL), BF16), jax.ShapeDtypeStruct((m, 128), F32)],
        compiler_params=_params("parallel"),
        name="mm_qg",
    )(x, w)


def _mm_plain_kernel(x_ref, w_ref, o_ref):
    o_ref[...] = _dot(x_ref[...], w_ref[...])


def _mm_plain(x, w):
    return pl.pallas_call(
        _mm_plain_kernel,
        out_shape=jax.ShapeDtypeStruct((x.shape[0], w.shape[1]), F32),
        compiler_params=pltpu.CompilerParams(vmem_limit_bytes=VMEM_LIMIT),
        name="mm_plain",
    )(x, w)


CONV_HIST = CONV_W - 1
CONV_HEAD = 32
CONV_CB = 512
SUBLANES = 8
LANE_TILE = 128


def _conv_kernel(*refs, tm, has_prev):
    if has_prev:
        state_ref, uprev_ref, ucur_ref, w_ref, bdw_ref, g_ref, b_ref, o_ref, z_ref, zs_ref, y_ref = refs
    else:
        state_ref, ucur_ref, w_ref, bdw_ref, g_ref, b_ref, o_ref, z_ref, zs_ref, y_ref = refs
    i = pl.program_id(1)
    pad = CONV_HEAD - CONV_HIST
    rows = CONV_HEAD + tm
    total = rows + SUBLANES

    @pl.when(i == 0)
    def _():
        z_ref[0:pad, :] = jnp.zeros((pad, D_MODEL), F32)
        z_ref[pad:CONV_HEAD, :] = state_ref[0]

    if has_prev:
        @pl.when(i > 0)
        def _():
            z_ref[0:CONV_HEAD, :] = uprev_ref[0]

    z_ref[CONV_HEAD:rows, :] = ucur_ref[0]
    z_ref[rows:total, :] = jnp.zeros((SUBLANES, D_MODEL), F32)
    for c0 in range(0, D_MODEL, CONV_CB):
        zc = z_ref[:, c0:c0 + CONV_CB]
        zs_ref[0, :, c0:c0 + CONV_CB] = zc[0:rows]
        for b in range(1, SUBLANES):
            zs_ref[b, :, c0:c0 + CONV_CB] = pltpu.roll(zc, total - b, 0)[0:rows]
    for c0 in range(0, D_MODEL, LANE_TILE):
        cols = slice(c0, c0 + LANE_TILE)
        taps = [w_ref[w * SUBLANES:(w + 1) * SUBLANES, cols] for w in range(CONV_W)]
        for r0 in range(0, tm, SUBLANES):
            acc = None
            for w in range(CONV_W):
                shift, base = (pad + w) % SUBLANES, (pad + w) // SUBLANES * SUBLANES
                term = zs_ref[shift, r0 + base:r0 + base + SUBLANES, cols] * taps[w]
                acc = term if acc is None else acc + term
            y_ref[r0:r0 + SUBLANES, cols] = acc
    y = _layer_norm(y_ref[...] + bdw_ref[...], g_ref[...], b_ref[...])
    o_ref[0] = _silu(y).astype(BF16)


def _conv_ln_silu(state, u, w_dw, b_dw, g, b, *, tm):
    bsz, t, d = u.shape
    has_prev = t > tm
    per = tm // CONV_HEAD
    vec = lambda bb, i: (0, 0)
    prev = [pl.BlockSpec((1, CONV_HEAD, d), lambda bb, i: (bb, jnp.maximum(i * per - 1, 0), 0))] if has_prev else []
    rows = CONV_HEAD + tm
    return pl.pallas_call(
        functools.partial(_conv_kernel, tm=tm, has_prev=has_prev),
        grid=(bsz, t // tm),
        in_specs=[pl.BlockSpec((1, CONV_HIST, d), lambda bb, i: (bb, 0, 0))] + prev + [
            pl.BlockSpec((1, tm, d), lambda bb, i: (bb, i, 0)),
            pl.BlockSpec((CONV_W * SUBLANES, d), vec),
            pl.BlockSpec((1, d), vec),
            pl.BlockSpec((1, d), vec),
            pl.BlockSpec((1, d), vec),
        ],
        out_specs=pl.BlockSpec((1, tm, d), lambda bb, i: (bb, i, 0)),
        out_shape=jax.ShapeDtypeStruct((bsz, t, d), BF16),
        scratch_shapes=[pltpu.VMEM((rows + SUBLANES, d), F32), pltpu.VMEM((SUBLANES, rows, d), F32),
                        pltpu.VMEM((tm, d), F32)],
        compiler_params=_params("parallel", "arbitrary"),
        name="conv_ln_silu",
    )(state, *([u] if has_prev else []), u, w_dw, b_dw, g, b)


CHUNK_ROWS = STRIDE * N_KV
CHUNK_TILES = CHUNK_ROWS // SUBLANES
TOK_PER_TILE = SUBLANES // N_KV


def _compress_kernel(*refs, n_in, ch_pp):
    x_refs = refs[1:1 + n_in]
    w1k_ref, bias1_ref, w2_ref, b2_ref, o_ref = refs[1 + n_in:]
    nch = n_in * ch_pp
    rows = nch * SUBLANES
    pages = [r[0].reshape(ch_pp, CHUNK_TILES, SUBLANES, HEAD_DIM) for r in x_refs]
    lhs = []
    for k in range(CHUNK_TILES):
        tiles = [p[:, k].reshape(ch_pp * SUBLANES, HEAD_DIM) for p in pages]
        lhs.append((jnp.concatenate(tiles, axis=0) if n_in > 1 else tiles[0]).astype(BF16))
    acc = _dot(jnp.concatenate(lhs, axis=1), w1k_ref[...])
    width = 2 * CMP_HID
    comb = acc[:, 0:width]
    for jj in range(1, TOK_PER_TILE):
        comb = comb + pltpu.roll(acc[:, jj * width:(jj + 1) * width], rows - jj * N_KV, 0)
    lo = comb[:, :CMP_HID]
    hi = comb[:, CMP_HID:]
    hi_next = pltpu.roll(hi, rows - SUBLANES, 0)
    h = _gelu_tanh(lo + hi_next + bias1_ref[...])
    out = _dot(h.astype(BF16), w2_ref[...]) + b2_ref[...]
    for c in range(nch):
        o_ref[0, c * N_KV:(c + 1) * N_KV, :] = out[c * SUBLANES:c * SUBLANES + N_KV, :]


def _compress(rows2d, page_table, w1k, bias1, w2, b2, *, pages_per_row, ch_per_page):
    bsz = page_table.shape[0]
    nch = pages_per_row * ch_per_page
    const2 = lambda bb, pt: (0, 0)
    page_rows = ch_per_page * CHUNK_ROWS
    in_specs = [pl.BlockSpec((1, page_rows, HEAD_DIM), functools.partial(lambda bb, pt, i: (pt[bb, i], 0, 0), i=i))
                for i in range(pages_per_row)]
    in_specs += [
        pl.BlockSpec(w1k.shape, const2),
        pl.BlockSpec((1, CMP_HID), const2),
        pl.BlockSpec((CMP_HID, HEAD_DIM), const2),
        pl.BlockSpec((1, HEAD_DIM), const2),
    ]
    return pl.pallas_call(
        functools.partial(_compress_kernel, n_in=pages_per_row, ch_pp=ch_per_page),
        grid_spec=pltpu.PrefetchScalarGridSpec(
            num_scalar_prefetch=1,
            grid=(bsz,),
            in_specs=in_specs,
            out_specs=pl.BlockSpec((1, nch * N_KV, HEAD_DIM), lambda bb, pt: (bb, 0, 0)),
        ),
        out_shape=jax.ShapeDtypeStruct((bsz, nch * N_KV, HEAD_DIM), F32),
        compiler_params=_params("parallel"),
        name="compress",
    )(page_table, *([rows2d] * pages_per_row), w1k, bias1, w2, b2)


def _compress_weights(w1):
    w1b = w1.astype(BF16).reshape(2, STRIDE, HEAD_DIM, CMP_HID)
    per_tok = jnp.concatenate([w1b[0], w1b[1]], axis=-1)
    per_tile = per_tok.reshape(CHUNK_TILES, TOK_PER_TILE, HEAD_DIM, 2 * CMP_HID).transpose(0, 2, 1, 3)
    return per_tile.reshape(CHUNK_TILES * HEAD_DIM, TOK_PER_TILE * 2 * CMP_HID)


def _select_topk(score, n_sel):
    jn = score.shape[0]
    sub = lax.broadcasted_iota(jnp.int32, (8, score.shape[1]), 0)
    blks = [score[8 * v:8 * v + 8, :] for v in range(jn // 8)]
    cnt = [jnp.zeros(b.shape, F32) for b in blks]
    for jp in range(jn):
        row = score[jp:jp + 1, :]
        for v, blk in enumerate(blks):
            gt = jnp.where(row > blk, 1.0, 0.0)
            ge = jnp.where(row >= blk, 1.0, 0.0)
            if 8 * v + 8 <= jp:
                beats = gt
            elif 8 * v > jp:
                beats = ge
            else:
                beats = jnp.where(sub > jp % 8, ge, gt)
            cnt[v] = cnt[v] + beats
    return jnp.concatenate(cnt, axis=0) < n_sel


def _col_max(x):
    return jnp.max(x, axis=0, keepdims=True)


def _col_sum(x):
    return jnp.sum(x, axis=0, keepdims=True)


def _attn_prompt_kernel(q_ref, graw_ref, kc_ref, vct_ref, kaug_ref, vst1_ref, vst4_ref, kwp_ref, vwt_ref,
                        strip_ref, t0_ref, t1_ref, wadd_ref, c2st_ref, o_ref,
                        *, n_sel, nc_pad, strip_zero, nwin):
    qb = pl.program_id(2)
    q = q_ref[...]
    qs = jnp.concatenate([q[:, h * HEAD_DIM:(h + 1) * HEAD_DIM] for h in range(HG)], axis=0)

    kw = kwp_ref[0, 0, pl.ds(pl.multiple_of(qb * Q_BLOCK, Q_BLOCK), (nwin + 1) * Q_BLOCK), :]
    sw = _dot_nt(kw, qs)
    sw = sw + wadd_ref[0, 0]
    pw = jnp.exp2(sw - _col_max(sw))
    acc_w = _dot(vwt_ref[0, 0, qb], pw[0:Q_BLOCK, :].astype(BF16))
    for dd in range(1, nwin + 1):
        acc_w = acc_w + _dot(vwt_ref[0, 0, qb + dd], pw[dd * Q_BLOCK:(dd + 1) * Q_BLOCK, :].astype(BF16))
    o_win = acc_w * (1.0 / jnp.maximum(_col_sum(pw), 1e-30))

    off = pl.multiple_of(strip_zero - (Q_BLOCK // STRIDE) * qb, 8)
    st = _dot_nt(kc_ref[0], qs) + strip_ref[0, pl.ds(off, nc_pad), :]
    m = _col_max(st)
    m = jnp.where(m < 0.1 * NEG, 0.0, m)
    e = jnp.exp2(st - m)
    p = e * (1.0 / jnp.maximum(_col_sum(e), 1e-30))
    o_cmp = _dot(vct_ref[0], p.astype(BF16))
    psum = p[:, 0:Q_BLOCK]
    for h in range(1, HG):
        psum = psum + p[:, h * Q_BLOCK:(h + 1) * Q_BLOCK]
    p_hi = psum.astype(BF16)
    p_lo = (psum - p_hi.astype(F32)).astype(BF16)
    imp = _dot(c2st_ref[...], p_hi) + _dot(c2st_ref[...], p_lo)

    ns = imp.shape[0]
    j = lax.broadcasted_iota(jnp.int32, imp.shape, 0)
    cur = (qb * Q_BLOCK + lax.broadcasted_iota(jnp.int32, imp.shape, 1)) // L_SLC
    forced = (j == 0) | (j == cur) | (j == cur - 1)
    score = jnp.where(j <= cur, imp + jnp.where(forced, FORCE_BONUS, 0.0), NEG)
    sel = _select_topk(score, n_sel)

    per = Q_BLOCK // L_SLC

    def with_mask(keep):
        mk = _pad_rows(jnp.where(keep, 0.0, NEG), Q_BLOCK).T
        return jnp.concatenate([qs, jnp.concatenate([mk] * HG, axis=0).astype(BF16)], axis=1)

    qa_near = with_mask(sel)
    qa_far = with_mask(sel & (j < per * (qb - 1)))

    kb1 = jnp.maximum(qb - 1, 0)
    s1 = _dot_nt(kaug_ref[0, 0, pl.ds(pl.multiple_of(kb1 * Q_BLOCK, Q_BLOCK), Q_BLOCK), :], qa_near)
    s1 = s1 + jnp.where(qb >= 1, t1_ref[0], NEG)
    s0 = _dot_nt(kaug_ref[0, 0, pl.ds(pl.multiple_of(qb * Q_BLOCK, Q_BLOCK), Q_BLOCK), :], qa_near) + t0_ref[0]
    mx_near = jnp.maximum(_col_max(s1), _col_max(s0))

    def far_variant(n_chunks):
        def run(mx):
            scores = [_dot_nt(kaug_ref[0, 0, c * FAR_CHUNK:(c + 1) * FAR_CHUNK, :], qa_far) for c in range(n_chunks)]
            for s in scores:
                mx = jnp.maximum(mx, _col_max(s))
            l = jnp.zeros((1, LANES), F32)
            acc = jnp.zeros((HEAD_DIM, LANES), F32)
            for c, s in enumerate(scores):
                pc = jnp.exp2(s - mx)
                l = l + _col_sum(pc)
                acc = acc + _dot(vst4_ref[0, 0, c], pc.astype(BF16))
            return mx, l, acc
        return run

    chunk_tiles = FAR_CHUNK // Q_BLOCK
    n_far = (jnp.maximum(qb - 1, 0) + chunk_tiles - 1) // chunk_tiles
    max_far = kaug_ref.shape[2] // FAR_CHUNK
    mx, l_s, acc_s = lax.switch(n_far, [far_variant(n) for n in range(max_far + 1)], mx_near)
    p1 = jnp.exp2(s1 - mx)
    p0 = jnp.exp2(s0 - mx)
    l_s = l_s + _col_sum(p1) + _col_sum(p0)
    acc_s = acc_s + _dot(vst1_ref[0, 0, kb1], p1.astype(BF16)) + _dot(vst1_ref[0, 0, qb], p0.astype(BF16))
    o_slc = acc_s * (1.0 / jnp.maximum(l_s, 1e-30))

    gates = jax.nn.sigmoid(graw_ref[0, 0])
    gl = [jnp.concatenate([gates[i, h:h + 1, :] for h in range(HG)], axis=1) for i in range(3)]
    o_t = gl[0] * o_cmp + gl[1] * o_slc + gl[2] * o_win
    for h in range(HG):
        o_ref[:, h * HEAD_DIM:(h + 1) * HEAD_DIM] = o_t[:, h * Q_BLOCK:(h + 1) * Q_BLOCK].T.astype(BF16)


def _attn_prompt(q, graw_t, kc, vct, kaug, vst1, vst4, kwp, vwt, strip, t0, t1, wadd, c2st, *, bsz, t):
    nq = t // Q_BLOCK
    nc_pad = kc.shape[1]
    ns = c2st.shape[0]
    nwin = WINDOW // Q_BLOCK
    per_bg = lambda arr: pl.BlockSpec((1, 1) + arr.shape[2:], lambda b, g, i: (b, g) + (0,) * (arr.ndim - 2))
    per_g = lambda arr: pl.BlockSpec((1,) + arr.shape[1:], lambda b, g, i: (g,) + (0,) * (arr.ndim - 1))
    return pl.pallas_call(
        functools.partial(_attn_prompt_kernel, n_sel=min(N_SEL, ns), nc_pad=nc_pad,
                          strip_zero=strip.shape[1] - nc_pad, nwin=nwin),
        grid=(bsz, N_KV, nq),
        in_specs=[
            pl.BlockSpec((Q_BLOCK, HG * HEAD_DIM), lambda b, g, i: (b * nq + i, g)),
            pl.BlockSpec((1, 1, 3, HG, Q_BLOCK), lambda b, g, i: (b, g, 0, 0, i)),
            pl.BlockSpec((1, nc_pad, HEAD_DIM), lambda b, g, i: (b, 0, g)),
            pl.BlockSpec((1, HEAD_DIM, nc_pad), lambda b, g, i: (b, g, 0)),
            per_bg(kaug), per_bg(vst1), per_bg(vst4), per_bg(kwp), per_bg(vwt),
            per_g(strip), per_g(t0), per_g(t1),
            pl.BlockSpec((1, 1) + wadd.shape[2:], lambda b, g, i: (g, jnp.minimum(i, nwin), 0, 0)),
            pl.BlockSpec(c2st.shape, lambda b, g, i: (0, 0)),
        ],
        out_specs=pl.BlockSpec((Q_BLOCK, HG * HEAD_DIM), lambda b, g, i: (b * nq + i, g)),
        out_shape=jax.ShapeDtypeStruct((bsz * t, D_MODEL), BF16),
        compiler_params=_params("parallel", "parallel", "arbitrary"),
        name="attn_prompt",
    )(q, graw_t, kc, vct, kaug, vst1, vst4, kwp, vwt, strip, t0, t1, wadd, c2st)


def _softmax_rows(s):
    m = jnp.max(s, axis=1, keepdims=True)
    m = jnp.where(m < 0.1 * NEG, 0.0, m)
    e = jnp.exp2(s - m)
    return e * (1.0 / jnp.maximum(jnp.sum(e, axis=1, keepdims=True), 1e-30))


def _attn_sample_kernel(*refs, n_pages, tq, n_sel, ns, n_state):
    kpages = refs[1:1 + n_pages]
    vpages = refs[1 + n_pages:1 + 2 * n_pages]
    (q_ref, g_ref, kc_ref, vc_ref, ksn_ref, vsn_ref, kwn_ref, vwn_ref, kwl_ref, vwl_ref, kst_ref, vst_ref,
     bc_ref, bs_ref, bw_ref, c2st_ref, e_ref,
     o_ref, okw_ref, ovw_ref, kbuf, vbuf, wkbuf, wvbuf) = refs[1 + 2 * n_pages:]
    n_cache = n_pages * PAGE_SIZE
    tail = kbuf.shape[0] - n_cache
    wtail = wkbuf.shape[0] - n_state

    def group_rows(ref, g, n):
        return ref[0, pl.ds(g, n, stride=N_KV), :].astype(BF16)

    def new_rows(ref, n):
        return _pad_rows(ref[0].astype(F32), n).astype(BF16)

    for i in range(n_pages):
        for g in range(N_KV):
            cols = slice(g * HEAD_DIM, (g + 1) * HEAD_DIM)
            kbuf[i * PAGE_SIZE:(i + 1) * PAGE_SIZE, cols] = group_rows(kpages[i], g, PAGE_SIZE)
            vbuf[i * PAGE_SIZE:(i + 1) * PAGE_SIZE, cols] = group_rows(vpages[i], g, PAGE_SIZE)
    kbuf[n_cache:, :] = new_rows(ksn_ref, tail)
    vbuf[n_cache:, :] = new_rows(vsn_ref, tail)
    for g in range(N_KV):
        cols = slice(g * HEAD_DIM, (g + 1) * HEAD_DIM)
        wkbuf[0:n_state, cols] = group_rows(kst_ref, g, n_state)
        wvbuf[0:n_state, cols] = group_rows(vst_ref, g, n_state)
    wkbuf[n_state:, :] = new_rows(kwn_ref, wtail)
    wvbuf[n_state:, :] = new_rows(vwn_ref, wtail)
    keep = okw_ref.shape[1] - tq * N_KV
    okw_ref[0, 0:keep, :] = kst_ref[0, n_state * N_KV - keep:, :]
    okw_ref[0, keep:, :] = kwl_ref[0]
    ovw_ref[0, 0:keep, :] = vst_ref[0, n_state * N_KV - keep:, :]
    ovw_ref[0, keep:, :] = vwl_ref[0]

    q = q_ref[0].astype(F32)
    zero = jnp.zeros((tq, HEAD_DIM), F32)
    qrows = []
    for g in range(N_KV):
        for h in range(HG):
            piece = q[:, (g * HG + h) * HEAD_DIM:(g * HG + h + 1) * HEAD_DIM]
            qrows.append(jnp.concatenate([piece if gg == g else zero for gg in range(N_KV)], axis=1))
    qbd = jnp.concatenate(qrows, axis=0).astype(BF16)

    n_cmp = kc_ref.shape[1] // N_KV
    kc = jnp.concatenate([group_rows(kc_ref, g, n_cmp) for g in range(N_KV)], axis=1)
    vc = jnp.concatenate([group_rows(vc_ref, g, n_cmp) for g in range(N_KV)], axis=1)
    p_c = _softmax_rows(_dot_nt(qbd, kc) + bc_ref[...])
    o_cmp = _dot(p_c.astype(BF16), vc)
    reps = []
    for g in range(N_KV):
        s = p_c[g * HG * tq:g * HG * tq + tq, :]
        for h in range(1, HG):
            s = s + p_c[(g * HG + h) * tq:(g * HG + h + 1) * tq, :]
        reps.extend([s] * HG)
    psum = jnp.concatenate(reps, axis=0)
    p_hi = psum.astype(BF16)
    p_lo = (psum - p_hi.astype(F32)).astype(BF16)
    imp_t = _dot_nt(c2st_ref[...], p_hi) + _dot_nt(c2st_ref[...], p_lo)

    nsp = -(-ns // 8) * 8
    imp_t = imp_t[0:nsp, :]
    j = lax.broadcasted_iota(jnp.int32, imp_t.shape, 0)
    tt = lax.broadcasted_iota(jnp.int32, imp_t.shape, 1) % tq
    cur = (PAST_LEN + tt) // L_SLC
    forced = (j == 0) | (j == cur) | (j == cur - 1)
    score = jnp.where(j <= cur, imp_t + jnp.where(forced, FORCE_BONUS, 0.0), NEG)
    score = jnp.where(j < ns, score, 2.0 * NEG)
    sel_t = jnp.where(_select_topk(score, n_sel), 1.0, 0.0)
    sel = _pad_rows(sel_t, e_ref.shape[0]).T.astype(BF16)
    mask = _dot(sel, e_ref[...])

    s_s = _dot_nt(qbd, kbuf[...]) + bs_ref[...] + jnp.where(mask > 0.5, 0.0, NEG)
    o_slc = _dot(_softmax_rows(s_s).astype(BF16), vbuf[...])

    s_w = _dot_nt(qbd, wkbuf[...]) + bw_ref[...]
    o_win = _dot(_softmax_rows(s_w).astype(BF16), wvbuf[...])

    gates = jax.nn.sigmoid(g_ref[0])
    o_full = gates[:, 0:1] * o_cmp + gates[:, 1:2] * o_slc + gates[:, 2:3] * o_win
    pieces = []
    for g in range(N_KV):
        for h in range(HG):
            r0 = (g * HG + h) * tq
            pieces.append(o_full[r0:r0 + tq, g * HEAD_DIM:(g + 1) * HEAD_DIM])
    o_ref[0] = jnp.concatenate(pieces, axis=1).astype(BF16)


def _attn_sample(page_table, pool_k, pool_v, q3, gcol, kc, vc, ksn, vsn, kwn, vwn, kwl, vwl, kst, vst,
                 bc, bs, bw, c2st, expand, *, ns):
    bsz, n_pages = page_table.shape
    tq = q3.shape[1]
    n_state = kst.shape[1] // N_KV
    n_keys = bs.shape[1]
    n_wkeys = bw.shape[1]
    rows = N_HEADS * tq
    wb_new = min(WINDOW, n_state + tq)
    page_specs = [pl.BlockSpec((1, PAGE_SIZE * N_KV, HEAD_DIM),
                               functools.partial(lambda b, pt, i: (pt[b, i], 0, 0), i=i)) for i in range(n_pages)]
    per_b = lambda shape: pl.BlockSpec((1,) + shape, lambda b, pt: (b, 0, 0))
    const = lambda arr: pl.BlockSpec(arr.shape, lambda b, pt: (0, 0))
    in_specs = page_specs + page_specs + [
        per_b((tq, D_MODEL)), per_b((rows, 3)),
        per_b(kc.shape[1:]), per_b(vc.shape[1:]),
        per_b((tq, KV_W)), per_b((tq, KV_W)), per_b((tq, KV_W)), per_b((tq, KV_W)),
        per_b((tq * N_KV, HEAD_DIM)), per_b((tq * N_KV, HEAD_DIM)),
        per_b((n_state * N_KV, HEAD_DIM)), per_b((n_state * N_KV, HEAD_DIM)),
        const(bc), const(bs), const(bw), const(c2st), const(expand),
    ]
    return pl.pallas_call(
        functools.partial(_attn_sample_kernel, n_pages=n_pages, tq=tq, n_sel=min(N_SEL, ns), ns=ns, n_state=n_state),
        grid_spec=pltpu.PrefetchScalarGridSpec(
            num_scalar_prefetch=1,
            grid=(bsz,),
            in_specs=in_specs,
            out_specs=[per_b((tq, D_MODEL)), per_b((wb_new * N_KV, HEAD_DIM)), per_b((wb_new * N_KV, HEAD_DIM))],
            scratch_shapes=[
                pltpu.VMEM((n_keys, KV_W), BF16), pltpu.VMEM((n_keys, KV_W), BF16),
                pltpu.VMEM((n_wkeys, KV_W), BF16), pltpu.VMEM((n_wkeys, KV_W), BF16),
            ],
        ),
        out_shape=[
            jax.ShapeDtypeStruct((bsz, tq, D_MODEL), BF16),
            jax.ShapeDtypeStruct((bsz, wb_new * N_KV, HEAD_DIM), F32),
            jax.ShapeDtypeStruct((bsz, wb_new * N_KV, HEAD_DIM), F32),
        ],
        compiler_params=_params("parallel"),
        name="attn_sample",
    )(page_table, *([pool_k] * n_pages), *([pool_v] * n_pages), q3, gcol, kc, vc, ksn, vsn, kwn, vwn, kwl, vwl,
      kst, vst, bc, bs, bw, c2st, expand)


def _t5_bucket(dist):
    n = jnp.maximum(dist, 0)
    max_exact = N_BUCKETS // 2
    nf = jnp.maximum(n, 1).astype(F32)
    large = max_exact + (jnp.log(nf / max_exact) / math.log(MAX_DIST / max_exact) * (N_BUCKETS - max_exact)).astype(jnp.int32)
    large = jnp.minimum(large, N_BUCKETS - 1)
    return jnp.where(n < max_exact, n, large)


def _bias_by_dist(rel_bias):
    return rel_bias.astype(F32)[_t5_bucket(jnp.arange(MAX_DIST + 1))]


def _bias_tile(btab, dist, valid, shift):
    onehot = (jnp.clip(dist, 0, MAX_DIST)[..., None] == jnp.arange(MAX_DIST + 1)).astype(F32)
    vals = jnp.einsum("...d,dh->h...", onehot, (btab - shift[None, :]) * LOG2E, precision=lax.Precision.HIGHEST)
    return jnp.where(valid[None], vals, NEG)


def _heads_to_lanes(tile):
    h, k, q = tile.shape
    return tile.reshape(N_KV, HG, k, q).transpose(0, 2, 1, 3).reshape(N_KV, k, HG * q)


def _cmp_to_slc_t(nc_pad, nc, ns, ns_pad):
    cs = jnp.arange(nc_pad)[None, :] * STRIDE
    js = jnp.arange(ns_pad)[:, None] * L_SLC
    ov = jnp.clip(jnp.minimum(cs + L_CMP, js + L_SLC) - jnp.maximum(cs, js), 0, None)
    ok = (jnp.arange(nc_pad)[None, :] < nc) & (jnp.arange(ns_pad)[:, None] < ns)
    return jnp.where(ok, ov.astype(F32) / L_CMP, 0.0).astype(BF16)


def _prompt_attention(q, graw, kc, vc, ksb, vsb, kwb, vwb, btab, *, bsz, t):
    nq = t // Q_BLOCK
    nch = t // STRIDE
    nc = (t - L_CMP) // STRIDE + 1
    ns = -(-t // L_SLC)
    nwin = WINDOW // Q_BLOCK
    far = btab[MAX_DIST]
    none = jnp.zeros_like(far)
    graw_t = graw[:, :3 * N_HEADS].reshape(bsz, t, N_KV, HG, 3).transpose(0, 2, 4, 3, 1)

    def by_group(a):
        return a.reshape(bsz, t, N_KV, HEAD_DIM).transpose(0, 2, 1, 3)

    def transposed_tiles(a, tile):
        return a.reshape(bsz, N_KV, a.shape[2] // tile, tile, HEAD_DIM).transpose(0, 1, 2, 4, 3)

    ks, vs, kw, vw = (by_group(a) for a in (ksb, vsb, kwb, vwb))
    block_onehot = (jnp.arange(t)[:, None] // L_SLC == jnp.arange(128)[None, :]).astype(BF16)
    kaug = jnp.concatenate([ks, jnp.broadcast_to(block_onehot, (bsz, N_KV, t, 128))], axis=-1)
    t_far = -(-t // FAR_CHUNK) * FAR_CHUNK
    kaug = jnp.pad(kaug, ((0, 0), (0, 0), (0, t_far - t), (0, 0)))
    vst1 = transposed_tiles(vs, Q_BLOCK)
    vst4 = transposed_tiles(jnp.pad(vs, ((0, 0), (0, 0), (0, t_far - t), (0, 0))), FAR_CHUNK)
    front = ((0, 0), (0, 0), (WINDOW, 0), (0, 0))
    kwp = jnp.pad(kw, front)
    vwt = transposed_tiles(jnp.pad(vw, front), Q_BLOCK)
    vct = jnp.swapaxes(vc, 1, 2)

    kk = jnp.arange(Q_BLOCK)[:, None]
    qq = jnp.arange(Q_BLOCK)[None, :]
    always = jnp.ones((Q_BLOCK, Q_BLOCK), bool)
    t0 = _heads_to_lanes(_bias_tile(btab, qq - kk, qq - kk >= 0, far))
    t1 = _heads_to_lanes(_bias_tile(btab, Q_BLOCK + qq - kk, always, far))
    oldest = jnp.where(kk > qq, 0.0, NEG).astype(F32)
    wadd = jnp.concatenate([jnp.broadcast_to(jnp.tile(oldest, (1, HG)), (N_KV, Q_BLOCK, LANES)),
                            jnp.zeros((N_KV, (nwin - 2) * Q_BLOCK, LANES), F32), t1, t0], axis=1)
    tile_of_row = jnp.arange((nwin + 1) * Q_BLOCK) // Q_BLOCK
    is_padding = tile_of_row[None, :] < nwin - jnp.arange(nwin + 1)[:, None]
    wadd = jnp.where(is_padding[None, :, :, None], NEG, wadd[:, None])
    nshift = (Q_BLOCK // STRIDE) * (nq - 1)
    lo = min(nshift, Q_BLOCK // STRIDE + MAX_DIST // STRIDE)
    cprime = jnp.arange(-lo, Q_BLOCK // STRIDE)[:, None]
    dist_c = qq - STRIDE * cprime - (L_CMP - 1)
    pattern = _heads_to_lanes(_bias_tile(btab, dist_c, dist_c >= 0, none))
    far_rows = jnp.broadcast_to(jnp.repeat(far * LOG2E, Q_BLOCK).reshape(N_KV, 1, LANES), (N_KV, nshift - lo, LANES))
    strip = jnp.concatenate([far_rows, pattern, jnp.full((N_KV, nch - Q_BLOCK // STRIDE, LANES), NEG, F32)], axis=1)
    c2st = _cmp_to_slc_t(nch, nc, ns, -(-ns // 8) * 8)
    return _attn_prompt(q, graw_t, kc, vct, kaug, vst1, vst4, kwp, vwt, strip, t0, t1, wadd, c2st, bsz=bsz, t=t)


def _sample_attention(q, graw, kc, vc, new_bf16, new_rows, page_table, pool_k, pool_v, state_k, state_v, btab,
                      *, dbsz, dt):
    d = D_MODEL
    n_pages = page_table.shape[1]
    n_pool = pool_k.shape[0]
    n_state = state_k.shape[1]
    nch = n_pages * (PAGE_SIZE // STRIDE)
    t_all = PAST_LEN + dt
    nc = (t_all - L_CMP) // STRIDE + 1
    ns = -(-t_all // L_SLC)
    n_keys = -(-(ns * L_SLC) // 128) * 128
    n_wkeys = -(-(n_state + dt) // 128) * 128
    far = btab[MAX_DIST]
    none = jnp.zeros_like(far)
    gcol = graw[:, :3 * N_HEADS].reshape(dbsz, dt, N_HEADS, 3).transpose(0, 2, 1, 3).reshape(dbsz, N_HEADS * dt, 3)

    qpos = (PAST_LEN + jnp.arange(dt))[:, None]
    cc = jnp.arange(nch)[None, :]
    dist = qpos - (cc * STRIDE + L_CMP - 1)
    bc = _bias_tile(btab, dist, (dist >= 0) & (cc < nc), none).reshape(N_HEADS * dt, nch)
    near0 = (PAST_LEN - MAX_DIST) // 128 * 128
    kpos = jnp.arange(near0, n_keys)[None, :]
    dist = qpos - kpos
    bs = jnp.concatenate([jnp.zeros((N_HEADS * dt, near0), F32),
                          _bias_tile(btab, dist, dist >= 0, far).reshape(N_HEADS * dt, n_keys - near0)], axis=1)
    wi = jnp.arange(n_wkeys)[None, :]
    dist = qpos - (PAST_LEN - n_state + wi)
    bw = _bias_tile(btab, dist, (dist >= 0) & (dist < WINDOW) & (wi < n_state + dt), none).reshape(N_HEADS * dt, n_wkeys)
    c2st = _cmp_to_slc_t(nch, nc, ns, 128)
    expand = (jnp.arange(128)[:, None] == (jnp.arange(n_keys)[None, :] // L_SLC)).astype(BF16)

    ksn, vsn, kwn, vwn = (a.reshape(dbsz, dt, KV_W) for a in new_bf16)
    kwl, vwl = (a.reshape(dbsz, dt * N_KV, HEAD_DIM) for a in new_rows)
    o, sk_win, sv_win = _attn_sample(
        page_table, pool_k.reshape(n_pool, PAGE_SIZE * N_KV, HEAD_DIM), pool_v.reshape(n_pool, PAGE_SIZE * N_KV, HEAD_DIM),
        q.reshape(dbsz, dt, d), gcol, kc, vc, ksn, vsn, kwn, vwn, kwl, vwl,
        state_k.reshape(dbsz, n_state * N_KV, HEAD_DIM), state_v.reshape(dbsz, n_state * N_KV, HEAD_DIM),
        bc, bs, bw, c2st, expand, ns=ns)
    wb = sk_win.shape[1] // N_KV
    return (o.reshape(dbsz * dt, d), sk_win.reshape(dbsz, wb, N_KV, HEAD_DIM), sv_win.reshape(dbsz, wb, N_KV, HEAD_DIM))


def _row(v):
    return v.reshape(1, -1).astype(F32)


def kernel(x_prompt, x_sample, cache_k_cmp, cache_v_cmp, cache_k_slc, cache_v_slc, state_k_win, state_v_win, state_conv, page_table, rel_bias, conv_w_pw1, conv_b_pw1, conv_w_dw, conv_b_dw, conv_ln_g, conv_ln_b, conv_w_pw2, conv_b_pw2, nsa_w_qg, nsa_w_o, w_kv, cmp_pe, cmp_w1, cmp_b1, cmp_w2, cmp_b2, ffn_w_in, ffn_w_out, ln_mix_g, ln_mix_b, ln_ffn_g, ln_ffn_b):
    d = D_MODEL
    d_ff = ffn_w_out.shape[1]
    bsz, t, _ = x_prompt.shape
    dbsz, dt, _ = x_sample.shape
    zeros_d = jnp.zeros((1, d), F32)
    zeros_ff = jnp.zeros((1, 2 * d_ff), F32)

    w_pw2 = conv_w_pw2[0].astype(BF16)
    w_out = [ffn_w_out[layer].astype(BF16) for layer in range(DEPTH)]
    w_kvb = w_kv.astype(BF16)
    w_qg = jnp.pad(nsa_w_qg[0], ((0, 0), (0, d + 128 - nsa_w_qg.shape[2]))).astype(BF16)
    w_o = nsa_w_o[0].astype(BF16)
    w_dw = jnp.repeat(conv_w_dw[0], SUBLANES, axis=0)

    cmp_ops = []
    for i in range(2):
        pe8 = jnp.broadcast_to(cmp_pe[i].reshape(1, -1), (8, L_CMP * HEAD_DIM)).astype(BF16)
        pe_w1 = _mm_plain(pe8, cmp_w1[i].astype(BF16))[0:1]
        cmp_ops.append((_compress_weights(cmp_w1[i]), pe_w1 + _row(cmp_b1[i]), cmp_w2[i].astype(BF16),
                        _row(cmp_b2[i])))

    btab = _bias_by_dist(rel_bias)

    def ffn(xf, xb, layer):
        m = xf.shape[0]
        h = _mm_pair(xb, ffn_w_in, layer, zeros_ff, act="swiglu", out_dtype=BF16, tm=min(m, 1024), tn=512)
        return _mm_res_ln(h, w_out[layer], zeros_d, xf, _row(ln_ffn_g[layer]), _row(ln_ffn_b[layer]),
                          tm=min(m, 512), tk=d_ff // 4)

    def trunk_front(x, state, tm_conv):
        b_, t_, _ = x.shape
        m = b_ * t_
        xb = x.reshape(m, d).astype(BF16)
        u = _mm_pair(xb, conv_w_pw1, 0, _row(conv_b_pw1[0]), act="glu", out_dtype=F32, tm=min(m, 1024), tn=512)
        c = _conv_ln_silu(state, u.reshape(b_, t_, d), w_dw, _row(conv_b_dw[0]), _row(conv_ln_g[0]),
                          _row(conv_ln_b[0]), tm=tm_conv)
        x1, x1b = _mm_res_ln(c.reshape(m, d), w_pw2, _row(conv_b_pw2[0]), x.reshape(m, d),
                             _row(ln_mix_g[0]), _row(ln_mix_b[0]), tm=min(m, 512), tk=d // 2)
        x2, x2b = ffn(x1, x1b, 0)
        return x2, x2b, u

    def trunk_back(x2, o):
        x3, x3b = _mm_res_ln(o, w_o, zeros_d, x2, _row(ln_mix_g[1]), _row(ln_mix_b[1]), tm=min(x2.shape[0], 512),
                             tk=d // 2)
        y, _ = ffn(x3, x3b, 1)
        return y

    as4 = lambda a, b_, t_: a.reshape(b_, t_, N_KV, HEAD_DIM)

    x2, x2b, u_p = trunk_front(x_prompt, jnp.zeros((bsz, CONV_HIST, d), F32), 128)
    kv = _mm_kv(x2b, w_kvb, tm=256)
    kcr, vcr, ksr, vsr, kwr, vwr = kv[:N_KV_OUT]
    ksb, vsb, kwb, vwb = kv[N_KV_OUT:]
    nch_p = t // STRIDE
    ident = jnp.arange(bsz, dtype=jnp.int32).reshape(bsz, 1)
    kc_p = _compress(kcr.reshape(bsz, t * N_KV, HEAD_DIM), ident, *cmp_ops[0], pages_per_row=1, ch_per_page=nch_p)
    vc_p = _compress(vcr.reshape(bsz, t * N_KV, HEAD_DIM), ident, *cmp_ops[1], pages_per_row=1, ch_per_page=nch_p)
    q_p, graw_p = _mm_qg(x2b, w_qg, tm=512)
    natural = lambda a: a.reshape(bsz, nch_p, KV_W).astype(BF16)
    o_p = _prompt_attention(q_p, graw_p, natural(kc_p), natural(vc_p), ksb, vsb, kwb, vwb, btab, bsz=bsz, t=t)
    y_prompt = trunk_back(x2, o_p).reshape(bsz, t, d)

    wb_p = min(WINDOW, t)
    pk_cmp, pv_cmp, pk_slc, pv_slc = (as4(a, bsz, t) for a in (kcr, vcr, ksr, vsr))
    pk_win = as4(kwr, bsz, t)[:, t - wb_p:]
    pv_win = as4(vwr, bsz, t)[:, t - wb_p:]
    p_conv = u_p.reshape(bsz, t, d)[:, t - (CONV_W - 1):][None]

    st0 = state_conv[0]
    s2, s2b, u_s = trunk_front(x_sample, st0, dt)
    kv = _mm_kv(s2b, w_kvb, tm=256)
    skc, svc, sks, svs, skw, svw = kv[:N_KV_OUT]
    n_pages = page_table.shape[1]
    n_pool = cache_k_cmp.shape[0]
    ch_pp = PAGE_SIZE // STRIDE
    kc_s = _compress(cache_k_cmp.reshape(n_pool, PAGE_SIZE * N_KV, HEAD_DIM), page_table, *cmp_ops[0],
                     pages_per_row=n_pages, ch_per_page=ch_pp)
    vc_s = _compress(cache_v_cmp.reshape(n_pool, PAGE_SIZE * N_KV, HEAD_DIM), page_table, *cmp_ops[1],
                     pages_per_row=n_pages, ch_per_page=ch_pp)
    q_s, graw_s = _mm_qg(s2b, w_qg, tm=512)
    o_s, sk_win, sv_win = _sample_attention(q_s, graw_s, kc_s, vc_s, kv[N_KV_OUT:], (skw, svw), page_table,
                                            cache_k_slc, cache_v_slc, state_k_win, state_v_win, btab,
                                            dbsz=dbsz, dt=dt)
    y_sample = trunk_back(s2, o_s).reshape(dbsz, dt, d)

    sk_cmp, sv_cmp, sk_slc, sv_slc = (as4(a, dbsz, dt) for a in (skc, svc, sks, svs))
    s_conv = jnp.concatenate([st0, u_s.reshape(dbsz, dt, d)], axis=1)[:, -(CONV_W - 1):][None]

    return (y_prompt, y_sample, pk_cmp, pv_cmp, pk_slc, pv_slc, pk_win, pv_win, p_conv,
            sk_cmp, sv_cmp, sk_slc, sv_slc, sk_win, sv_win, s_conv)
```

```python
import functools
import math

import jax
import jax.numpy as jnp
from jax import lax
from jax.experimental import pallas as pl
from jax.experimental.pallas import tpu as pltpu

D_MODEL = 2048
PAST_LEN = 2048
PAGE_SIZE = 128
N_HEADS = 16
HEAD_DIM = D_MODEL // N_HEADS
N_KV = 4
HG = N_HEADS // N_KV
L_CMP = 32
STRIDE = 16
CMP_HID = HEAD_DIM
L_SLC = 64
N_SEL = 16
WINDOW = 512
Q_BLOCK = 128
CONV_W = 31
N_BUCKETS = 32
MAX_DIST = 128
DEPTH = 2
ALPHA = (2 * DEPTH) ** 0.25
LN_EPS = 1e-5
NEG = -1e30
FORCE_BONUS = 1e4
LOG2E = math.log2(math.e)

KV_W = N_KV * HEAD_DIM
LANES = HG * Q_BLOCK
FAR_CHUNK = 8 * Q_BLOCK
VMEM_LIMIT = 56 * 1024 * 1024

F32 = jnp.float32
BF16 = jnp.bfloat16


def _params(*sem):
    return pltpu.CompilerParams(dimension_semantics=sem, vmem_limit_bytes=VMEM_LIMIT)


def _dot(a, b):
    return jnp.dot(a, b, preferred_element_type=F32)


def _dot_nt(a, b):
    return lax.dot_general(a, b, (((1,), (1,)), ((), ())), preferred_element_type=F32)


def _layer_norm(x, g, b):
    mu = jnp.mean(x, axis=-1, keepdims=True)
    xc = x - mu
    var = jnp.mean(xc * xc, axis=-1, keepdims=True)
    return xc * lax.rsqrt(var + LN_EPS) * g + b


def _silu(x):
    return x * jax.nn.sigmoid(x)


def _gelu_tanh(x):
    c = math.sqrt(2.0 / math.pi)
    return 0.5 * x * (1.0 + jnp.tanh(c * (x + 0.044715 * (x * x * x))))


def _pad_rows(x, n):
    if x.shape[0] == n:
        return x
    return jnp.concatenate([x, jnp.zeros((n - x.shape[0], x.shape[1]), x.dtype)], axis=0)


def _mm_pair_kernel(x_ref, wa_ref, wb_ref, ba_ref, bb_ref, o_ref, wa_bf, wb_bf, *, act):
    @pl.when(pl.program_id(1) == 0)
    def _():
        wa_bf[...] = wa_ref[...].astype(BF16)
        wb_bf[...] = wb_ref[...].astype(BF16)

    x = x_ref[...]
    a = _dot(x, wa_bf[...]) + ba_ref[...]
    b = _dot(x, wb_bf[...]) + bb_ref[...]
    if act == "glu":
        o = a * jax.nn.sigmoid(b)
    else:
        o = _silu(a) * b
    o_ref[...] = o.astype(o_ref.dtype)


def _mm_pair(x, w, layer, bias, *, act, out_dtype, tm, tn):
    m, k = x.shape
    n = w.shape[2] // 2
    nj = n // tn
    return pl.pallas_call(
        functools.partial(_mm_pair_kernel, act=act),
        grid=(nj, m // tm),
        in_specs=[
            pl.BlockSpec((tm, k), lambda j, i: (i, 0)),
            pl.BlockSpec((None, k, tn), lambda j, i: (layer, 0, j)),
            pl.BlockSpec((None, k, tn), lambda j, i: (layer, 0, j + nj)),
            pl.BlockSpec((1, tn), lambda j, i: (0, j)),
            pl.BlockSpec((1, tn), lambda j, i: (0, j + nj)),
        ],
        out_specs=pl.BlockSpec((tm, tn), lambda j, i: (i, j)),
        out_shape=jax.ShapeDtypeStruct((m, n), out_dtype),
        scratch_shapes=[pltpu.VMEM((k, tn), BF16), pltpu.VMEM((k, tn), BF16)],
        compiler_params=_params("parallel", "arbitrary"),
        name="mm_pair_" + act,
    )(x, w, w, bias, bias)


LN_ROWS = 128


ROW_TILES_PER_WEIGHT = 2


def _mm_res_ln_kernel(h_ref, w_ref, bias_ref, xres_ref, g_ref, b_ref, of_ref, ob_ref, acc_ref):
    kk = pl.program_id(1)
    r = pl.program_id(2)

    @pl.when(kk == 0)
    def _():
        acc_ref[r] = jnp.zeros(acc_ref.shape[1:], F32)

    acc_ref[r] += _dot(h_ref[...], w_ref[...])

    @pl.when(kk == pl.num_programs(1) - 1)
    def _():
        step = min(LN_ROWS, of_ref.shape[0])
        for r0 in range(0, of_ref.shape[0], step):
            rows = slice(r0, r0 + step)
            y = ALPHA * xres_ref[rows, :] + (acc_ref[r, rows, :] + bias_ref[...])
            o = _layer_norm(y, g_ref[...], b_ref[...])
            of_ref[rows, :] = o
            ob_ref[rows, :] = o.astype(BF16)


def _mm_res_ln(h, w, bias, xres, g, b, *, tm, tk):
    m, k = h.shape
    d = w.shape[1]
    sub = ROW_TILES_PER_WEIGHT
    last = k // tk - 1
    at_end = lambda i, kk, r: (jnp.where(kk == last, i * sub + r, i * sub), 0)
    vec = lambda i, kk, r: (0, 0)
    return pl.pallas_call(
        _mm_res_ln_kernel,
        grid=(m // (tm * sub), k // tk, sub),
        in_specs=[
            pl.BlockSpec((tm, tk), lambda i, kk, r: (i * sub + r, kk)),
            pl.BlockSpec((tk, d), lambda i, kk, r: (kk, 0)),
            pl.BlockSpec((1, d), vec),
            pl.BlockSpec((tm, d), at_end),
            pl.BlockSpec((1, d), vec),
            pl.BlockSpec((1, d), vec),
        ],
        out_specs=[pl.BlockSpec((tm, d), at_end), pl.BlockSpec((tm, d), at_end)],
        out_shape=[jax.ShapeDtypeStruct((m, d), F32), jax.ShapeDtypeStruct((m, d), BF16)],
        scratch_shapes=[pltpu.VMEM((sub, tm, d), F32)],
        compiler_params=_params("parallel", "arbitrary", "arbitrary"),
        name="mm_res_ln",
    )(h, w, bias, xres, g, b)


N_KV_OUT = 6
N_KV_BF16 = 4


def _mm_kv_kernel(x_ref, w_ref, *out_refs, tm):
    x = x_ref[...]
    for idx in range(N_KV_OUT):
        acc = _dot(x, w_ref[:, idx * KV_W:(idx + 1) * KV_W])
        for g in range(N_KV):
            out_refs[idx][pl.ds(g, tm, stride=N_KV), :] = acc[:, g * HEAD_DIM:(g + 1) * HEAD_DIM]
        first_bf16 = N_KV_OUT - N_KV_BF16
        if idx >= first_bf16:
            out_refs[N_KV_OUT + idx - first_bf16][...] = acc.astype(BF16)


def _mm_kv(x, w, *, tm):
    m, k = x.shape
    row = lambda i: (i, 0)
    return pl.pallas_call(
        functools.partial(_mm_kv_kernel, tm=tm),
        grid=(m // tm,),
        in_specs=[pl.BlockSpec((tm, k), row), pl.BlockSpec((k, N_KV_OUT * KV_W), lambda i: (0, 0))],
        out_specs=[pl.BlockSpec((tm * N_KV, HEAD_DIM), row)] * N_KV_OUT + [pl.BlockSpec((tm, KV_W), row)] * N_KV_BF16,
        out_shape=([jax.ShapeDtypeStruct((m * N_KV, HEAD_DIM), F32)] * N_KV_OUT
                   + [jax.ShapeDtypeStruct((m, KV_W), BF16)] * N_KV_BF16),
        compiler_params=_params("parallel"),
        name="mm_kv",
    )(x, w)


def _mm_qg_kernel(x_ref, w_ref, q_ref, g_ref):
    x = x_ref[...]
    scale = HEAD_DIM ** -0.5 * LOG2E
    for c in range(D_MODEL // 512):
        acc = _dot(x, w_ref[:, c * 512:(c + 1) * 512])
        q_ref[:, c * 512:(c + 1) * 512] = (acc * scale).astype(BF16)
    g_ref[...] = _dot(x, w_ref[:, D_MODEL:])


def _mm_qg(x, w, *, tm):
    m, k = x.shape
    row = lambda i: (i, 0)
    return pl.pallas_call(
        _mm_qg_kernel,
        grid=(m // tm,),
        in_specs=[pl.BlockSpec((tm, k), row), pl.BlockSpec((k, D_MODEL + 128), lambda i: (0, 0))],
        out_specs=[pl.BlockSpec((tm, D_MODEL), row), pl.BlockSpec((tm, 128), row)],
        out_shape=[jax.ShapeDtypeStruct((m, D_MODEL), BF16), jax.ShapeDtypeStruct((m, 128), F32)],
        compiler_params=_params("parallel"),
        name="mm_qg",
    )(x, w)


def _mm_plain_kernel(x_ref, w_ref, o_ref):
    o_ref[...] = _dot(x_ref[...], w_ref[...])


def _mm_plain(x, w):
    return pl.pallas_call(
        _mm_plain_kernel,
        out_shape=jax.ShapeDtypeStruct((x.shape[0], w.shape[1]), F32),
        compiler_params=pltpu.CompilerParams(vmem_limit_bytes=VMEM_LIMIT),
        name="mm_plain",
    )(x, w)


CONV_HIST = CONV_W - 1
CONV_HEAD = 32
CONV_CB = 512
SUBLANES = 8
LANE_TILE = 128


def _conv_kernel(*refs, tm, has_prev):
    if has_prev:
        state_ref, uprev_ref, ucur_ref, w_ref, bdw_ref, g_ref, b_ref, o_ref, z_ref, zs_ref, y_ref = refs
    else:
        state_ref, ucur_ref, w_ref, bdw_ref, g_ref, b_ref, o_ref, z_ref, zs_ref, y_ref = refs
    i = pl.program_id(1)
    pad = CONV_HEAD - CONV_HIST
    rows = CONV_HEAD + tm
    total = rows + SUBLANES

    @pl.when(i == 0)
    def _():
        z_ref[0:pad, :] = jnp.zeros((pad, D_MODEL), F32)
        z_ref[pad:CONV_HEAD, :] = state_ref[0]

    if has_prev:
        @pl.when(i > 0)
        def _():
            z_ref[0:CONV_HEAD, :] = uprev_ref[0]

    z_ref[CONV_HEAD:rows, :] = ucur_ref[0]
    z_ref[rows:total, :] = jnp.zeros((SUBLANES, D_MODEL), F32)
    for c0 in range(0, D_MODEL, CONV_CB):
        zc = z_ref[:, c0:c0 + CONV_CB]
        zs_ref[0, :, c0:c0 + CONV_CB] = zc[0:rows]
        for b in range(1, SUBLANES):
            zs_ref[b, :, c0:c0 + CONV_CB] = pltpu.roll(zc, total - b, 0)[0:rows]
    for c0 in range(0, D_MODEL, LANE_TILE):
        cols = slice(c0, c0 + LANE_TILE)
        taps = [w_ref[w * SUBLANES:(w + 1) * SUBLANES, cols] for w in range(CONV_W)]
        for r0 in range(0, tm, SUBLANES):
            acc = None
            for w in range(CONV_W):
                shift, base = (pad + w) % SUBLANES, (pad + w) // SUBLANES * SUBLANES
                term = zs_ref[shift, r0 + base:r0 + base + SUBLANES, cols] * taps[w]
                acc = term if acc is None else acc + term
            y_ref[r0:r0 + SUBLANES, cols] = acc
    y = _layer_norm(y_ref[...] + bdw_ref[...], g_ref[...], b_ref[...])
    o_ref[0] = _silu(y).astype(BF16)


def _conv_ln_silu(state, u, w_dw, b_dw, g, b, *, tm):
    bsz, t, d = u.shape
    has_prev = t > tm
    per = tm // CONV_HEAD
    vec = lambda bb, i: (0, 0)
    prev = [pl.BlockSpec((1, CONV_HEAD, d), lambda bb, i: (bb, jnp.maximum(i * per - 1, 0), 0))] if has_prev else []
    rows = CONV_HEAD + tm
    return pl.pallas_call(
        functools.partial(_conv_kernel, tm=tm, has_prev=has_prev),
        grid=(bsz, t // tm),
        in_specs=[pl.BlockSpec((1, CONV_HIST, d), lambda bb, i: (bb, 0, 0))] + prev + [
            pl.BlockSpec((1, tm, d), lambda bb, i: (bb, i, 0)),
            pl.BlockSpec((CONV_W * SUBLANES, d), vec),
            pl.BlockSpec((1, d), vec),
            pl.BlockSpec((1, d), vec),
            pl.BlockSpec((1, d), vec),
        ],
        out_specs=pl.BlockSpec((1, tm, d), lambda bb, i: (bb, i, 0)),
        out_shape=jax.ShapeDtypeStruct((bsz, t, d), BF16),
        scratch_shapes=[pltpu.VMEM((rows + SUBLANES, d), F32), pltpu.VMEM((SUBLANES, rows, d), F32),
                        pltpu.VMEM((tm, d), F32)],
        compiler_params=_params("parallel", "arbitrary"),
        name="conv_ln_silu",
    )(state, *([u] if has_prev else []), u, w_dw, b_dw, g, b)


CHUNK_ROWS = STRIDE * N_KV
CHUNK_TILES = CHUNK_ROWS // SUBLANES
TOK_PER_TILE = SUBLANES // N_KV


def _compress_kernel(*refs, n_in, ch_pp):
    x_refs = refs[1:1 + n_in]
    w1k_ref, bias1_ref, w2_ref, b2_ref, o_ref = refs[1 + n_in:]
    nch = n_in * ch_pp
    rows = nch * SUBLANES
    pages = [r[0].reshape(ch_pp, CHUNK_TILES, SUBLANES, HEAD_DIM) for r in x_refs]
    lhs = []
    for k in range(CHUNK_TILES):
        tiles = [p[:, k].reshape(ch_pp * SUBLANES, HEAD_DIM) for p in pages]
        lhs.append((jnp.concatenate(tiles, axis=0) if n_in > 1 else tiles[0]).astype(BF16))
    acc = _dot(jnp.concatenate(lhs, axis=1), w1k_ref[...])
    width = 2 * CMP_HID
    comb = acc[:, 0:width]
    for jj in range(1, TOK_PER_TILE):
        comb = comb + pltpu.roll(acc[:, jj * width:(jj + 1) * width], rows - jj * N_KV, 0)
    lo = comb[:, :CMP_HID]
    hi = comb[:, CMP_HID:]
    hi_next = pltpu.roll(hi, rows - SUBLANES, 0)
    h = _gelu_tanh(lo + hi_next + bias1_ref[...])
    out = _dot(h.astype(BF16), w2_ref[...]) + b2_ref[...]
    for c in range(nch):
        o_ref[0, c * N_KV:(c + 1) * N_KV, :] = out[c * SUBLANES:c * SUBLANES + N_KV, :]


def _compress(rows2d, page_table, w1k, bias1, w2, b2, *, pages_per_row, ch_per_page):
    bsz = page_table.shape[0]
    nch = pages_per_row * ch_per_page
    const2 = lambda bb, pt: (0, 0)
    page_rows = ch_per_page * CHUNK_ROWS
    in_specs = [pl.BlockSpec((1, page_rows, HEAD_DIM), functools.partial(lambda bb, pt, i: (pt[bb, i], 0, 0), i=i))
                for i in range(pages_per_row)]
    in_specs += [
        pl.BlockSpec(w1k.shape, const2),
        pl.BlockSpec((1, CMP_HID), const2),
        pl.BlockSpec((CMP_HID, HEAD_DIM), const2),
        pl.BlockSpec((1, HEAD_DIM), const2),
    ]
    return pl.pallas_call(
        functools.partial(_compress_kernel, n_in=pages_per_row, ch_pp=ch_per_page),
        grid_spec=pltpu.PrefetchScalarGridSpec(
            num_scalar_prefetch=1,
            grid=(bsz,),
            in_specs=in_specs,
            out_specs=pl.BlockSpec((1, nch * N_KV, HEAD_DIM), lambda bb, pt: (bb, 0, 0)),
        ),
        out_shape=jax.ShapeDtypeStruct((bsz, nch * N_KV, HEAD_DIM), F32),
        compiler_params=_params("parallel"),
        name="compress",
    )(page_table, *([rows2d] * pages_per_row), w1k, bias1, w2, b2)


def _compress_weights(w1):
    w1b = w1.astype(BF16).reshape(2, STRIDE, HEAD_DIM, CMP_HID)
    per_tok = jnp.concatenate([w1b[0], w1b[1]], axis=-1)
    per_tile = per_tok.reshape(CHUNK_TILES, TOK_PER_TILE, HEAD_DIM, 2 * CMP_HID).transpose(0, 2, 1, 3)
    return per_tile.reshape(CHUNK_TILES * HEAD_DIM, TOK_PER_TILE * 2 * CMP_HID)


def _select_topk(score, n_sel):
    jn = score.shape[0]
    sub = lax.broadcasted_iota(jnp.int32, (8, score.shape[1]), 0)
    blks = [score[8 * v:8 * v + 8, :] for v in range(jn // 8)]
    cnt = [jnp.zeros(b.shape, F32) for b in blks]
    for jp in range(jn):
        row = score[jp:jp + 1, :]
        for v, blk in enumerate(blks):
            gt = jnp.where(row > blk, 1.0, 0.0)
            ge = jnp.where(row >= blk, 1.0, 0.0)
            if 8 * v + 8 <= jp:
                beats = gt
            elif 8 * v > jp:
                beats = ge
            else:
                beats = jnp.where(sub > jp % 8, ge, gt)
            cnt[v] = cnt[v] + beats
    return jnp.concatenate(cnt, axis=0) < n_sel


def _col_max(x):
    return jnp.max(x, axis=0, keepdims=True)


def _col_sum(x):
    return jnp.sum(x, axis=0, keepdims=True)


QB_PER_STEP = 2


def _attn_prompt_kernel(*refs, n_sel, nc_pad, strip_zero, nwin):
    (q_ref, graw_ref, kc_ref, vct_ref, kaug_ref, vst1_ref, vst4_ref, kwp_ref, vwt_ref,
     strip_ref, t0_ref, t1_ref) = refs[:12]
    wadd_refs = refs[12:12 + QB_PER_STEP]
    c2st_ref, o_ref = refs[12 + QB_PER_STEP:]
    per = Q_BLOCK // L_SLC
    chunk_tiles = FAR_CHUNK // Q_BLOCK
    max_far = kaug_ref.shape[2] // FAR_CHUNK

    def front(u):
        qb = pl.program_id(2) * QB_PER_STEP + u
        q = q_ref[u * Q_BLOCK:(u + 1) * Q_BLOCK, :]
        qs = jnp.concatenate([q[:, h * HEAD_DIM:(h + 1) * HEAD_DIM] for h in range(HG)], axis=0)

        kw = kwp_ref[0, 0, pl.ds(pl.multiple_of(qb * Q_BLOCK, Q_BLOCK), (nwin + 1) * Q_BLOCK), :]
        sw = _dot_nt(kw, qs) + wadd_refs[u][0, 0]
        pw = jnp.exp2(sw - _col_max(sw))
        acc_w = _dot(vwt_ref[0, 0, qb], pw[0:Q_BLOCK, :].astype(BF16))
        for dd in range(1, nwin + 1):
            acc_w = acc_w + _dot(vwt_ref[0, 0, qb + dd], pw[dd * Q_BLOCK:(dd + 1) * Q_BLOCK, :].astype(BF16))
        o_win = acc_w * (1.0 / jnp.maximum(_col_sum(pw), 1e-30))

        off = pl.multiple_of(strip_zero - (Q_BLOCK // STRIDE) * qb, 8)
        st = _dot_nt(kc_ref[0], qs) + strip_ref[0, pl.ds(off, nc_pad), :]
        m = _col_max(st)
        m = jnp.where(m < 0.1 * NEG, 0.0, m)
        e = jnp.exp2(st - m)
        p = e * (1.0 / jnp.maximum(_col_sum(e), 1e-30))
        o_cmp = _dot(vct_ref[0], p.astype(BF16))
        psum = p[:, 0:Q_BLOCK]
        for h in range(1, HG):
            psum = psum + p[:, h * Q_BLOCK:(h + 1) * Q_BLOCK]
        p_hi = psum.astype(BF16)
        p_lo = (psum - p_hi.astype(F32)).astype(BF16)
        imp = _dot(c2st_ref[...], p_hi) + _dot(c2st_ref[...], p_lo)

        j = lax.broadcasted_iota(jnp.int32, imp.shape, 0)
        cur = (qb * Q_BLOCK + lax.broadcasted_iota(jnp.int32, imp.shape, 1)) // L_SLC
        forced = (j == 0) | (j == cur) | (j == cur - 1)
        score = jnp.where(j <= cur, imp + jnp.where(forced, FORCE_BONUS, 0.0), NEG)
        sel = _select_topk(score, n_sel)

        def with_mask(keep):
            mk = _pad_rows(jnp.where(keep, 0.0, NEG), Q_BLOCK).T
            return jnp.concatenate([qs, jnp.concatenate([mk] * HG, axis=0).astype(BF16)], axis=1)

        qa_near = with_mask(sel)
        qa_far = with_mask(sel & (j < per * (qb - 1)))
        kb1 = jnp.maximum(qb - 1, 0)
        s1 = _dot_nt(kaug_ref[0, 0, pl.ds(pl.multiple_of(kb1 * Q_BLOCK, Q_BLOCK), Q_BLOCK), :], qa_near)
        s1 = s1 + jnp.where(qb >= 1, t1_ref[0], NEG)
        s0 = _dot_nt(kaug_ref[0, 0, pl.ds(pl.multiple_of(qb * Q_BLOCK, Q_BLOCK), Q_BLOCK), :], qa_near) + t0_ref[0]
        return dict(qb=qb, kb1=kb1, o_win=o_win, o_cmp=o_cmp, qa_far=qa_far, s1=s1, s0=s0)

    def far(f):
        def variant(n_chunks):
            def run(mx):
                scores = [_dot_nt(kaug_ref[0, 0, c * FAR_CHUNK:(c + 1) * FAR_CHUNK, :], f["qa_far"])
                          for c in range(n_chunks)]
                for s in scores:
                    mx = jnp.maximum(mx, _col_max(s))
                l = jnp.zeros((1, LANES), F32)
                acc = jnp.zeros((HEAD_DIM, LANES), F32)
                for c, s in enumerate(scores):
                    pc = jnp.exp2(s - mx)
                    l = l + _col_sum(pc)
                    acc = acc + _dot(vst4_ref[0, 0, c], pc.astype(BF16))
                return mx, l, acc
            return run

        mx_near = jnp.maximum(_col_max(f["s1"]), _col_max(f["s0"]))
        n_far = (jnp.maximum(f["qb"] - 1, 0) + chunk_tiles - 1) // chunk_tiles
        return lax.switch(n_far, [variant(n) for n in range(max_far + 1)], mx_near)

    def back(u, f, mx, l_s, acc_s):
        p1 = jnp.exp2(f["s1"] - mx)
        p0 = jnp.exp2(f["s0"] - mx)
        l_s = l_s + _col_sum(p1) + _col_sum(p0)
        acc_s = (acc_s + _dot(vst1_ref[0, 0, f["kb1"]], p1.astype(BF16))
                 + _dot(vst1_ref[0, 0, f["qb"]], p0.astype(BF16)))
        o_slc = acc_s * (1.0 / jnp.maximum(l_s, 1e-30))
        gates = jax.nn.sigmoid(graw_ref[0, 0, :, :, u * Q_BLOCK:(u + 1) * Q_BLOCK])
        gl = [jnp.concatenate([gates[i, h:h + 1, :] for h in range(HG)], axis=1) for i in range(3)]
        o_t = gl[0] * f["o_cmp"] + gl[1] * o_slc + gl[2] * f["o_win"]
        for h in range(HG):
            o_ref[u * Q_BLOCK:(u + 1) * Q_BLOCK, h * HEAD_DIM:(h + 1) * HEAD_DIM] = (
                o_t[:, h * Q_BLOCK:(h + 1) * Q_BLOCK].T.astype(BF16))

    fronts = [front(u) for u in range(QB_PER_STEP)]
    fars = [far(f) for f in fronts]
    for u in range(QB_PER_STEP):
        back(u, fronts[u], *fars[u])


def _attn_prompt(q, graw_t, kc, vct, kaug, vst1, vst4, kwp, vwt, strip, t0, t1, wadd, c2st, *, bsz, t):
    nq = t // Q_BLOCK
    steps = nq // QB_PER_STEP
    rows = QB_PER_STEP * Q_BLOCK
    nc_pad = kc.shape[1]
    ns = c2st.shape[0]
    nwin = WINDOW // Q_BLOCK
    per_bg = lambda arr: pl.BlockSpec((1, 1) + arr.shape[2:], lambda b, g, i: (b, g) + (0,) * (arr.ndim - 2))
    per_g = lambda arr: pl.BlockSpec((1,) + arr.shape[1:], lambda b, g, i: (g,) + (0,) * (arr.ndim - 1))
    wadd_specs = [pl.BlockSpec((1, 1) + wadd.shape[2:],
                               functools.partial(lambda b, g, i, u: (g, jnp.minimum(i * QB_PER_STEP + u, nwin), 0, 0), u=u))
                  for u in range(QB_PER_STEP)]
    return pl.pallas_call(
        functools.partial(_attn_prompt_kernel, n_sel=min(N_SEL, ns), nc_pad=nc_pad,
                          strip_zero=strip.shape[1] - nc_pad, nwin=nwin),
        grid=(bsz, N_KV, steps),
        in_specs=[
            pl.BlockSpec((rows, HG * HEAD_DIM), lambda b, g, i: (b * steps + i, g)),
            pl.BlockSpec((1, 1, 3, HG, rows), lambda b, g, i: (b, g, 0, 0, i)),
            pl.BlockSpec((1, nc_pad, HEAD_DIM), lambda b, g, i: (b, 0, g)),
            pl.BlockSpec((1, HEAD_DIM, nc_pad), lambda b, g, i: (b, g, 0)),
            per_bg(kaug), per_bg(vst1), per_bg(vst4), per_bg(kwp), per_bg(vwt),
            per_g(strip), per_g(t0), per_g(t1),
        ] + wadd_specs + [pl.BlockSpec(c2st.shape, lambda b, g, i: (0, 0))],
        out_specs=pl.BlockSpec((rows, HG * HEAD_DIM), lambda b, g, i: (b * steps + i, g)),
        out_shape=jax.ShapeDtypeStruct((bsz * t, D_MODEL), BF16),
        compiler_params=_params("parallel", "parallel", "arbitrary"),
        name="attn_prompt",
    )(q, graw_t, kc, vct, kaug, vst1, vst4, kwp, vwt, strip, t0, t1, *([wadd] * QB_PER_STEP), c2st)


def _softmax_rows(s):
    m = jnp.max(s, axis=1, keepdims=True)
    m = jnp.where(m < 0.1 * NEG, 0.0, m)
    e = jnp.exp2(s - m)
    return e * (1.0 / jnp.maximum(jnp.sum(e, axis=1, keepdims=True), 1e-30))


def _attn_sample_kernel(*refs, n_pages, tq, n_sel, ns, n_state):
    kpages = refs[1:1 + n_pages]
    vpages = refs[1 + n_pages:1 + 2 * n_pages]
    (q_ref, g_ref, kc_ref, vc_ref, ksn_ref, vsn_ref, kwn_ref, vwn_ref, kwl_ref, vwl_ref, kst_ref, vst_ref,
     bc_ref, bs_ref, bw_ref, c2st_ref, e_ref,
     o_ref, okw_ref, ovw_ref, kbuf, vbuf, wkbuf, wvbuf) = refs[1 + 2 * n_pages:]
    n_cache = n_pages * PAGE_SIZE
    tail = kbuf.shape[0] - n_cache
    wtail = wkbuf.shape[0] - n_state

    def group_rows(ref, g, n):
        return ref[0, pl.ds(g, n, stride=N_KV), :].astype(BF16)

    def new_rows(ref, n):
        return _pad_rows(ref[0].astype(F32), n).astype(BF16)

    for i in range(n_pages):
        for g in range(N_KV):
            cols = slice(g * HEAD_DIM, (g + 1) * HEAD_DIM)
            kbuf[i * PAGE_SIZE:(i + 1) * PAGE_SIZE, cols] = group_rows(kpages[i], g, PAGE_SIZE)
            vbuf[i * PAGE_SIZE:(i + 1) * PAGE_SIZE, cols] = group_rows(vpages[i], g, PAGE_SIZE)
    kbuf[n_cache:, :] = new_rows(ksn_ref, tail)
    vbuf[n_cache:, :] = new_rows(vsn_ref, tail)
    for g in range(N_KV):
        cols = slice(g * HEAD_DIM, (g + 1) * HEAD_DIM)
        wkbuf[0:n_state, cols] = group_rows(kst_ref, g, n_state)
        wvbuf[0:n_state, cols] = group_rows(vst_ref, g, n_state)
    wkbuf[n_state:, :] = new_rows(kwn_ref, wtail)
    wvbuf[n_state:, :] = new_rows(vwn_ref, wtail)
    keep = okw_ref.shape[1] - tq * N_KV
    okw_ref[0, 0:keep, :] = kst_ref[0, n_state * N_KV - keep:, :]
    okw_ref[0, keep:, :] = kwl_ref[0]
    ovw_ref[0, 0:keep, :] = vst_ref[0, n_state * N_KV - keep:, :]
    ovw_ref[0, keep:, :] = vwl_ref[0]

    q = q_ref[0].astype(F32)
    zero = jnp.zeros((tq, HEAD_DIM), F32)
    qrows = []
    for g in range(N_KV):
        for h in range(HG):
            piece = q[:, (g * HG + h) * HEAD_DIM:(g * HG + h + 1) * HEAD_DIM]
            qrows.append(jnp.concatenate([piece if gg == g else zero for gg in range(N_KV)], axis=1))
    qbd = jnp.concatenate(qrows, axis=0).astype(BF16)

    n_cmp = kc_ref.shape[1] // N_KV
    kc = jnp.concatenate([group_rows(kc_ref, g, n_cmp) for g in range(N_KV)], axis=1)
    vc = jnp.concatenate([group_rows(vc_ref, g, n_cmp) for g in range(N_KV)], axis=1)
    p_c = _softmax_rows(_dot_nt(qbd, kc) + bc_ref[...])
    o_cmp = _dot(p_c.astype(BF16), vc)
    reps = []
    for g in range(N_KV):
        s = p_c[g * HG * tq:g * HG * tq + tq, :]
        for h in range(1, HG):
            s = s + p_c[(g * HG + h) * tq:(g * HG + h + 1) * tq, :]
        reps.extend([s] * HG)
    psum = jnp.concatenate(reps, axis=0)
    p_hi = psum.astype(BF16)
    p_lo = (psum - p_hi.astype(F32)).astype(BF16)
    imp_t = _dot_nt(c2st_ref[...], p_hi) + _dot_nt(c2st_ref[...], p_lo)

    nsp = -(-ns // 8) * 8
    imp_t = imp_t[0:nsp, :]
    j = lax.broadcasted_iota(jnp.int32, imp_t.shape, 0)
    tt = lax.broadcasted_iota(jnp.int32, imp_t.shape, 1) % tq
    cur = (PAST_LEN + tt) // L_SLC
    forced = (j == 0) | (j == cur) | (j == cur - 1)
    score = jnp.where(j <= cur, imp_t + jnp.where(forced, FORCE_BONUS, 0.0), NEG)
    score = jnp.where(j < ns, score, 2.0 * NEG)
    sel_t = jnp.where(_select_topk(score, n_sel), 1.0, 0.0)
    sel = _pad_rows(sel_t, e_ref.shape[0]).T.astype(BF16)
    mask = _dot(sel, e_ref[...])

    s_s = _dot_nt(qbd, kbuf[...]) + bs_ref[...] + jnp.where(mask > 0.5, 0.0, NEG)
    o_slc = _dot(_softmax_rows(s_s).astype(BF16), vbuf[...])

    s_w = _dot_nt(qbd, wkbuf[...]) + bw_ref[...]
    o_win = _dot(_softmax_rows(s_w).astype(BF16), wvbuf[...])

    gates = jax.nn.sigmoid(g_ref[0])
    o_full = gates[:, 0:1] * o_cmp + gates[:, 1:2] * o_slc + gates[:, 2:3] * o_win
    pieces = []
    for g in range(N_KV):
        for h in range(HG):
            r0 = (g * HG + h) * tq
            pieces.append(o_full[r0:r0 + tq, g * HEAD_DIM:(g + 1) * HEAD_DIM])
    o_ref[0] = jnp.concatenate(pieces, axis=1).astype(BF16)


def _attn_sample(page_table, pool_k, pool_v, q3, gcol, kc, vc, ksn, vsn, kwn, vwn, kwl, vwl, kst, vst,
                 bc, bs, bw, c2st, expand, *, ns):
    bsz, n_pages = page_table.shape
    tq = q3.shape[1]
    n_state = kst.shape[1] // N_KV
    n_keys = bs.shape[1]
    n_wkeys = bw.shape[1]
    rows = N_HEADS * tq
    wb_new = min(WINDOW, n_state + tq)
    page_specs = [pl.BlockSpec((1, PAGE_SIZE * N_KV, HEAD_DIM),
                               functools.partial(lambda b, pt, i: (pt[b, i], 0, 0), i=i)) for i in range(n_pages)]
    per_b = lambda shape: pl.BlockSpec((1,) + shape, lambda b, pt: (b, 0, 0))
    const = lambda arr: pl.BlockSpec(arr.shape, lambda b, pt: (0, 0))
    in_specs = page_specs + page_specs + [
        per_b((tq, D_MODEL)), per_b((rows, 3)),
        per_b(kc.shape[1:]), per_b(vc.shape[1:]),
        per_b((tq, KV_W)), per_b((tq, KV_W)), per_b((tq, KV_W)), per_b((tq, KV_W)),
        per_b((tq * N_KV, HEAD_DIM)), per_b((tq * N_KV, HEAD_DIM)),
        per_b((n_state * N_KV, HEAD_DIM)), per_b((n_state * N_KV, HEAD_DIM)),
        const(bc), const(bs), const(bw), const(c2st), const(expand),
    ]
    return pl.pallas_call(
        functools.partial(_attn_sample_kernel, n_pages=n_pages, tq=tq, n_sel=min(N_SEL, ns), ns=ns, n_state=n_state),
        grid_spec=pltpu.PrefetchScalarGridSpec(
            num_scalar_prefetch=1,
            grid=(bsz,),
            in_specs=in_specs,
            out_specs=[per_b((tq, D_MODEL)), per_b((wb_new * N_KV, HEAD_DIM)), per_b((wb_new * N_KV, HEAD_DIM))],
            scratch_shapes=[
                pltpu.VMEM((n_keys, KV_W), BF16), pltpu.VMEM((n_keys, KV_W), BF16),
                pltpu.VMEM((n_wkeys, KV_W), BF16), pltpu.VMEM((n_wkeys, KV_W), BF16),
            ],
        ),
        out_shape=[
            jax.ShapeDtypeStruct((bsz, tq, D_MODEL), BF16),
            jax.ShapeDtypeStruct((bsz, wb_new * N_KV, HEAD_DIM), F32),
            jax.ShapeDtypeStruct((bsz, wb_new * N_KV, HEAD_DIM), F32),
        ],
        compiler_params=_params("parallel"),
        name="attn_sample",
    )(page_table, *([pool_k] * n_pages), *([pool_v] * n_pages), q3, gcol, kc, vc, ksn, vsn, kwn, vwn, kwl, vwl,
      kst, vst, bc, bs, bw, c2st, expand)


def _t5_bucket(dist):
    n = jnp.maximum(dist, 0)
    max_exact = N_BUCKETS // 2
    nf = jnp.maximum(n, 1).astype(F32)
    large = max_exact + (jnp.log(nf / max_exact) / math.log(MAX_DIST / max_exact) * (N_BUCKETS - max_exact)).astype(jnp.int32)
    large = jnp.minimum(large, N_BUCKETS - 1)
    return jnp.where(n < max_exact, n, large)


def _bias_by_dist(rel_bias):
    return rel_bias.astype(F32)[_t5_bucket(jnp.arange(MAX_DIST + 1))]


def _bias_tile(btab, dist, valid, shift):
    onehot = (jnp.clip(dist, 0, MAX_DIST)[..., None] == jnp.arange(MAX_DIST + 1)).astype(F32)
    vals = jnp.einsum("...d,dh->h...", onehot, (btab - shift[None, :]) * LOG2E, precision=lax.Precision.HIGHEST)
    return jnp.where(valid[None], vals, NEG)


def _heads_to_lanes(tile):
    h, k, q = tile.shape
    return tile.reshape(N_KV, HG, k, q).transpose(0, 2, 1, 3).reshape(N_KV, k, HG * q)


def _cmp_to_slc_t(nc_pad, nc, ns, ns_pad):
    cs = jnp.arange(nc_pad)[None, :] * STRIDE
    js = jnp.arange(ns_pad)[:, None] * L_SLC
    ov = jnp.clip(jnp.minimum(cs + L_CMP, js + L_SLC) - jnp.maximum(cs, js), 0, None)
    ok = (jnp.arange(nc_pad)[None, :] < nc) & (jnp.arange(ns_pad)[:, None] < ns)
    return jnp.where(ok, ov.astype(F32) / L_CMP, 0.0).astype(BF16)


def _prompt_attention(q, graw, kc, vc, ksb, vsb, kwb, vwb, btab, *, bsz, t):
    nq = t // Q_BLOCK
    nch = t // STRIDE
    nc = (t - L_CMP) // STRIDE + 1
    ns = -(-t // L_SLC)
    nwin = WINDOW // Q_BLOCK
    far = btab[MAX_DIST]
    none = jnp.zeros_like(far)
    graw_t = graw[:, :3 * N_HEADS].reshape(bsz, t, N_KV, HG, 3).transpose(0, 2, 4, 3, 1)

    def by_group(a):
        return a.reshape(bsz, t, N_KV, HEAD_DIM).transpose(0, 2, 1, 3)

    def transposed_tiles(a, tile):
        return a.reshape(bsz, N_KV, a.shape[2] // tile, tile, HEAD_DIM).transpose(0, 1, 2, 4, 3)

    ks, vs, kw, vw = (by_group(a) for a in (ksb, vsb, kwb, vwb))
    block_onehot = (jnp.arange(t)[:, None] // L_SLC == jnp.arange(128)[None, :]).astype(BF16)
    kaug = jnp.concatenate([ks, jnp.broadcast_to(block_onehot, (bsz, N_KV, t, 128))], axis=-1)
    t_far = -(-t // FAR_CHUNK) * FAR_CHUNK
    kaug = jnp.pad(kaug, ((0, 0), (0, 0), (0, t_far - t), (0, 0)))
    vst1 = transposed_tiles(vs, Q_BLOCK)
    vst4 = transposed_tiles(jnp.pad(vs, ((0, 0), (0, 0), (0, t_far - t), (0, 0))), FAR_CHUNK)
    front = ((0, 0), (0, 0), (WINDOW, 0), (0, 0))
    kwp = jnp.pad(kw, front)
    vwt = transposed_tiles(jnp.pad(vw, front), Q_BLOCK)
    vct = jnp.swapaxes(vc, 1, 2)

    kk = jnp.arange(Q_BLOCK)[:, None]
    qq = jnp.arange(Q_BLOCK)[None, :]
    always = jnp.ones((Q_BLOCK, Q_BLOCK), bool)
    t0 = _heads_to_lanes(_bias_tile(btab, qq - kk, qq - kk >= 0, far))
    t1 = _heads_to_lanes(_bias_tile(btab, Q_BLOCK + qq - kk, always, far))
    oldest = jnp.where(kk > qq, 0.0, NEG).astype(F32)
    wadd = jnp.concatenate([jnp.broadcast_to(jnp.tile(oldest, (1, HG)), (N_KV, Q_BLOCK, LANES)),
                            jnp.zeros((N_KV, (nwin - 2) * Q_BLOCK, LANES), F32), t1, t0], axis=1)
    tile_of_row = jnp.arange((nwin + 1) * Q_BLOCK) // Q_BLOCK
    is_padding = tile_of_row[None, :] < nwin - jnp.arange(nwin + 1)[:, None]
    wadd = jnp.where(is_padding[None, :, :, None], NEG, wadd[:, None])
    nshift = (Q_BLOCK // STRIDE) * (nq - 1)
    lo = min(nshift, Q_BLOCK // STRIDE + MAX_DIST // STRIDE)
    cprime = jnp.arange(-lo, Q_BLOCK // STRIDE)[:, None]
    dist_c = qq - STRIDE * cprime - (L_CMP - 1)
    pattern = _heads_to_lanes(_bias_tile(btab, dist_c, dist_c >= 0, none))
    far_rows = jnp.broadcast_to(jnp.repeat(far * LOG2E, Q_BLOCK).reshape(N_KV, 1, LANES), (N_KV, nshift - lo, LANES))
    strip = jnp.concatenate([far_rows, pattern, jnp.full((N_KV, nch - Q_BLOCK // STRIDE, LANES), NEG, F32)], axis=1)
    c2st = _cmp_to_slc_t(nch, nc, ns, -(-ns // 8) * 8)
    return _attn_prompt(q, graw_t, kc, vct, kaug, vst1, vst4, kwp, vwt, strip, t0, t1, wadd, c2st, bsz=bsz, t=t)


def _sample_attention(q, graw, kc, vc, new_bf16, new_rows, page_table, pool_k, pool_v, state_k, state_v, btab,
                      *, dbsz, dt):
    d = D_MODEL
    n_pages = page_table.shape[1]
    n_pool = pool_k.shape[0]
    n_state = state_k.shape[1]
    nch = n_pages * (PAGE_SIZE // STRIDE)
    t_all = PAST_LEN + dt
    nc = (t_all - L_CMP) // STRIDE + 1
    ns = -(-t_all // L_SLC)
    n_keys = -(-(ns * L_SLC) // 128) * 128
    n_wkeys = -(-(n_state + dt) // 128) * 128
    far = btab[MAX_DIST]
    none = jnp.zeros_like(far)
    gcol = graw[:, :3 * N_HEADS].reshape(dbsz, dt, N_HEADS, 3).transpose(0, 2, 1, 3).reshape(dbsz, N_HEADS * dt, 3)

    qpos = (PAST_LEN + jnp.arange(dt))[:, None]
    cc = jnp.arange(nch)[None, :]
    dist = qpos - (cc * STRIDE + L_CMP - 1)
    bc = _bias_tile(btab, dist, (dist >= 0) & (cc < nc), none).reshape(N_HEADS * dt, nch)
    near0 = (PAST_LEN - MAX_DIST) // 128 * 128
    kpos = jnp.arange(near0, n_keys)[None, :]
    dist = qpos - kpos
    bs = jnp.concatenate([jnp.zeros((N_HEADS * dt, near0), F32),
                          _bias_tile(btab, dist, dist >= 0, far).reshape(N_HEADS * dt, n_keys - near0)], axis=1)
    wi = jnp.arange(n_wkeys)[None, :]
    dist = qpos - (PAST_LEN - n_state + wi)
    bw = _bias_tile(btab, dist, (dist >= 0) & (dist < WINDOW) & (wi < n_state + dt), none).reshape(N_HEADS * dt, n_wkeys)
    c2st = _cmp_to_slc_t(nch, nc, ns, 128)
    expand = (jnp.arange(128)[:, None] == (jnp.arange(n_keys)[None, :] // L_SLC)).astype(BF16)

    ksn, vsn, kwn, vwn = (a.reshape(dbsz, dt, KV_W) for a in new_bf16)
    kwl, vwl = (a.reshape(dbsz, dt * N_KV, HEAD_DIM) for a in new_rows)
    o, sk_win, sv_win = _attn_sample(
        page_table, pool_k.reshape(n_pool, PAGE_SIZE * N_KV, HEAD_DIM), pool_v.reshape(n_pool, PAGE_SIZE * N_KV, HEAD_DIM),
        q.reshape(dbsz, dt, d), gcol, kc, vc, ksn, vsn, kwn, vwn, kwl, vwl,
        state_k.reshape(dbsz, n_state * N_KV, HEAD_DIM), state_v.reshape(dbsz, n_state * N_KV, HEAD_DIM),
        bc, bs, bw, c2st, expand, ns=ns)
    wb = sk_win.shape[1] // N_KV
    return (o.reshape(dbsz * dt, d), sk_win.reshape(dbsz, wb, N_KV, HEAD_DIM), sv_win.reshape(dbsz, wb, N_KV, HEAD_DIM))


def _row(v):
    return v.reshape(1, -1).astype(F32)


def kernel(x_prompt, x_sample, cache_k_cmp, cache_v_cmp, cache_k_slc, cache_v_slc, state_k_win, state_v_win, state_conv, page_table, rel_bias, conv_w_pw1, conv_b_pw1, conv_w_dw, conv_b_dw, conv_ln_g, conv_ln_b, conv_w_pw2, conv_b_pw2, nsa_w_qg, nsa_w_o, w_kv, cmp_pe, cmp_w1, cmp_b1, cmp_w2, cmp_b2, ffn_w_in, ffn_w_out, ln_mix_g, ln_mix_b, ln_ffn_g, ln_ffn_b):
    d = D_MODEL
    d_ff = ffn_w_out.shape[1]
    bsz, t, _ = x_prompt.shape
    dbsz, dt, _ = x_sample.shape
    zeros_d = jnp.zeros((1, d), F32)
    zeros_ff = jnp.zeros((1, 2 * d_ff), F32)

    w_pw2 = conv_w_pw2[0].astype(BF16)
    w_out = [ffn_w_out[layer].astype(BF16) for layer in range(DEPTH)]
    w_kvb = w_kv.astype(BF16)
    w_qg = jnp.pad(nsa_w_qg[0], ((0, 0), (0, d + 128 - nsa_w_qg.shape[2]))).astype(BF16)
    w_o = nsa_w_o[0].astype(BF16)
    w_dw = jnp.repeat(conv_w_dw[0], SUBLANES, axis=0)

    cmp_ops = []
    for i in range(2):
        pe8 = jnp.broadcast_to(cmp_pe[i].reshape(1, -1), (8, L_CMP * HEAD_DIM)).astype(BF16)
        pe_w1 = _mm_plain(pe8, cmp_w1[i].astype(BF16))[0:1]
        cmp_ops.append((_compress_weights(cmp_w1[i]), pe_w1 + _row(cmp_b1[i]), cmp_w2[i].astype(BF16),
                        _row(cmp_b2[i])))

    btab = _bias_by_dist(rel_bias)

    def ffn(xf, xb, layer):
        m = xf.shape[0]
        h = _mm_pair(xb, ffn_w_in, layer, zeros_ff, act="swiglu", out_dtype=BF16, tm=min(m, 1024), tn=512)
        return _mm_res_ln(h, w_out[layer], zeros_d, xf, _row(ln_ffn_g[layer]), _row(ln_ffn_b[layer]),
                          tm=min(m, 512), tk=d_ff // 4)

    def trunk_front(x, state, tm_conv):
        b_, t_, _ = x.shape
        m = b_ * t_
        xb = x.reshape(m, d).astype(BF16)
        u = _mm_pair(xb, conv_w_pw1, 0, _row(conv_b_pw1[0]), act="glu", out_dtype=F32, tm=min(m, 1024), tn=512)
        c = _conv_ln_silu(state, u.reshape(b_, t_, d), w_dw, _row(conv_b_dw[0]), _row(conv_ln_g[0]),
                          _row(conv_ln_b[0]), tm=tm_conv)
        x1, x1b = _mm_res_ln(c.reshape(m, d), w_pw2, _row(conv_b_pw2[0]), x.reshape(m, d),
                             _row(ln_mix_g[0]), _row(ln_mix_b[0]), tm=min(m, 512), tk=d // 2)
        x2, x2b = ffn(x1, x1b, 0)
        return x2, x2b, u

    def trunk_back(x2, o):
        x3, x3b = _mm_res_ln(o, w_o, zeros_d, x2, _row(ln_mix_g[1]), _row(ln_mix_b[1]), tm=min(x2.shape[0], 512),
                             tk=d // 2)
        y, _ = ffn(x3, x3b, 1)
        return y

    as4 = lambda a, b_, t_: a.reshape(b_, t_, N_KV, HEAD_DIM)

    x2, x2b, u_p = trunk_front(x_prompt, jnp.zeros((bsz, CONV_HIST, d), F32), 128)
    kv = _mm_kv(x2b, w_kvb, tm=256)
    kcr, vcr, ksr, vsr, kwr, vwr = kv[:N_KV_OUT]
    ksb, vsb, kwb, vwb = kv[N_KV_OUT:]
    nch_p = t // STRIDE
    ident = jnp.arange(bsz, dtype=jnp.int32).reshape(bsz, 1)
    kc_p = _compress(kcr.reshape(bsz, t * N_KV, HEAD_DIM), ident, *cmp_ops[0], pages_per_row=1, ch_per_page=nch_p)
    vc_p = _compress(vcr.reshape(bsz, t * N_KV, HEAD_DIM), ident, *cmp_ops[1], pages_per_row=1, ch_per_page=nch_p)
    q_p, graw_p = _mm_qg(x2b, w_qg, tm=512)
    natural = lambda a: a.reshape(bsz, nch_p, KV_W).astype(BF16)
    o_p = _prompt_attention(q_p, graw_p, natural(kc_p), natural(vc_p), ksb, vsb, kwb, vwb, btab, bsz=bsz, t=t)
    y_prompt = trunk_back(x2, o_p).reshape(bsz, t, d)

    wb_p = min(WINDOW, t)
    pk_cmp, pv_cmp, pk_slc, pv_slc = (as4(a, bsz, t) for a in (kcr, vcr, ksr, vsr))
    pk_win = as4(kwr, bsz, t)[:, t - wb_p:]
    pv_win = as4(vwr, bsz, t)[:, t - wb_p:]
    p_conv = u_p.reshape(bsz, t, d)[:, t - (CONV_W - 1):][None]

    st0 = state_conv[0]
    s2, s2b, u_s = trunk_front(x_sample, st0, dt)
    kv = _mm_kv(s2b, w_kvb, tm=256)
    skc, svc, sks, svs, skw, svw = kv[:N_KV_OUT]
    n_pages = page_table.shape[1]
    n_pool = cache_k_cmp.shape[0]
    ch_pp = PAGE_SIZE // STRIDE
    kc_s = _compress(cache_k_cmp.reshape(n_pool, PAGE_SIZE * N_KV, HEAD_DIM), page_table, *cmp_ops[0],
                     pages_per_row=n_pages, ch_per_page=ch_pp)
    vc_s = _compress(cache_v_cmp.reshape(n_pool, PAGE_SIZE * N_KV, HEAD_DIM), page_table, *cmp_ops[1],
                     pages_per_row=n_pages, ch_per_page=ch_pp)
    q_s, graw_s = _mm_qg(s2b, w_qg, tm=512)
    o_s, sk_win, sv_win = _sample_attention(q_s, graw_s, kc_s, vc_s, kv[N_KV_OUT:], (skw, svw), page_table,
                                            cache_k_slc, cache_v_slc, state_k_win, state_v_win, btab,
                                            dbsz=dbsz, dt=dt)
    y_sample = trunk_back(s2, o_s).reshape(dbsz, dt, d)

    sk_cmp, sv_cmp, sk_slc, sv_slc = (as4(a, dbsz, dt) for a in (skc, svc, sks, svs))
    s_conv = jnp.concatenate([st0, u_s.reshape(dbsz, dt, d)], axis=1)[:, -(CONV_W - 1):][None]

    return (y_prompt, y_sample, pk_cmp, pv_cmp, pk_slc, pv_slc, pk_win, pv_win, p_conv,
            sk_cmp, sv_cmp, sk_slc, sv_slc, sk_win, sv_win, s_conv)
```

```python
import functools
import math

import jax
import jax.numpy as jnp
from jax import lax
from jax.experimental import pallas as pl
from jax.experimental.pallas import tpu as pltpu

D_MODEL = 2048
PAST_LEN = 2048
PAGE_SIZE = 128
N_HEADS = 16
HEAD_DIM = D_MODEL // N_HEADS
N_KV = 4
HG = N_HEADS // N_KV
L_CMP = 32
STRIDE = 16
CMP_HID = HEAD_DIM
L_SLC = 64
N_SEL = 16
WINDOW = 512
Q_BLOCK = 128
CONV_W = 31
N_BUCKETS = 32
MAX_DIST = 128
DEPTH = 2
ALPHA = (2 * DEPTH) ** 0.25
LN_EPS = 1e-5
NEG = -1e30
FORCE_BONUS = 1e4
LOG2E = math.log2(math.e)

KV_W = N_KV * HEAD_DIM
LANES = HG * Q_BLOCK
FAR_CHUNK = 8 * Q_BLOCK
VMEM_LIMIT = 56 * 1024 * 1024

F32 = jnp.float32
BF16 = jnp.bfloat16


def _params(*sem):
    return pltpu.CompilerParams(dimension_semantics=sem, vmem_limit_bytes=VMEM_LIMIT)


def _dot(a, b):
    return jnp.dot(a, b, preferred_element_type=F32)


def _dot_nt(a, b):
    return lax.dot_general(a, b, (((1,), (1,)), ((), ())), preferred_element_type=F32)


def _layer_norm(x, g, b):
    mu = jnp.mean(x, axis=-1, keepdims=True)
    xc = x - mu
    var = jnp.mean(xc * xc, axis=-1, keepdims=True)
    return xc * lax.rsqrt(var + LN_EPS) * g + b


def _silu(x):
    return x * jax.nn.sigmoid(x)


def _gelu_tanh(x):
    c = math.sqrt(2.0 / math.pi)
    return 0.5 * x * (1.0 + jnp.tanh(c * (x + 0.044715 * (x * x * x))))


def _pad_rows(x, n):
    if x.shape[0] == n:
        return x
    return jnp.concatenate([x, jnp.zeros((n - x.shape[0], x.shape[1]), x.dtype)], axis=0)


def _mm_pair_kernel(x_ref, wa_ref, wb_ref, ba_ref, bb_ref, o_ref, wa_bf, wb_bf, *, act):
    @pl.when(pl.program_id(1) == 0)
    def _():
        wa_bf[...] = wa_ref[...].astype(BF16)
        wb_bf[...] = wb_ref[...].astype(BF16)

    x = x_ref[...]
    a = _dot(x, wa_bf[...]) + ba_ref[...]
    b = _dot(x, wb_bf[...]) + bb_ref[...]
    if act == "glu":
        o = a * jax.nn.sigmoid(b)
    else:
        o = _silu(a) * b
    o_ref[...] = o.astype(o_ref.dtype)


def _mm_pair(x, w, layer, bias, *, act, out_dtype, tm, tn):
    m, k = x.shape
    n = w.shape[2] // 2
    nj = n // tn
    return pl.pallas_call(
        functools.partial(_mm_pair_kernel, act=act),
        grid=(nj, m // tm),
        in_specs=[
            pl.BlockSpec((tm, k), lambda j, i: (i, 0)),
            pl.BlockSpec((None, k, tn), lambda j, i: (layer, 0, j)),
            pl.BlockSpec((None, k, tn), lambda j, i: (layer, 0, j + nj)),
            pl.BlockSpec((1, tn), lambda j, i: (0, j)),
            pl.BlockSpec((1, tn), lambda j, i: (0, j + nj)),
        ],
        out_specs=pl.BlockSpec((tm, tn), lambda j, i: (i, j)),
        out_shape=jax.ShapeDtypeStruct((m, n), out_dtype),
        scratch_shapes=[pltpu.VMEM((k, tn), BF16), pltpu.VMEM((k, tn), BF16)],
        compiler_params=_params("parallel", "arbitrary"),
        name="mm_pair_" + act,
    )(x, w, w, bias, bias)


LN_ROWS = 128


ROW_TILES_PER_WEIGHT = 2


def _mm_res_ln_kernel(h_ref, w_ref, bias_ref, xres_ref, g_ref, b_ref, of_ref, ob_ref, acc_ref):
    kk = pl.program_id(1)
    r = pl.program_id(2)

    @pl.when(kk == 0)
    def _():
        acc_ref[r] = jnp.zeros(acc_ref.shape[1:], F32)

    acc_ref[r] += _dot(h_ref[...], w_ref[...])

    @pl.when(kk == pl.num_programs(1) - 1)
    def _():
        step = min(LN_ROWS, of_ref.shape[0])
        for r0 in range(0, of_ref.shape[0], step):
            rows = slice(r0, r0 + step)
            y = ALPHA * xres_ref[rows, :] + (acc_ref[r, rows, :] + bias_ref[...])
            o = _layer_norm(y, g_ref[...], b_ref[...])
            of_ref[rows, :] = o
            ob_ref[rows, :] = o.astype(BF16)


def _mm_res_ln(h, w, layer, bias, xres, g, b, *, tm, tk):
    m, k = h.shape
    d = w.shape[2]
    sub = ROW_TILES_PER_WEIGHT
    last = k // tk - 1
    at_end = lambda i, kk, r: (jnp.where(kk == last, i * sub + r, i * sub), 0)
    vec = lambda i, kk, r: (0, 0)
    return pl.pallas_call(
        _mm_res_ln_kernel,
        grid=(m // (tm * sub), k // tk, sub),
        in_specs=[
            pl.BlockSpec((tm, tk), lambda i, kk, r: (i * sub + r, kk)),
            pl.BlockSpec((None, tk, d), lambda i, kk, r: (layer, kk, 0)),
            pl.BlockSpec((1, d), vec),
            pl.BlockSpec((tm, d), at_end),
            pl.BlockSpec((1, d), vec),
            pl.BlockSpec((1, d), vec),
        ],
        out_specs=[pl.BlockSpec((tm, d), at_end), pl.BlockSpec((tm, d), at_end)],
        out_shape=[jax.ShapeDtypeStruct((m, d), F32), jax.ShapeDtypeStruct((m, d), BF16)],
        scratch_shapes=[pltpu.VMEM((sub, tm, d), F32)],
        compiler_params=_params("parallel", "arbitrary", "arbitrary"),
        name="mm_res_ln",
    )(h, w, bias, xres, g, b)


N_KV_OUT = 6
N_KV_BF16 = 4


def _mm_kv_kernel(x_ref, w_ref, *out_refs, tm):
    x = x_ref[...]
    for idx in range(N_KV_OUT):
        acc = _dot(x, w_ref[:, idx * KV_W:(idx + 1) * KV_W])
        for g in range(N_KV):
            out_refs[idx][pl.ds(g, tm, stride=N_KV), :] = acc[:, g * HEAD_DIM:(g + 1) * HEAD_DIM]
        first_bf16 = N_KV_OUT - N_KV_BF16
        if idx >= first_bf16:
            for g in range(N_KV):
                out_refs[N_KV_OUT + idx - first_bf16][g] = acc[:, g * HEAD_DIM:(g + 1) * HEAD_DIM].astype(BF16)


def _mm_kv(x, w, *, tm):
    m, k = x.shape
    row = lambda i: (i, 0)
    return pl.pallas_call(
        functools.partial(_mm_kv_kernel, tm=tm),
        grid=(m // tm,),
        in_specs=[pl.BlockSpec((tm, k), row), pl.BlockSpec((k, N_KV_OUT * KV_W), lambda i: (0, 0))],
        out_specs=([pl.BlockSpec((tm * N_KV, HEAD_DIM), row)] * N_KV_OUT
                   + [pl.BlockSpec((N_KV, tm, HEAD_DIM), lambda i: (0, i, 0))] * N_KV_BF16),
        out_shape=([jax.ShapeDtypeStruct((m * N_KV, HEAD_DIM), F32)] * N_KV_OUT
                   + [jax.ShapeDtypeStruct((N_KV, m, HEAD_DIM), BF16)] * N_KV_BF16),
        compiler_params=_params("parallel"),
        name="mm_kv",
    )(x, w)


def _mm_qg_kernel(x_ref, w_ref, q_ref, g_ref):
    x = x_ref[...]
    scale = HEAD_DIM ** -0.5 * LOG2E
    for c in range(D_MODEL // 512):
        acc = _dot(x, w_ref[:, c * 512:(c + 1) * 512])
        q_ref[:, c * 512:(c + 1) * 512] = (acc * scale).astype(BF16)
    g_ref[...] = _dot(x, w_ref[:, D_MODEL:])


def _mm_qg(x, w, *, tm):
    m, k = x.shape
    row = lambda i: (i, 0)
    return pl.pallas_call(
        _mm_qg_kernel,
        grid=(m // tm,),
        in_specs=[pl.BlockSpec((tm, k), row), pl.BlockSpec((k, D_MODEL + 128), lambda i: (0, 0))],
        out_specs=[pl.BlockSpec((tm, D_MODEL), row), pl.BlockSpec((tm, 128), row)],
        out_shape=[jax.ShapeDtypeStruct((m, D_MODEL), BF16), jax.ShapeDtypeStruct((m, 128), F32)],
        compiler_params=_params("parallel"),
        name="mm_qg",
    )(x, w)


def _mm_plain_kernel(x_ref, w_ref, o_ref):
    o_ref[...] = _dot(x_ref[...], w_ref[...])


def _mm_plain(x, w):
    return pl.pallas_call(
        _mm_plain_kernel,
        out_shape=jax.ShapeDtypeStruct((x.shape[0], w.shape[1]), F32),
        compiler_params=pltpu.CompilerParams(vmem_limit_bytes=VMEM_LIMIT),
        name="mm_plain",
    )(x, w)


CONV_HIST = CONV_W - 1
CONV_HEAD = 32
CONV_CB = 512
SUBLANES = 8
LANE_TILE = 128


def _conv_kernel(*refs, tm, has_prev):
    if has_prev:
        state_ref, uprev_ref, ucur_ref, w_ref, bdw_ref, g_ref, b_ref, o_ref, z_ref, zs_ref, y_ref = refs
    else:
        state_ref, ucur_ref, w_ref, bdw_ref, g_ref, b_ref, o_ref, z_ref, zs_ref, y_ref = refs
    i = pl.program_id(1)
    pad = CONV_HEAD - CONV_HIST
    rows = CONV_HEAD + tm
    total = rows + SUBLANES

    @pl.when(i == 0)
    def _():
        z_ref[0:pad, :] = jnp.zeros((pad, D_MODEL), F32)
        z_ref[pad:CONV_HEAD, :] = state_ref[0]

    if has_prev:
        @pl.when(i > 0)
        def _():
            z_ref[0:CONV_HEAD, :] = uprev_ref[0]

    z_ref[CONV_HEAD:rows, :] = ucur_ref[0]
    z_ref[rows:total, :] = jnp.zeros((SUBLANES, D_MODEL), F32)
    for c0 in range(0, D_MODEL, CONV_CB):
        zc = z_ref[:, c0:c0 + CONV_CB]
        zs_ref[0, :, c0:c0 + CONV_CB] = zc[0:rows]
        for b in range(1, SUBLANES):
            zs_ref[b, :, c0:c0 + CONV_CB] = pltpu.roll(zc, total - b, 0)[0:rows]
    for c0 in range(0, D_MODEL, LANE_TILE):
        cols = slice(c0, c0 + LANE_TILE)
        taps = [w_ref[w * SUBLANES:(w + 1) * SUBLANES, cols] for w in range(CONV_W)]
        for r0 in range(0, tm, SUBLANES):
            acc = None
            for w in range(CONV_W):
                shift, base = (pad + w) % SUBLANES, (pad + w) // SUBLANES * SUBLANES
                term = zs_ref[shift, r0 + base:r0 + base + SUBLANES, cols] * taps[w]
                acc = term if acc is None else acc + term
            y_ref[r0:r0 + SUBLANES, cols] = acc
    y = _layer_norm(y_ref[...] + bdw_ref[...], g_ref[...], b_ref[...])
    o_ref[0] = _silu(y).astype(BF16)


def _conv_ln_silu(state, u, w_dw, b_dw, g, b, *, tm):
    bsz, t, d = u.shape
    has_prev = t > tm
    per = tm // CONV_HEAD
    vec = lambda bb, i: (0, 0)
    prev = [pl.BlockSpec((1, CONV_HEAD, d), lambda bb, i: (bb, jnp.maximum(i * per - 1, 0), 0))] if has_prev else []
    rows = CONV_HEAD + tm
    return pl.pallas_call(
        functools.partial(_conv_kernel, tm=tm, has_prev=has_prev),
        grid=(bsz, t // tm),
        in_specs=[pl.BlockSpec((None, 1, CONV_HIST, d), lambda bb, i: (0, bb, 0, 0))] + prev + [
            pl.BlockSpec((1, tm, d), lambda bb, i: (bb, i, 0)),
            pl.BlockSpec((CONV_W * SUBLANES, d), vec),
            pl.BlockSpec((1, d), vec),
            pl.BlockSpec((1, d), vec),
            pl.BlockSpec((1, d), vec),
        ],
        out_specs=pl.BlockSpec((1, tm, d), lambda bb, i: (bb, i, 0)),
        out_shape=jax.ShapeDtypeStruct((bsz, t, d), BF16),
        scratch_shapes=[pltpu.VMEM((rows + SUBLANES, d), F32), pltpu.VMEM((SUBLANES, rows, d), F32),
                        pltpu.VMEM((tm, d), F32)],
        compiler_params=_params("parallel", "arbitrary"),
        name="conv_ln_silu",
    )(state, *([u] if has_prev else []), u, w_dw, b_dw, g, b)


CHUNK_ROWS = STRIDE * N_KV
CHUNK_TILES = CHUNK_ROWS // SUBLANES
TOK_PER_TILE = SUBLANES // N_KV


def _compress_kernel(*refs, n_in, ch_pp):
    x_refs = refs[1:1 + n_in]
    w1k_ref, bias1_ref, w2_ref, b2_ref, o_ref = refs[1 + n_in:]
    nch = n_in * ch_pp
    rows = nch * SUBLANES
    pages = [r[0].reshape(ch_pp, CHUNK_TILES, SUBLANES, HEAD_DIM) for r in x_refs]
    lhs = []
    for k in range(CHUNK_TILES):
        tiles = [p[:, k].reshape(ch_pp * SUBLANES, HEAD_DIM) for p in pages]
        lhs.append((jnp.concatenate(tiles, axis=0) if n_in > 1 else tiles[0]).astype(BF16))
    acc = _dot(jnp.concatenate(lhs, axis=1), w1k_ref[...])
    width = 2 * CMP_HID
    comb = acc[:, 0:width]
    for jj in range(1, TOK_PER_TILE):
        comb = comb + pltpu.roll(acc[:, jj * width:(jj + 1) * width], rows - jj * N_KV, 0)
    lo = comb[:, :CMP_HID]
    hi = comb[:, CMP_HID:]
    hi_next = pltpu.roll(hi, rows - SUBLANES, 0)
    h = _gelu_tanh(lo + hi_next + bias1_ref[...])
    out = _dot(h.astype(BF16), w2_ref[...]) + b2_ref[...]
    for c in range(nch):
        o_ref[0, c * N_KV:(c + 1) * N_KV, :] = out[c * SUBLANES:c * SUBLANES + N_KV, :]


def _compress(rows2d, page_table, w1k, bias1, w2, b2, *, pages_per_row, ch_per_page):
    bsz = page_table.shape[0]
    nch = pages_per_row * ch_per_page
    const2 = lambda bb, pt: (0, 0)
    page_rows = ch_per_page * CHUNK_ROWS
    in_specs = [pl.BlockSpec((1, page_rows, HEAD_DIM), functools.partial(lambda bb, pt, i: (pt[bb, i], 0, 0), i=i))
                for i in range(pages_per_row)]
    in_specs += [
        pl.BlockSpec(w1k.shape, const2),
        pl.BlockSpec((1, CMP_HID), const2),
        pl.BlockSpec((CMP_HID, HEAD_DIM), const2),
        pl.BlockSpec((1, HEAD_DIM), const2),
    ]
    return pl.pallas_call(
        functools.partial(_compress_kernel, n_in=pages_per_row, ch_pp=ch_per_page),
        grid_spec=pltpu.PrefetchScalarGridSpec(
            num_scalar_prefetch=1,
            grid=(bsz,),
            in_specs=in_specs,
            out_specs=pl.BlockSpec((1, nch * N_KV, HEAD_DIM), lambda bb, pt: (bb, 0, 0)),
        ),
        out_shape=jax.ShapeDtypeStruct((bsz, nch * N_KV, HEAD_DIM), F32),
        compiler_params=_params("parallel"),
        name="compress",
    )(page_table, *([rows2d] * pages_per_row), w1k, bias1, w2, b2)


def _compress_weights(w1):
    w1b = w1.astype(BF16).reshape(2, STRIDE, HEAD_DIM, CMP_HID)
    per_tok = jnp.concatenate([w1b[0], w1b[1]], axis=-1)
    per_tile = per_tok.reshape(CHUNK_TILES, TOK_PER_TILE, HEAD_DIM, 2 * CMP_HID).transpose(0, 2, 1, 3)
    return per_tile.reshape(CHUNK_TILES * HEAD_DIM, TOK_PER_TILE * 2 * CMP_HID)


def _select_topk(score, n_sel):
    jn = score.shape[0]
    sub = lax.broadcasted_iota(jnp.int32, (8, score.shape[1]), 0)
    blks = [score[8 * v:8 * v + 8, :] for v in range(jn // 8)]
    cnt = [jnp.zeros(b.shape, F32) for b in blks]
    for jp in range(jn):
        row = score[jp:jp + 1, :]
        for v, blk in enumerate(blks):
            gt = jnp.where(row > blk, 1.0, 0.0)
            ge = jnp.where(row >= blk, 1.0, 0.0)
            if 8 * v + 8 <= jp:
                beats = gt
            elif 8 * v > jp:
                beats = ge
            else:
                beats = jnp.where(sub > jp % 8, ge, gt)
            cnt[v] = cnt[v] + beats
    return jnp.concatenate(cnt, axis=0) < n_sel


def _col_max(x):
    return jnp.max(x, axis=0, keepdims=True)


def _col_sum(x):
    return jnp.sum(x, axis=0, keepdims=True)


QB_PER_STEP = 2


def _attn_prompt_kernel(*refs, n_sel, nc_pad, strip_zero, nwin):
    (q_ref, graw_ref, kc_ref, vct_ref, kaug_ref, vst1_ref, vst4_ref, kwp_ref, vwt_ref,
     strip_ref, t0_ref, t1_ref) = refs[:12]
    wadd_refs = refs[12:12 + QB_PER_STEP]
    c2st_ref, o_ref = refs[12 + QB_PER_STEP:]
    per = Q_BLOCK // L_SLC
    chunk_tiles = FAR_CHUNK // Q_BLOCK
    max_far = kaug_ref.shape[2] // FAR_CHUNK

    def front(u):
        qb = pl.program_id(2) * QB_PER_STEP + u
        q = q_ref[u * Q_BLOCK:(u + 1) * Q_BLOCK, :]
        qs = jnp.concatenate([q[:, h * HEAD_DIM:(h + 1) * HEAD_DIM] for h in range(HG)], axis=0)

        kw = kwp_ref[0, 0, pl.ds(pl.multiple_of(qb * Q_BLOCK, Q_BLOCK), (nwin + 1) * Q_BLOCK), :]
        sw = _dot_nt(kw, qs) + wadd_refs[u][0, 0]
        pw = jnp.exp2(sw - _col_max(sw))
        acc_w = _dot(vwt_ref[0, 0, qb], pw[0:Q_BLOCK, :].astype(BF16))
        for dd in range(1, nwin + 1):
            acc_w = acc_w + _dot(vwt_ref[0, 0, qb + dd], pw[dd * Q_BLOCK:(dd + 1) * Q_BLOCK, :].astype(BF16))
        o_win = acc_w * (1.0 / jnp.maximum(_col_sum(pw), 1e-30))

        off = pl.multiple_of(strip_zero - (Q_BLOCK // STRIDE) * qb, 8)
        st = _dot_nt(kc_ref[0], qs) + strip_ref[0, pl.ds(off, nc_pad), :]
        m = _col_max(st)
        m = jnp.where(m < 0.1 * NEG, 0.0, m)
        e = jnp.exp2(st - m)
        p = e * (1.0 / jnp.maximum(_col_sum(e), 1e-30))
        o_cmp = _dot(vct_ref[0], p.astype(BF16))
        psum = p[:, 0:Q_BLOCK]
        for h in range(1, HG):
            psum = psum + p[:, h * Q_BLOCK:(h + 1) * Q_BLOCK]
        p_hi = psum.astype(BF16)
        p_lo = (psum - p_hi.astype(F32)).astype(BF16)
        imp = _dot(c2st_ref[...], p_hi) + _dot(c2st_ref[...], p_lo)

        j = lax.broadcasted_iota(jnp.int32, imp.shape, 0)
        cur = (qb * Q_BLOCK + lax.broadcasted_iota(jnp.int32, imp.shape, 1)) // L_SLC
        forced = (j == 0) | (j == cur) | (j == cur - 1)
        score = jnp.where(j <= cur, imp + jnp.where(forced, FORCE_BONUS, 0.0), NEG)
        sel = _select_topk(score, n_sel)

        def with_mask(keep):
            mk = _pad_rows(jnp.where(keep, 0.0, NEG), Q_BLOCK).T
            return jnp.concatenate([qs, jnp.concatenate([mk] * HG, axis=0).astype(BF16)], axis=1)

        qa_near = with_mask(sel)
        qa_far = with_mask(sel & (j < per * (qb - 1)))
        kb1 = jnp.maximum(qb - 1, 0)
        s1 = _dot_nt(kaug_ref[0, 0, pl.ds(pl.multiple_of(kb1 * Q_BLOCK, Q_BLOCK), Q_BLOCK), :], qa_near)
        s1 = s1 + jnp.where(qb >= 1, t1_ref[0], NEG)
        s0 = _dot_nt(kaug_ref[0, 0, pl.ds(pl.multiple_of(qb * Q_BLOCK, Q_BLOCK), Q_BLOCK), :], qa_near) + t0_ref[0]
        return dict(qb=qb, kb1=kb1, o_win=o_win, o_cmp=o_cmp, qa_far=qa_far, s1=s1, s0=s0)

    def far(f):
        def variant(n_chunks):
            def run(mx):
                scores = [_dot_nt(kaug_ref[0, 0, c * FAR_CHUNK:(c + 1) * FAR_CHUNK, :], f["qa_far"])
                          for c in range(n_chunks)]
                for s in scores:
                    mx = jnp.maximum(mx, _col_max(s))
                l = jnp.zeros((1, LANES), F32)
                acc = jnp.zeros((HEAD_DIM, LANES), F32)
                for c, s in enumerate(scores):
                    pc = jnp.exp2(s - mx)
                    l = l + _col_sum(pc)
                    acc = acc + _dot(vst4_ref[0, 0, c], pc.astype(BF16))
                return mx, l, acc
            return run

        mx_near = jnp.maximum(_col_max(f["s1"]), _col_max(f["s0"]))
        n_far = (jnp.maximum(f["qb"] - 1, 0) + chunk_tiles - 1) // chunk_tiles
        return lax.switch(n_far, [variant(n) for n in range(max_far + 1)], mx_near)

    def back(u, f, mx, l_s, acc_s):
        p1 = jnp.exp2(f["s1"] - mx)
        p0 = jnp.exp2(f["s0"] - mx)
        l_s = l_s + _col_sum(p1) + _col_sum(p0)
        acc_s = (acc_s + _dot(vst1_ref[0, 0, f["kb1"]], p1.astype(BF16))
                 + _dot(vst1_ref[0, 0, f["qb"]], p0.astype(BF16)))
        o_slc = acc_s * (1.0 / jnp.maximum(l_s, 1e-30))
        gates = jax.nn.sigmoid(graw_ref[0, 0, :, :, u * Q_BLOCK:(u + 1) * Q_BLOCK])
        gl = [jnp.concatenate([gates[i, h:h + 1, :] for h in range(HG)], axis=1) for i in range(3)]
        o_t = gl[0] * f["o_cmp"] + gl[1] * o_slc + gl[2] * f["o_win"]
        for h in range(HG):
            o_ref[u * Q_BLOCK:(u + 1) * Q_BLOCK, h * HEAD_DIM:(h + 1) * HEAD_DIM] = (
                o_t[:, h * Q_BLOCK:(h + 1) * Q_BLOCK].T.astype(BF16))

    fronts = [front(u) for u in range(QB_PER_STEP)]
    fars = [far(f) for f in fronts]
    for u in range(QB_PER_STEP):
        back(u, fronts[u], *fars[u])


def _attn_prompt(q, graw_t, kc, vct, kaug, vst1, vst4, kwp, vwt, strip, t0, t1, wadd, c2st, *, bsz, t):
    nq = t // Q_BLOCK
    steps = nq // QB_PER_STEP
    rows = QB_PER_STEP * Q_BLOCK
    nc_pad = kc.shape[1]
    ns = c2st.shape[0]
    nwin = WINDOW // Q_BLOCK
    per_bg = lambda arr: pl.BlockSpec((1, 1) + arr.shape[2:], lambda b, g, i: (g, b) + (0,) * (arr.ndim - 2))
    per_g = lambda arr: pl.BlockSpec((1,) + arr.shape[1:], lambda b, g, i: (g,) + (0,) * (arr.ndim - 1))
    wadd_specs = [pl.BlockSpec((1, 1) + wadd.shape[2:],
                               functools.partial(lambda b, g, i, u: (g, jnp.minimum(i * QB_PER_STEP + u, nwin), 0, 0), u=u))
                  for u in range(QB_PER_STEP)]
    return pl.pallas_call(
        functools.partial(_attn_prompt_kernel, n_sel=min(N_SEL, ns), nc_pad=nc_pad,
                          strip_zero=strip.shape[1] - nc_pad, nwin=nwin),
        grid=(bsz, N_KV, steps),
        in_specs=[
            pl.BlockSpec((rows, HG * HEAD_DIM), lambda b, g, i: (b * steps + i, g)),
            pl.BlockSpec((1, 1, 3, HG, rows), lambda b, g, i: (b, g, 0, 0, i)),
            pl.BlockSpec((1, nc_pad, HEAD_DIM), lambda b, g, i: (b, 0, g)),
            pl.BlockSpec((1, HEAD_DIM, nc_pad), lambda b, g, i: (b, g, 0)),
            per_bg(kaug), per_bg(vst1), per_bg(vst4), per_bg(kwp), per_bg(vwt),
            per_g(strip), per_g(t0), per_g(t1),
        ] + wadd_specs + [pl.BlockSpec(c2st.shape, lambda b, g, i: (0, 0))],
        out_specs=pl.BlockSpec((rows, HG * HEAD_DIM), lambda b, g, i: (b * steps + i, g)),
        out_shape=jax.ShapeDtypeStruct((bsz * t, D_MODEL), BF16),
        compiler_params=_params("parallel", "parallel", "arbitrary"),
        name="attn_prompt",
    )(q, graw_t, kc, vct, kaug, vst1, vst4, kwp, vwt, strip, t0, t1, *([wadd] * QB_PER_STEP), c2st)


def _softmax_rows(s):
    m = jnp.max(s, axis=1, keepdims=True)
    m = jnp.where(m < 0.1 * NEG, 0.0, m)
    e = jnp.exp2(s - m)
    return e * (1.0 / jnp.maximum(jnp.sum(e, axis=1, keepdims=True), 1e-30))


def _attn_sample_kernel(*refs, n_pages, tq, n_sel, ns, n_state):
    kpages = refs[1:1 + n_pages]
    vpages = refs[1 + n_pages:1 + 2 * n_pages]
    (q_ref, g_ref, kc_ref, vc_ref, ksn_ref, vsn_ref, kwn_ref, vwn_ref, kwl_ref, vwl_ref, kst_ref, vst_ref,
     bc_ref, bs_ref, bw_ref, c2st_ref, e_ref,
     o_ref, okw_ref, ovw_ref, kbuf, vbuf, wkbuf, wvbuf) = refs[1 + 2 * n_pages:]
    n_cache = n_pages * PAGE_SIZE
    tail = kbuf.shape[0] - n_cache
    wtail = wkbuf.shape[0] - n_state

    def group_rows(ref, g, n):
        return ref[0, pl.ds(g, n, stride=N_KV), :].astype(BF16)

    def new_rows(ref, n):
        return _pad_rows(ref[0].astype(F32), n).astype(BF16)

    for i in range(n_pages):
        for g in range(N_KV):
            cols = slice(g * HEAD_DIM, (g + 1) * HEAD_DIM)
            kbuf[i * PAGE_SIZE:(i + 1) * PAGE_SIZE, cols] = group_rows(kpages[i], g, PAGE_SIZE)
            vbuf[i * PAGE_SIZE:(i + 1) * PAGE_SIZE, cols] = group_rows(vpages[i], g, PAGE_SIZE)
    kbuf[n_cache:, :] = new_rows(ksn_ref, tail)
    vbuf[n_cache:, :] = new_rows(vsn_ref, tail)
    for g in range(N_KV):
        cols = slice(g * HEAD_DIM, (g + 1) * HEAD_DIM)
        wkbuf[0:n_state, cols] = group_rows(kst_ref, g, n_state)
        wvbuf[0:n_state, cols] = group_rows(vst_ref, g, n_state)
    wkbuf[n_state:, :] = new_rows(kwn_ref, wtail)
    wvbuf[n_state:, :] = new_rows(vwn_ref, wtail)
    keep = okw_ref.shape[1] - tq * N_KV
    okw_ref[0, 0:keep, :] = kst_ref[0, n_state * N_KV - keep:, :]
    okw_ref[0, keep:, :] = kwl_ref[0]
    ovw_ref[0, 0:keep, :] = vst_ref[0, n_state * N_KV - keep:, :]
    ovw_ref[0, keep:, :] = vwl_ref[0]

    q = q_ref[0].astype(F32)
    zero = jnp.zeros((tq, HEAD_DIM), F32)
    qrows = []
    for g in range(N_KV):
        for h in range(HG):
            piece = q[:, (g * HG + h) * HEAD_DIM:(g * HG + h + 1) * HEAD_DIM]
            qrows.append(jnp.concatenate([piece if gg == g else zero for gg in range(N_KV)], axis=1))
    qbd = jnp.concatenate(qrows, axis=0).astype(BF16)

    n_cmp = kc_ref.shape[1] // N_KV
    kc = jnp.concatenate([group_rows(kc_ref, g, n_cmp) for g in range(N_KV)], axis=1)
    vc = jnp.concatenate([group_rows(vc_ref, g, n_cmp) for g in range(N_KV)], axis=1)
    p_c = _softmax_rows(_dot_nt(qbd, kc) + bc_ref[...])
    o_cmp = _dot(p_c.astype(BF16), vc)
    reps = []
    for g in range(N_KV):
        s = p_c[g * HG * tq:g * HG * tq + tq, :]
        for h in range(1, HG):
            s = s + p_c[(g * HG + h) * tq:(g * HG + h + 1) * tq, :]
        reps.extend([s] * HG)
    psum = jnp.concatenate(reps, axis=0)
    p_hi = psum.astype(BF16)
    p_lo = (psum - p_hi.astype(F32)).astype(BF16)
    imp_t = _dot_nt(c2st_ref[...], p_hi) + _dot_nt(c2st_ref[...], p_lo)

    nsp = -(-ns // 8) * 8
    imp_t = imp_t[0:nsp, :]
    j = lax.broadcasted_iota(jnp.int32, imp_t.shape, 0)
    tt = lax.broadcasted_iota(jnp.int32, imp_t.shape, 1) % tq
    cur = (PAST_LEN + tt) // L_SLC
    forced = (j == 0) | (j == cur) | (j == cur - 1)
    score = jnp.where(j <= cur, imp_t + jnp.where(forced, FORCE_BONUS, 0.0), NEG)
    score = jnp.where(j < ns, score, 2.0 * NEG)
    sel_t = jnp.where(_select_topk(score, n_sel), 1.0, 0.0)
    sel = _pad_rows(sel_t, e_ref.shape[0]).T.astype(BF16)
    mask = _dot(sel, e_ref[...])

    s_s = _dot_nt(qbd, kbuf[...]) + bs_ref[...] + jnp.where(mask > 0.5, 0.0, NEG)
    o_slc = _dot(_softmax_rows(s_s).astype(BF16), vbuf[...])

    s_w = _dot_nt(qbd, wkbuf[...]) + bw_ref[...]
    o_win = _dot(_softmax_rows(s_w).astype(BF16), wvbuf[...])

    gates = jax.nn.sigmoid(g_ref[0])
    o_full = gates[:, 0:1] * o_cmp + gates[:, 1:2] * o_slc + gates[:, 2:3] * o_win
    pieces = []
    for g in range(N_KV):
        for h in range(HG):
            r0 = (g * HG + h) * tq
            pieces.append(o_full[r0:r0 + tq, g * HEAD_DIM:(g + 1) * HEAD_DIM])
    o_ref[0] = jnp.concatenate(pieces, axis=1).astype(BF16)


def _attn_sample(page_table, pool_k, pool_v, q3, gcol, kc, vc, ksn, vsn, kwn, vwn, kwl, vwl, kst, vst,
                 bc, bs, bw, c2st, expand, *, ns):
    bsz, n_pages = page_table.shape
    tq = q3.shape[1]
    n_state = kst.shape[1] // N_KV
    n_keys = bs.shape[1]
    n_wkeys = bw.shape[1]
    rows = N_HEADS * tq
    wb_new = min(WINDOW, n_state + tq)
    page_specs = [pl.BlockSpec((1, PAGE_SIZE * N_KV, HEAD_DIM),
                               functools.partial(lambda b, pt, i: (pt[b, i], 0, 0), i=i)) for i in range(n_pages)]
    per_b = lambda shape: pl.BlockSpec((1,) + shape, lambda b, pt: (b, 0, 0))
    const = lambda arr: pl.BlockSpec(arr.shape, lambda b, pt: (0, 0))
    in_specs = page_specs + page_specs + [
        per_b((tq, D_MODEL)), per_b((rows, 3)),
        per_b(kc.shape[1:]), per_b(vc.shape[1:]),
        per_b((tq, KV_W)), per_b((tq, KV_W)), per_b((tq, KV_W)), per_b((tq, KV_W)),
        per_b((tq * N_KV, HEAD_DIM)), per_b((tq * N_KV, HEAD_DIM)),
        per_b((n_state * N_KV, HEAD_DIM)), per_b((n_state * N_KV, HEAD_DIM)),
        const(bc), const(bs), const(bw), const(c2st), const(expand),
    ]
    return pl.pallas_call(
        functools.partial(_attn_sample_kernel, n_pages=n_pages, tq=tq, n_sel=min(N_SEL, ns), ns=ns, n_state=n_state),
        grid_spec=pltpu.PrefetchScalarGridSpec(
            num_scalar_prefetch=1,
            grid=(bsz,),
            in_specs=in_specs,
            out_specs=[per_b((tq, D_MODEL)), per_b((wb_new * N_KV, HEAD_DIM)), per_b((wb_new * N_KV, HEAD_DIM))],
            scratch_shapes=[
                pltpu.VMEM((n_keys, KV_W), BF16), pltpu.VMEM((n_keys, KV_W), BF16),
                pltpu.VMEM((n_wkeys, KV_W), BF16), pltpu.VMEM((n_wkeys, KV_W), BF16),
            ],
        ),
        out_shape=[
            jax.ShapeDtypeStruct((bsz, tq, D_MODEL), BF16),
            jax.ShapeDtypeStruct((bsz, wb_new * N_KV, HEAD_DIM), F32),
            jax.ShapeDtypeStruct((bsz, wb_new * N_KV, HEAD_DIM), F32),
        ],
        compiler_params=_params("parallel"),
        name="attn_sample",
    )(page_table, *([pool_k] * n_pages), *([pool_v] * n_pages), q3, gcol, kc, vc, ksn, vsn, kwn, vwn, kwl, vwl,
      kst, vst, bc, bs, bw, c2st, expand)


def _t5_bucket(dist):
    n = jnp.maximum(dist, 0)
    max_exact = N_BUCKETS // 2
    nf = jnp.maximum(n, 1).astype(F32)
    large = max_exact + (jnp.log(nf / max_exact) / math.log(MAX_DIST / max_exact) * (N_BUCKETS - max_exact)).astype(jnp.int32)
    large = jnp.minimum(large, N_BUCKETS - 1)
    return jnp.where(n < max_exact, n, large)


def _bias_by_dist(rel_bias):
    return rel_bias.astype(F32)[_t5_bucket(jnp.arange(MAX_DIST + 1))]


def _bias_tile(btab, dist, valid, shift):
    onehot = (jnp.clip(dist, 0, MAX_DIST)[..., None] == jnp.arange(MAX_DIST + 1)).astype(F32)
    vals = jnp.einsum("...d,dh->h...", onehot, (btab - shift[None, :]) * LOG2E, precision=lax.Precision.HIGHEST)
    return jnp.where(valid[None], vals, NEG)


def _heads_to_lanes(tile):
    h, k, q = tile.shape
    return tile.reshape(N_KV, HG, k, q).transpose(0, 2, 1, 3).reshape(N_KV, k, HG * q)


def _cmp_to_slc_t(nc_pad, nc, ns, ns_pad):
    cs = jnp.arange(nc_pad)[None, :] * STRIDE
    js = jnp.arange(ns_pad)[:, None] * L_SLC
    ov = jnp.clip(jnp.minimum(cs + L_CMP, js + L_SLC) - jnp.maximum(cs, js), 0, None)
    ok = (jnp.arange(nc_pad)[None, :] < nc) & (jnp.arange(ns_pad)[:, None] < ns)
    return jnp.where(ok, ov.astype(F32) / L_CMP, 0.0).astype(BF16)


def _prompt_attention(q, graw, kc, vc, ksb, vsb, kwb, vwb, btab, *, bsz, t):
    nq = t // Q_BLOCK
    nch = t // STRIDE
    nc = (t - L_CMP) // STRIDE + 1
    ns = -(-t // L_SLC)
    nwin = WINDOW // Q_BLOCK
    far = btab[MAX_DIST]
    none = jnp.zeros_like(far)
    graw_t = graw[:, :3 * N_HEADS].reshape(bsz, t, N_KV, HG, 3).transpose(0, 2, 4, 3, 1)

    def transposed_tiles(a, tile):
        return a.reshape(N_KV, bsz, a.shape[2] // tile, tile, HEAD_DIM).transpose(0, 1, 2, 4, 3)

    ks, vs, kw, vw = (a.reshape(N_KV, bsz, t, HEAD_DIM) for a in (ksb, vsb, kwb, vwb))
    block_onehot = (jnp.arange(t)[:, None] // L_SLC == jnp.arange(128)[None, :]).astype(BF16)
    kaug = jnp.concatenate([ks, jnp.broadcast_to(block_onehot, (N_KV, bsz, t, 128))], axis=-1)
    t_far = -(-t // FAR_CHUNK) * FAR_CHUNK
    kaug = jnp.pad(kaug, ((0, 0), (0, 0), (0, t_far - t), (0, 0)))
    vst1 = transposed_tiles(vs, Q_BLOCK)
    vst4 = transposed_tiles(jnp.pad(vs, ((0, 0), (0, 0), (0, t_far - t), (0, 0))), FAR_CHUNK)
    front = ((0, 0), (0, 0), (WINDOW, 0), (0, 0))
    kwp = jnp.pad(kw, front)
    vwt = transposed_tiles(jnp.pad(vw, front), Q_BLOCK)
    vct = jnp.swapaxes(vc, 1, 2)

    kk = jnp.arange(Q_BLOCK)[:, None]
    qq = jnp.arange(Q_BLOCK)[None, :]
    always = jnp.ones((Q_BLOCK, Q_BLOCK), bool)
    t0 = _heads_to_lanes(_bias_tile(btab, qq - kk, qq - kk >= 0, far))
    t1 = _heads_to_lanes(_bias_tile(btab, Q_BLOCK + qq - kk, always, far))
    oldest = jnp.where(kk > qq, 0.0, NEG).astype(F32)
    wadd = jnp.concatenate([jnp.broadcast_to(jnp.tile(oldest, (1, HG)), (N_KV, Q_BLOCK, LANES)),
                            jnp.zeros((N_KV, (nwin - 2) * Q_BLOCK, LANES), F32), t1, t0], axis=1)
    tile_of_row = jnp.arange((nwin + 1) * Q_BLOCK) // Q_BLOCK
    is_padding = tile_of_row[None, :] < nwin - jnp.arange(nwin + 1)[:, None]
    wadd = jnp.where(is_padding[None, :, :, None], NEG, wadd[:, None])
    nshift = (Q_BLOCK // STRIDE) * (nq - 1)
    lo = min(nshift, Q_BLOCK // STRIDE + MAX_DIST // STRIDE)
    cprime = jnp.arange(-lo, Q_BLOCK // STRIDE)[:, None]
    dist_c = qq - STRIDE * cprime - (L_CMP - 1)
    pattern = _heads_to_lanes(_bias_tile(btab, dist_c, dist_c >= 0, none))
    far_rows = jnp.broadcast_to(jnp.repeat(far * LOG2E, Q_BLOCK).reshape(N_KV, 1, LANES), (N_KV, nshift - lo, LANES))
    strip = jnp.concatenate([far_rows, pattern, jnp.full((N_KV, nch - Q_BLOCK // STRIDE, LANES), NEG, F32)], axis=1)
    c2st = _cmp_to_slc_t(nch, nc, ns, -(-ns // 8) * 8)
    return _attn_prompt(q, graw_t, kc, vct, kaug, vst1, vst4, kwp, vwt, strip, t0, t1, wadd, c2st, bsz=bsz, t=t)


def _sample_attention(q, graw, kc, vc, new_bf16, new_rows, page_table, pool_k, pool_v, state_k, state_v, btab,
                      *, dbsz, dt):
    d = D_MODEL
    n_pages = page_table.shape[1]
    n_pool = pool_k.shape[0]
    n_state = state_k.shape[1]
    nch = n_pages * (PAGE_SIZE // STRIDE)
    t_all = PAST_LEN + dt
    nc = (t_all - L_CMP) // STRIDE + 1
    ns = -(-t_all // L_SLC)
    n_keys = -(-(ns * L_SLC) // 128) * 128
    n_wkeys = -(-(n_state + dt) // 128) * 128
    far = btab[MAX_DIST]
    none = jnp.zeros_like(far)
    gcol = graw[:, :3 * N_HEADS].reshape(dbsz, dt, N_HEADS, 3).transpose(0, 2, 1, 3).reshape(dbsz, N_HEADS * dt, 3)

    qpos = (PAST_LEN + jnp.arange(dt))[:, None]
    cc = jnp.arange(nch)[None, :]
    dist = qpos - (cc * STRIDE + L_CMP - 1)
    bc = _bias_tile(btab, dist, (dist >= 0) & (cc < nc), none).reshape(N_HEADS * dt, nch)
    near0 = (PAST_LEN - MAX_DIST) // 128 * 128
    kpos = jnp.arange(near0, n_keys)[None, :]
    dist = qpos - kpos
    bs = jnp.concatenate([jnp.zeros((N_HEADS * dt, near0), F32),
                          _bias_tile(btab, dist, dist >= 0, far).reshape(N_HEADS * dt, n_keys - near0)], axis=1)
    wi = jnp.arange(n_wkeys)[None, :]
    dist = qpos - (PAST_LEN - n_state + wi)
    bw = _bias_tile(btab, dist, (dist >= 0) & (dist < WINDOW) & (wi < n_state + dt), none).reshape(N_HEADS * dt, n_wkeys)
    c2st = _cmp_to_slc_t(nch, nc, ns, 128)
    expand = (jnp.arange(128)[:, None] == (jnp.arange(n_keys)[None, :] // L_SLC)).astype(BF16)

    ksn, vsn, kwn, vwn = (a.transpose(1, 0, 2).reshape(dbsz, dt, KV_W) for a in new_bf16)
    kwl, vwl = (a.reshape(dbsz, dt * N_KV, HEAD_DIM) for a in new_rows)
    o, sk_win, sv_win = _attn_sample(
        page_table, pool_k.reshape(n_pool, PAGE_SIZE * N_KV, HEAD_DIM), pool_v.reshape(n_pool, PAGE_SIZE * N_KV, HEAD_DIM),
        q.reshape(dbsz, dt, d), gcol, kc, vc, ksn, vsn, kwn, vwn, kwl, vwl,
        state_k.reshape(dbsz, n_state * N_KV, HEAD_DIM), state_v.reshape(dbsz, n_state * N_KV, HEAD_DIM),
        bc, bs, bw, c2st, expand, ns=ns)
    wb = sk_win.shape[1] // N_KV
    return (o.reshape(dbsz * dt, d), sk_win.reshape(dbsz, wb, N_KV, HEAD_DIM), sv_win.reshape(dbsz, wb, N_KV, HEAD_DIM))


def _row(v):
    return v.reshape(1, -1).astype(F32)


def kernel(x_prompt, x_sample, cache_k_cmp, cache_v_cmp, cache_k_slc, cache_v_slc, state_k_win, state_v_win, state_conv, page_table, rel_bias, conv_w_pw1, conv_b_pw1, conv_w_dw, conv_b_dw, conv_ln_g, conv_ln_b, conv_w_pw2, conv_b_pw2, nsa_w_qg, nsa_w_o, w_kv, cmp_pe, cmp_w1, cmp_b1, cmp_w2, cmp_b2, ffn_w_in, ffn_w_out, ln_mix_g, ln_mix_b, ln_ffn_g, ln_ffn_b):
    d = D_MODEL
    d_ff = ffn_w_out.shape[1]
    bsz, t, _ = x_prompt.shape
    dbsz, dt, _ = x_sample.shape
    zeros_d = jnp.zeros((1, d), F32)
    zeros_ff = jnp.zeros((1, 2 * d_ff), F32)

    w_pw2 = conv_w_pw2.astype(BF16)
    w_out = ffn_w_out.astype(BF16)
    w_kvb = w_kv.astype(BF16)
    w_qg = jnp.pad(nsa_w_qg[0], ((0, 0), (0, d + 128 - nsa_w_qg.shape[2]))).astype(BF16)
    w_o = nsa_w_o.astype(BF16)
    w_dw = jnp.repeat(conv_w_dw[0], SUBLANES, axis=0)

    cmp_ops = []
    for i in range(2):
        pe8 = jnp.broadcast_to(cmp_pe[i].reshape(1, -1), (8, L_CMP * HEAD_DIM)).astype(BF16)
        pe_w1 = _mm_plain(pe8, cmp_w1[i].astype(BF16))[0:1]
        cmp_ops.append((_compress_weights(cmp_w1[i]), pe_w1 + _row(cmp_b1[i]), cmp_w2[i].astype(BF16),
                        _row(cmp_b2[i])))

    btab = _bias_by_dist(rel_bias)

    def ffn(xf, xb, layer):
        m = xf.shape[0]
        h = _mm_pair(xb, ffn_w_in, layer, zeros_ff, act="swiglu", out_dtype=BF16, tm=min(m, 1024), tn=512)
        return _mm_res_ln(h, w_out, layer, zeros_d, xf, _row(ln_ffn_g[layer]), _row(ln_ffn_b[layer]),
                          tm=min(m, 512), tk=d_ff // 4)

    def trunk_front(x, state, tm_conv):
        b_, t_, _ = x.shape
        m = b_ * t_
        xb = x.reshape(m, d).astype(BF16)
        u = _mm_pair(xb, conv_w_pw1, 0, _row(conv_b_pw1[0]), act="glu", out_dtype=F32, tm=min(m, 1024), tn=512)
        c = _conv_ln_silu(state, u.reshape(b_, t_, d), w_dw, _row(conv_b_dw[0]), _row(conv_ln_g[0]),
                          _row(conv_ln_b[0]), tm=tm_conv)
        x1, x1b = _mm_res_ln(c.reshape(m, d), w_pw2, 0, _row(conv_b_pw2[0]), x.reshape(m, d),
                             _row(ln_mix_g[0]), _row(ln_mix_b[0]), tm=min(m, 512), tk=d // 2)
        x2, x2b = ffn(x1, x1b, 0)
        return x2, x2b, u

    def trunk_back(x2, o):
        x3, x3b = _mm_res_ln(o, w_o, 0, zeros_d, x2, _row(ln_mix_g[1]), _row(ln_mix_b[1]), tm=min(x2.shape[0], 512),
                             tk=d // 2)
        y, _ = ffn(x3, x3b, 1)
        return y

    as4 = lambda a, b_, t_: a.reshape(b_, t_, N_KV, HEAD_DIM)

    x2, x2b, u_p = trunk_front(x_prompt, jnp.zeros((1, bsz, CONV_HIST, d), F32), 128)
    kv = _mm_kv(x2b, w_kvb, tm=256)
    kcr, vcr, ksr, vsr, kwr, vwr = kv[:N_KV_OUT]
    ksb, vsb, kwb, vwb = kv[N_KV_OUT:]
    nch_p = t // STRIDE
    ident = jnp.arange(bsz, dtype=jnp.int32).reshape(bsz, 1)
    kc_p = _compress(kcr.reshape(bsz, t * N_KV, HEAD_DIM), ident, *cmp_ops[0], pages_per_row=1, ch_per_page=nch_p)
    vc_p = _compress(vcr.reshape(bsz, t * N_KV, HEAD_DIM), ident, *cmp_ops[1], pages_per_row=1, ch_per_page=nch_p)
    q_p, graw_p = _mm_qg(x2b, w_qg, tm=512)
    natural = lambda a: a.reshape(bsz, nch_p, KV_W).astype(BF16)
    o_p = _prompt_attention(q_p, graw_p, natural(kc_p), natural(vc_p), ksb, vsb, kwb, vwb, btab, bsz=bsz, t=t)
    y_prompt = trunk_back(x2, o_p).reshape(bsz, t, d)

    wb_p = min(WINDOW, t)
    pk_cmp, pv_cmp, pk_slc, pv_slc = (as4(a, bsz, t) for a in (kcr, vcr, ksr, vsr))
    pk_win = as4(kwr, bsz, t)[:, t - wb_p:]
    pv_win = as4(vwr, bsz, t)[:, t - wb_p:]
    p_conv = u_p.reshape(bsz, t, d)[:, t - (CONV_W - 1):][None]

    st0 = state_conv[0]
    s2, s2b, u_s = trunk_front(x_sample, state_conv, dt)
    kv = _mm_kv(s2b, w_kvb, tm=256)
    skc, svc, sks, svs, skw, svw = kv[:N_KV_OUT]
    n_pages = page_table.shape[1]
    n_pool = cache_k_cmp.shape[0]
    ch_pp = PAGE_SIZE // STRIDE
    kc_s = _compress(cache_k_cmp.reshape(n_pool, PAGE_SIZE * N_KV, HEAD_DIM), page_table, *cmp_ops[0],
                     pages_per_row=n_pages, ch_per_page=ch_pp)
    vc_s = _compress(cache_v_cmp.reshape(n_pool, PAGE_SIZE * N_KV, HEAD_DIM), page_table, *cmp_ops[1],
                     pages_per_row=n_pages, ch_per_page=ch_pp)
    q_s, graw_s = _mm_qg(s2b, w_qg, tm=512)
    o_s, sk_win, sv_win = _sample_attention(q_s, graw_s, kc_s, vc_s, kv[N_KV_OUT:], (skw, svw), page_table,
                                            cache_k_slc, cache_v_slc, state_k_win, state_v_win, btab,
                                            dbsz=dbsz, dt=dt)
    y_sample = trunk_back(s2, o_s).reshape(dbsz, dt, d)

    sk_cmp, sv_cmp, sk_slc, sv_slc = (as4(a, dbsz, dt) for a in (skc, svc, sks, svs))
    s_conv = jnp.concatenate([st0, u_s.reshape(dbsz, dt, d)], axis=1)[:, -(CONV_W - 1):][None]

    return (y_prompt, y_sample, pk_cmp, pv_cmp, pk_slc, pv_slc, pk_win, pv_win, p_conv,
            sk_cmp, sv_cmp, sk_slc, sv_slc, sk_win, sv_win, s_conv)
```

```python
import functools
import math

import jax
import jax.numpy as jnp
from jax import lax
from jax.experimental import pallas as pl
from jax.experimental.pallas import tpu as pltpu

D_MODEL = 2048
PAST_LEN = 2048
PAGE_SIZE = 128
N_HEADS = 16
HEAD_DIM = D_MODEL // N_HEADS
N_KV = 4
HG = N_HEADS // N_KV
L_CMP = 32
STRIDE = 16
CMP_HID = HEAD_DIM
L_SLC = 64
N_SEL = 16
WINDOW = 512
Q_BLOCK = 128
CONV_W = 31
N_BUCKETS = 32
MAX_DIST = 128
DEPTH = 2
ALPHA = (2 * DEPTH) ** 0.25
LN_EPS = 1e-5
NEG = -1e30
FORCE_BONUS = 1e4
LOG2E = math.log2(math.e)

KV_W = N_KV * HEAD_DIM
LANES = HG * Q_BLOCK
FAR_CHUNK = 8 * Q_BLOCK
VMEM_LIMIT = 56 * 1024 * 1024

F32 = jnp.float32
BF16 = jnp.bfloat16


def _params(*sem):
    return pltpu.CompilerParams(dimension_semantics=sem, vmem_limit_bytes=VMEM_LIMIT)


def _dot(a, b):
    return jnp.dot(a, b, preferred_element_type=F32)


def _dot_nt(a, b):
    return lax.dot_general(a, b, (((1,), (1,)), ((), ())), preferred_element_type=F32)


def _layer_norm(x, g, b):
    mu = jnp.mean(x, axis=-1, keepdims=True)
    xc = x - mu
    var = jnp.mean(xc * xc, axis=-1, keepdims=True)
    return xc * lax.rsqrt(var + LN_EPS) * g + b


def _silu(x):
    return x * jax.nn.sigmoid(x)


def _gelu_tanh(x):
    c = math.sqrt(2.0 / math.pi)
    return 0.5 * x * (1.0 + jnp.tanh(c * (x + 0.044715 * (x * x * x))))


def _pad_rows(x, n):
    if x.shape[0] == n:
        return x
    return jnp.concatenate([x, jnp.zeros((n - x.shape[0], x.shape[1]), x.dtype)], axis=0)


def _mm_pair_kernel(x_ref, wa_ref, wb_ref, ba_ref, bb_ref, o_ref, wa_bf, wb_bf, *, act):
    @pl.when(pl.program_id(1) == 0)
    def _():
        wa_bf[...] = wa_ref[...].astype(BF16)
        wb_bf[...] = wb_ref[...].astype(BF16)

    x = x_ref[...].astype(BF16)
    a = _dot(x, wa_bf[...]) + ba_ref[...]
    b = _dot(x, wb_bf[...]) + bb_ref[...]
    if act == "glu":
        o = a * jax.nn.sigmoid(b)
    else:
        o = _silu(a) * b
    o_ref[...] = o.astype(o_ref.dtype)


def _mm_pair(x, w, layer, bias, *, act, out_dtype, tm, tn):
    m, k = x.shape
    n = w.shape[2] // 2
    nj = n // tn
    return pl.pallas_call(
        functools.partial(_mm_pair_kernel, act=act),
        grid=(nj, m // tm),
        in_specs=[
            pl.BlockSpec((tm, k), lambda j, i: (i, 0)),
            pl.BlockSpec((None, k, tn), lambda j, i: (layer, 0, j)),
            pl.BlockSpec((None, k, tn), lambda j, i: (layer, 0, j + nj)),
            pl.BlockSpec((1, tn), lambda j, i: (0, j)),
            pl.BlockSpec((1, tn), lambda j, i: (0, j + nj)),
        ],
        out_specs=pl.BlockSpec((tm, tn), lambda j, i: (i, j)),
        out_shape=jax.ShapeDtypeStruct((m, n), out_dtype),
        scratch_shapes=[pltpu.VMEM((k, tn), BF16), pltpu.VMEM((k, tn), BF16)],
        compiler_params=_params("parallel", "arbitrary"),
        name="mm_pair_" + act,
    )(x, w, w, bias, bias)


LN_ROWS = 128


ROW_TILES_PER_WEIGHT = 2


def _mm_res_ln_kernel(h_ref, w_ref, bias_ref, xres_ref, g_ref, b_ref, of_ref, ob_ref, acc_ref):
    kk = pl.program_id(1)
    r = pl.program_id(2)

    @pl.when(kk == 0)
    def _():
        acc_ref[r] = jnp.zeros(acc_ref.shape[1:], F32)

    acc_ref[r] += _dot(h_ref[...], w_ref[...])

    @pl.when(kk == pl.num_programs(1) - 1)
    def _():
        step = min(LN_ROWS, of_ref.shape[0])
        for r0 in range(0, of_ref.shape[0], step):
            rows = slice(r0, r0 + step)
            y = ALPHA * xres_ref[rows, :] + (acc_ref[r, rows, :] + bias_ref[...])
            o = _layer_norm(y, g_ref[...], b_ref[...])
            of_ref[rows, :] = o
            ob_ref[rows, :] = o.astype(BF16)


def _mm_res_ln(h, w, layer, bias, xres, g, b, *, tm, tk):
    m, k = h.shape
    d = w.shape[2]
    sub = ROW_TILES_PER_WEIGHT
    last = k // tk - 1
    at_end = lambda i, kk, r: (jnp.where(kk == last, i * sub + r, i * sub), 0)
    vec = lambda i, kk, r: (0, 0)
    return pl.pallas_call(
        _mm_res_ln_kernel,
        grid=(m // (tm * sub), k // tk, sub),
        in_specs=[
            pl.BlockSpec((tm, tk), lambda i, kk, r: (i * sub + r, kk)),
            pl.BlockSpec((None, tk, d), lambda i, kk, r: (layer, kk, 0)),
            pl.BlockSpec((1, d), vec),
            pl.BlockSpec((tm, d), at_end),
            pl.BlockSpec((1, d), vec),
            pl.BlockSpec((1, d), vec),
        ],
        out_specs=[pl.BlockSpec((tm, d), at_end), pl.BlockSpec((tm, d), at_end)],
        out_shape=[jax.ShapeDtypeStruct((m, d), F32), jax.ShapeDtypeStruct((m, d), BF16)],
        scratch_shapes=[pltpu.VMEM((sub, tm, d), F32)],
        compiler_params=_params("parallel", "arbitrary", "arbitrary"),
        name="mm_res_ln",
    )(h, w, bias, xres, g, b)


N_KV_OUT = 6
N_KV_BF16 = 4


def _mm_kv_kernel(x_ref, w_ref, *out_refs, tm):
    x = x_ref[...]
    for idx in range(N_KV_OUT):
        acc = _dot(x, w_ref[:, idx * KV_W:(idx + 1) * KV_W])
        for g in range(N_KV):
            out_refs[idx][pl.ds(g, tm, stride=N_KV), :] = acc[:, g * HEAD_DIM:(g + 1) * HEAD_DIM]
        first_bf16 = N_KV_OUT - N_KV_BF16
        if idx >= first_bf16:
            for g in range(N_KV):
                out_refs[N_KV_OUT + idx - first_bf16][g] = acc[:, g * HEAD_DIM:(g + 1) * HEAD_DIM].astype(BF16)


def _mm_kv(x, w, *, tm):
    m, k = x.shape
    row = lambda i: (i, 0)
    return pl.pallas_call(
        functools.partial(_mm_kv_kernel, tm=tm),
        grid=(m // tm,),
        in_specs=[pl.BlockSpec((tm, k), row), pl.BlockSpec((k, N_KV_OUT * KV_W), lambda i: (0, 0))],
        out_specs=([pl.BlockSpec((tm * N_KV, HEAD_DIM), row)] * N_KV_OUT
                   + [pl.BlockSpec((N_KV, tm, HEAD_DIM), lambda i: (0, i, 0))] * N_KV_BF16),
        out_shape=([jax.ShapeDtypeStruct((m * N_KV, HEAD_DIM), F32)] * N_KV_OUT
                   + [jax.ShapeDtypeStruct((N_KV, m, HEAD_DIM), BF16)] * N_KV_BF16),
        compiler_params=_params("parallel"),
        name="mm_kv",
    )(x, w)


def _mm_qg_kernel(x_ref, w_ref, q_ref, g_ref):
    x = x_ref[...]
    scale = HEAD_DIM ** -0.5 * LOG2E
    for c in range(D_MODEL // 512):
        acc = _dot(x, w_ref[:, c * 512:(c + 1) * 512])
        q_ref[:, c * 512:(c + 1) * 512] = (acc * scale).astype(BF16)
    g_ref[...] = _dot(x, w_ref[:, D_MODEL:])


def _mm_qg(x, w, *, tm):
    m, k = x.shape
    row = lambda i: (i, 0)
    return pl.pallas_call(
        _mm_qg_kernel,
        grid=(m // tm,),
        in_specs=[pl.BlockSpec((tm, k), row), pl.BlockSpec((k, D_MODEL + 128), lambda i: (0, 0))],
        out_specs=[pl.BlockSpec((tm, D_MODEL), row), pl.BlockSpec((tm, 128), row)],
        out_shape=[jax.ShapeDtypeStruct((m, D_MODEL), BF16), jax.ShapeDtypeStruct((m, 128), F32)],
        compiler_params=_params("parallel"),
        name="mm_qg",
    )(x, w)


def _mm_plain_kernel(x_ref, w_ref, o_ref):
    o_ref[...] = _dot(x_ref[...], w_ref[...])


def _mm_plain(x, w):
    return pl.pallas_call(
        _mm_plain_kernel,
        out_shape=jax.ShapeDtypeStruct((x.shape[0], w.shape[1]), F32),
        compiler_params=pltpu.CompilerParams(vmem_limit_bytes=VMEM_LIMIT),
        name="mm_plain",
    )(x, w)


CONV_HIST = CONV_W - 1
CONV_HEAD = 32
CONV_CB = 512
SUBLANES = 8
LANE_TILE = 128


def _conv_kernel(*refs, tm, has_prev):
    if has_prev:
        state_ref, uprev_ref, ucur_ref, w_ref, bdw_ref, g_ref, b_ref, o_ref, z_ref, zs_ref, y_ref = refs
    else:
        state_ref, ucur_ref, w_ref, bdw_ref, g_ref, b_ref, o_ref, z_ref, zs_ref, y_ref = refs
    i = pl.program_id(1)
    pad = CONV_HEAD - CONV_HIST
    rows = CONV_HEAD + tm
    total = rows + SUBLANES

    @pl.when(i == 0)
    def _():
        z_ref[0:pad, :] = jnp.zeros((pad, D_MODEL), F32)
        z_ref[pad:CONV_HEAD, :] = state_ref[0]

    if has_prev:
        @pl.when(i > 0)
        def _():
            z_ref[0:CONV_HEAD, :] = uprev_ref[0]

    z_ref[CONV_HEAD:rows, :] = ucur_ref[0]
    z_ref[rows:total, :] = jnp.zeros((SUBLANES, D_MODEL), F32)
    for c0 in range(0, D_MODEL, CONV_CB):
        zc = z_ref[:, c0:c0 + CONV_CB]
        zs_ref[0, :, c0:c0 + CONV_CB] = zc[0:rows]
        for b in range(1, SUBLANES):
            zs_ref[b, :, c0:c0 + CONV_CB] = pltpu.roll(zc, total - b, 0)[0:rows]
    for c0 in range(0, D_MODEL, LANE_TILE):
        cols = slice(c0, c0 + LANE_TILE)
        taps = [w_ref[w * SUBLANES:(w + 1) * SUBLANES, cols] for w in range(CONV_W)]
        for r0 in range(0, tm, SUBLANES):
            acc = None
            for w in range(CONV_W):
                shift, base = (pad + w) % SUBLANES, (pad + w) // SUBLANES * SUBLANES
                term = zs_ref[shift, r0 + base:r0 + base + SUBLANES, cols] * taps[w]
                acc = term if acc is None else acc + term
            y_ref[r0:r0 + SUBLANES, cols] = acc
    y = _layer_norm(y_ref[...] + bdw_ref[...], g_ref[...], b_ref[...])
    o_ref[0] = _silu(y).astype(BF16)


def _conv_ln_silu(state, u, w_dw, b_dw, g, b, *, tm):
    bsz, t, d = u.shape
    has_prev = t > tm
    per = tm // CONV_HEAD
    vec = lambda bb, i: (0, 0)
    prev = [pl.BlockSpec((1, CONV_HEAD, d), lambda bb, i: (bb, jnp.maximum(i * per - 1, 0), 0))] if has_prev else []
    rows = CONV_HEAD + tm
    return pl.pallas_call(
        functools.partial(_conv_kernel, tm=tm, has_prev=has_prev),
        grid=(bsz, t // tm),
        in_specs=[pl.BlockSpec((None, 1, CONV_HIST, d), lambda bb, i: (0, bb, 0, 0))] + prev + [
            pl.BlockSpec((1, tm, d), lambda bb, i: (bb, i, 0)),
            pl.BlockSpec((CONV_W * SUBLANES, d), vec),
            pl.BlockSpec((1, d), vec),
            pl.BlockSpec((1, d), vec),
            pl.BlockSpec((1, d), vec),
        ],
        out_specs=pl.BlockSpec((1, tm, d), lambda bb, i: (bb, i, 0)),
        out_shape=jax.ShapeDtypeStruct((bsz, t, d), BF16),
        scratch_shapes=[pltpu.VMEM((rows + SUBLANES, d), F32), pltpu.VMEM((SUBLANES, rows, d), F32),
                        pltpu.VMEM((tm, d), F32)],
        compiler_params=_params("parallel", "arbitrary"),
        name="conv_ln_silu",
    )(state, *([u] if has_prev else []), u, w_dw, b_dw, g, b)


CHUNK_ROWS = STRIDE * N_KV
CHUNK_TILES = CHUNK_ROWS // SUBLANES
TOK_PER_TILE = SUBLANES // N_KV


def _compress_kernel(*refs, n_in, ch_pp, n_pools):
    nch = n_in * ch_pp
    rows = nch * SUBLANES
    for pool in range(n_pools):
        x_refs = refs[1 + pool * n_in:1 + (pool + 1) * n_in]
        w1k_ref, bias1_ref, w2_ref, b2_ref = refs[1 + n_pools * n_in + 4 * pool:1 + n_pools * n_in + 4 * (pool + 1)]
        o_ref = refs[1 + n_pools * (n_in + 4) + pool]
        pages = [r[0].reshape(ch_pp, CHUNK_TILES, SUBLANES, HEAD_DIM) for r in x_refs]
        lhs = []
        for k in range(CHUNK_TILES):
            tiles = [p[:, k].reshape(ch_pp * SUBLANES, HEAD_DIM) for p in pages]
            lhs.append((jnp.concatenate(tiles, axis=0) if n_in > 1 else tiles[0]).astype(BF16))
        acc = _dot(jnp.concatenate(lhs, axis=1), w1k_ref[...])
        width = 2 * CMP_HID
        comb = acc[:, 0:width]
        for jj in range(1, TOK_PER_TILE):
            comb = comb + pltpu.roll(acc[:, jj * width:(jj + 1) * width], rows - jj * N_KV, 0)
        lo = comb[:, :CMP_HID]
        hi = comb[:, CMP_HID:]
        hi_next = pltpu.roll(hi, rows - SUBLANES, 0)
        h = _gelu_tanh(lo + hi_next + bias1_ref[...])
        out = _dot(h.astype(BF16), w2_ref[...]) + b2_ref[...]
        for c in range(nch):
            o_ref[0, c * N_KV:(c + 1) * N_KV, :] = out[c * SUBLANES:c * SUBLANES + N_KV, :]


def _compress(pools, page_table, ops, *, pages_per_row, ch_per_page):
    bsz = page_table.shape[0]
    nch = pages_per_row * ch_per_page
    const2 = lambda bb, pt: (0, 0)
    page_rows = ch_per_page * CHUNK_ROWS
    page_specs = [pl.BlockSpec((1, page_rows, HEAD_DIM), functools.partial(lambda bb, pt, i: (pt[bb, i], 0, 0), i=i))
                  for i in range(pages_per_row)]
    op_specs = [
        pl.BlockSpec(ops[0][0].shape, const2),
        pl.BlockSpec((1, CMP_HID), const2),
        pl.BlockSpec((CMP_HID, HEAD_DIM), const2),
        pl.BlockSpec((1, HEAD_DIM), const2),
    ]
    n_pools = len(pools)
    page_args = [p for pool in pools for p in [pool] * pages_per_row]
    return pl.pallas_call(
        functools.partial(_compress_kernel, n_in=pages_per_row, ch_pp=ch_per_page, n_pools=n_pools),
        grid_spec=pltpu.PrefetchScalarGridSpec(
            num_scalar_prefetch=1,
            grid=(bsz,),
            in_specs=page_specs * n_pools + op_specs * n_pools,
            out_specs=[pl.BlockSpec((1, nch * N_KV, HEAD_DIM), lambda bb, pt: (bb, 0, 0))] * n_pools,
        ),
        out_shape=[jax.ShapeDtypeStruct((bsz, nch * N_KV, HEAD_DIM), F32)] * n_pools,
        compiler_params=_params("parallel"),
        name="compress",
    )(page_table, *page_args, *[a for op in ops for a in op])


def _compress_weights(w1):
    w1b = w1.astype(BF16).reshape(2, STRIDE, HEAD_DIM, CMP_HID)
    per_tok = jnp.concatenate([w1b[0], w1b[1]], axis=-1)
    per_tile = per_tok.reshape(CHUNK_TILES, TOK_PER_TILE, HEAD_DIM, 2 * CMP_HID).transpose(0, 2, 1, 3)
    return per_tile.reshape(CHUNK_TILES * HEAD_DIM, TOK_PER_TILE * 2 * CMP_HID)


def _select_topk(score, n_sel):
    jn = score.shape[0]
    sub = lax.broadcasted_iota(jnp.int32, (8, score.shape[1]), 0)
    blks = [score[8 * v:8 * v + 8, :] for v in range(jn // 8)]
    cnt = [jnp.zeros(b.shape, F32) for b in blks]
    for jp in range(jn):
        row = score[jp:jp + 1, :]
        for v, blk in enumerate(blks):
            gt = jnp.where(row > blk, 1.0, 0.0)
            ge = jnp.where(row >= blk, 1.0, 0.0)
            if 8 * v + 8 <= jp:
                beats = gt
            elif 8 * v > jp:
                beats = ge
            else:
                beats = jnp.where(sub > jp % 8, ge, gt)
            cnt[v] = cnt[v] + beats
    return jnp.concatenate(cnt, axis=0) < n_sel


def _col_max(x):
    return jnp.max(x, axis=0, keepdims=True)


def _col_sum(x):
    return jnp.sum(x, axis=0, keepdims=True)


QB_PER_STEP = 2


def _attn_prompt_kernel(*refs, n_sel, nc_pad, strip_zero, nwin):
    (q_ref, graw_ref, kc_ref, vct_ref, kaug_ref, vst1_ref, vst4_ref, kwp_ref, vwt_ref,
     strip_ref, t0_ref, t1_ref) = refs[:12]
    wadd_refs = refs[12:12 + QB_PER_STEP]
    c2st_ref, o_ref = refs[12 + QB_PER_STEP:]
    per = Q_BLOCK // L_SLC
    chunk_tiles = FAR_CHUNK // Q_BLOCK
    max_far = kaug_ref.shape[2] // FAR_CHUNK

    def front(u):
        qb = pl.program_id(2) * QB_PER_STEP + u
        q = q_ref[u * Q_BLOCK:(u + 1) * Q_BLOCK, :]
        qs = jnp.concatenate([q[:, h * HEAD_DIM:(h + 1) * HEAD_DIM] for h in range(HG)], axis=0)

        kw = kwp_ref[0, 0, pl.ds(pl.multiple_of(qb * Q_BLOCK, Q_BLOCK), (nwin + 1) * Q_BLOCK), :]
        sw = _dot_nt(kw, qs) + wadd_refs[u][0, 0]
        pw = jnp.exp2(sw - _col_max(sw))
        acc_w = _dot(vwt_ref[0, 0, qb], pw[0:Q_BLOCK, :].astype(BF16))
        for dd in range(1, nwin + 1):
            acc_w = acc_w + _dot(vwt_ref[0, 0, qb + dd], pw[dd * Q_BLOCK:(dd + 1) * Q_BLOCK, :].astype(BF16))
        o_win = acc_w * (1.0 / jnp.maximum(_col_sum(pw), 1e-30))

        off = pl.multiple_of(strip_zero - (Q_BLOCK // STRIDE) * qb, 8)
        st = _dot_nt(kc_ref[0], qs) + strip_ref[0, pl.ds(off, nc_pad), :]
        m = _col_max(st)
        m = jnp.where(m < 0.1 * NEG, 0.0, m)
        e = jnp.exp2(st - m)
        p = e * (1.0 / jnp.maximum(_col_sum(e), 1e-30))
        o_cmp = _dot(vct_ref[0], p.astype(BF16))
        psum = p[:, 0:Q_BLOCK]
        for h in range(1, HG):
            psum = psum + p[:, h * Q_BLOCK:(h + 1) * Q_BLOCK]
        p_hi = psum.astype(BF16)
        p_lo = (psum - p_hi.astype(F32)).astype(BF16)
        imp = _dot(c2st_ref[...], p_hi) + _dot(c2st_ref[...], p_lo)

        j = lax.broadcasted_iota(jnp.int32, imp.shape, 0)
        cur = (qb * Q_BLOCK + lax.broadcasted_iota(jnp.int32, imp.shape, 1)) // L_SLC
        forced = (j == 0) | (j == cur) | (j == cur - 1)
        score = jnp.where(j <= cur, imp + jnp.where(forced, FORCE_BONUS, 0.0), NEG)
        sel = _select_topk(score, n_sel)

        def with_mask(keep):
            mk = _pad_rows(jnp.where(keep, 0.0, NEG), Q_BLOCK).T
            return jnp.concatenate([qs, jnp.concatenate([mk] * HG, axis=0).astype(BF16)], axis=1)

        qa_near = with_mask(sel)
        qa_far = with_mask(sel & (j < per * (qb - 1)))
        kb1 = jnp.maximum(qb - 1, 0)
        s1 = _dot_nt(kaug_ref[0, 0, pl.ds(pl.multiple_of(kb1 * Q_BLOCK, Q_BLOCK), Q_BLOCK), :], qa_near)
        s1 = s1 + jnp.where(qb >= 1, t1_ref[0], NEG)
        s0 = _dot_nt(kaug_ref[0, 0, pl.ds(pl.multiple_of(qb * Q_BLOCK, Q_BLOCK), Q_BLOCK), :], qa_near) + t0_ref[0]
        return dict(qb=qb, kb1=kb1, o_win=o_win, o_cmp=o_cmp, qa_far=qa_far, s1=s1, s0=s0)

    def far(f):
        def variant(n_chunks):
            def run(mx):
                scores = [_dot_nt(kaug_ref[0, 0, c * FAR_CHUNK:(c + 1) * FAR_CHUNK, :], f["qa_far"])
                          for c in range(n_chunks)]
                for s in scores:
                    mx = jnp.maximum(mx, _col_max(s))
                l = jnp.zeros((1, LANES), F32)
                acc = jnp.zeros((HEAD_DIM, LANES), F32)
                for c, s in enumerate(scores):
                    pc = jnp.exp2(s - mx)
                    l = l + _col_sum(pc)
                    acc = acc + _dot(vst4_ref[0, 0, c], pc.astype(BF16))
                return mx, l, acc
            return run

        mx_near = jnp.maximum(_col_max(f["s1"]), _col_max(f["s0"]))
        n_far = (jnp.maximum(f["qb"] - 1, 0) + chunk_tiles - 1) // chunk_tiles
        return lax.switch(n_far, [variant(n) for n in range(max_far + 1)], mx_near)

    def back(u, f, mx, l_s, acc_s):
        p1 = jnp.exp2(f["s1"] - mx)
        p0 = jnp.exp2(f["s0"] - mx)
        l_s = l_s + _col_sum(p1) + _col_sum(p0)
        acc_s = (acc_s + _dot(vst1_ref[0, 0, f["kb1"]], p1.astype(BF16))
                 + _dot(vst1_ref[0, 0, f["qb"]], p0.astype(BF16)))
        o_slc = acc_s * (1.0 / jnp.maximum(l_s, 1e-30))
        gates = jax.nn.sigmoid(graw_ref[0, 0, :, :, u * Q_BLOCK:(u + 1) * Q_BLOCK])
        gl = [jnp.concatenate([gates[i, h:h + 1, :] for h in range(HG)], axis=1) for i in range(3)]
        o_t = gl[0] * f["o_cmp"] + gl[1] * o_slc + gl[2] * f["o_win"]
        for h in range(HG):
            o_ref[u * Q_BLOCK:(u + 1) * Q_BLOCK, h * HEAD_DIM:(h + 1) * HEAD_DIM] = (
                o_t[:, h * Q_BLOCK:(h + 1) * Q_BLOCK].T.astype(BF16))

    fronts = [front(u) for u in range(QB_PER_STEP)]
    fars = [far(f) for f in fronts]
    for u in range(QB_PER_STEP):
        back(u, fronts[u], *fars[u])


def _attn_prompt(q, graw_t, kc, vct, kaug, vst1, vst4, kwp, vwt, strip, t0, t1, wadd, c2st, *, bsz, t):
    nq = t // Q_BLOCK
    steps = nq // QB_PER_STEP
    rows = QB_PER_STEP * Q_BLOCK
    nc_pad = kc.shape[1]
    ns = c2st.shape[0]
    nwin = WINDOW // Q_BLOCK
    per_bg = lambda arr: pl.BlockSpec((1, 1) + arr.shape[2:], lambda b, g, i: (g, b) + (0,) * (arr.ndim - 2))
    per_g = lambda arr: pl.BlockSpec((1,) + arr.shape[1:], lambda b, g, i: (g,) + (0,) * (arr.ndim - 1))
    wadd_specs = [pl.BlockSpec((1, 1) + wadd.shape[2:],
                               functools.partial(lambda b, g, i, u: (g, jnp.minimum(i * QB_PER_STEP + u, nwin), 0, 0), u=u))
                  for u in range(QB_PER_STEP)]
    return pl.pallas_call(
        functools.partial(_attn_prompt_kernel, n_sel=min(N_SEL, ns), nc_pad=nc_pad,
                          strip_zero=strip.shape[1] - nc_pad, nwin=nwin),
        grid=(bsz, N_KV, steps),
        in_specs=[
            pl.BlockSpec((rows, HG * HEAD_DIM), lambda b, g, i: (b * steps + i, g)),
            pl.BlockSpec((1, 1, 3, HG, rows), lambda b, g, i: (b, g, 0, 0, i)),
            pl.BlockSpec((1, nc_pad, HEAD_DIM), lambda b, g, i: (b, 0, g)),
            pl.BlockSpec((1, HEAD_DIM, nc_pad), lambda b, g, i: (b, g, 0)),
            per_bg(kaug), per_bg(vst1), per_bg(vst4), per_bg(kwp), per_bg(vwt),
            per_g(strip), per_g(t0), per_g(t1),
        ] + wadd_specs + [pl.BlockSpec(c2st.shape, lambda b, g, i: (0, 0))],
        out_specs=pl.BlockSpec((rows, HG * HEAD_DIM), lambda b, g, i: (b * steps + i, g)),
        out_shape=jax.ShapeDtypeStruct((bsz * t, D_MODEL), BF16),
        compiler_params=_params("parallel", "parallel", "arbitrary"),
        name="attn_prompt",
    )(q, graw_t, kc, vct, kaug, vst1, vst4, kwp, vwt, strip, t0, t1, *([wadd] * QB_PER_STEP), c2st)


def _softmax_rows(s):
    m = jnp.max(s, axis=1, keepdims=True)
    m = jnp.where(m < 0.1 * NEG, 0.0, m)
    e = jnp.exp2(s - m)
    return e * (1.0 / jnp.maximum(jnp.sum(e, axis=1, keepdims=True), 1e-30))


def _attn_sample_kernel(*refs, n_pages, tq, n_sel, ns, n_state):
    kpages = refs[1:1 + n_pages]
    vpages = refs[1 + n_pages:1 + 2 * n_pages]
    (q_ref, g_ref, kc_ref, vc_ref, ksn_ref, vsn_ref, kwn_ref, vwn_ref, kwl_ref, vwl_ref, kst_ref, vst_ref,
     bc_ref, bs_ref, bw_ref, c2st_ref, e_ref,
     o_ref, okw_ref, ovw_ref, kbuf, vbuf, wkbuf, wvbuf) = refs[1 + 2 * n_pages:]
    n_cache = n_pages * PAGE_SIZE
    tail = kbuf.shape[0] - n_cache
    wtail = wkbuf.shape[0] - n_state

    def group_rows(ref, g, n):
        return ref[0, pl.ds(g, n, stride=N_KV), :].astype(BF16)

    def new_rows(ref, n):
        return _pad_rows(ref[0].astype(F32), n).astype(BF16)

    for i in range(n_pages):
        for g in range(N_KV):
            cols = slice(g * HEAD_DIM, (g + 1) * HEAD_DIM)
            kbuf[i * PAGE_SIZE:(i + 1) * PAGE_SIZE, cols] = group_rows(kpages[i], g, PAGE_SIZE)
            vbuf[i * PAGE_SIZE:(i + 1) * PAGE_SIZE, cols] = group_rows(vpages[i], g, PAGE_SIZE)
    kbuf[n_cache:, :] = new_rows(ksn_ref, tail)
    vbuf[n_cache:, :] = new_rows(vsn_ref, tail)
    for g in range(N_KV):
        cols = slice(g * HEAD_DIM, (g + 1) * HEAD_DIM)
        wkbuf[0:n_state, cols] = group_rows(kst_ref, g, n_state)
        wvbuf[0:n_state, cols] = group_rows(vst_ref, g, n_state)
    wkbuf[n_state:, :] = new_rows(kwn_ref, wtail)
    wvbuf[n_state:, :] = new_rows(vwn_ref, wtail)
    keep = okw_ref.shape[1] - tq * N_KV
    okw_ref[0, 0:keep, :] = kst_ref[0, n_state * N_KV - keep:, :]
    okw_ref[0, keep:, :] = kwl_ref[0]
    ovw_ref[0, 0:keep, :] = vst_ref[0, n_state * N_KV - keep:, :]
    ovw_ref[0, keep:, :] = vwl_ref[0]

    q = q_ref[0].astype(F32)
    zero = jnp.zeros((tq, HEAD_DIM), F32)
    qrows = []
    for g in range(N_KV):
        for h in range(HG):
            piece = q[:, (g * HG + h) * HEAD_DIM:(g * HG + h + 1) * HEAD_DIM]
            qrows.append(jnp.concatenate([piece if gg == g else zero for gg in range(N_KV)], axis=1))
    qbd = jnp.concatenate(qrows, axis=0).astype(BF16)

    n_cmp = kc_ref.shape[1] // N_KV
    kc = jnp.concatenate([group_rows(kc_ref, g, n_cmp) for g in range(N_KV)], axis=1)
    vc = jnp.concatenate([group_rows(vc_ref, g, n_cmp) for g in range(N_KV)], axis=1)
    p_c = _softmax_rows(_dot_nt(qbd, kc) + bc_ref[...])
    o_cmp = _dot(p_c.astype(BF16), vc)
    reps = []
    for g in range(N_KV):
        s = p_c[g * HG * tq:g * HG * tq + tq, :]
        for h in range(1, HG):
            s = s + p_c[(g * HG + h) * tq:(g * HG + h + 1) * tq, :]
        reps.extend([s] * HG)
    psum = jnp.concatenate(reps, axis=0)
    p_hi = psum.astype(BF16)
    p_lo = (psum - p_hi.astype(F32)).astype(BF16)
    imp_t = _dot_nt(c2st_ref[...], p_hi) + _dot_nt(c2st_ref[...], p_lo)

    nsp = -(-ns // 8) * 8
    imp_t = imp_t[0:nsp, :]
    j = lax.broadcasted_iota(jnp.int32, imp_t.shape, 0)
    tt = lax.broadcasted_iota(jnp.int32, imp_t.shape, 1) % tq
    cur = (PAST_LEN + tt) // L_SLC
    forced = (j == 0) | (j == cur) | (j == cur - 1)
    score = jnp.where(j <= cur, imp_t + jnp.where(forced, FORCE_BONUS, 0.0), NEG)
    score = jnp.where(j < ns, score, 2.0 * NEG)
    sel_t = jnp.where(_select_topk(score, n_sel), 1.0, 0.0)
    sel = _pad_rows(sel_t, e_ref.shape[0]).T.astype(BF16)
    mask = _dot(sel, e_ref[...])

    s_s = _dot_nt(qbd, kbuf[...]) + bs_ref[...] + jnp.where(mask > 0.5, 0.0, NEG)
    o_slc = _dot(_softmax_rows(s_s).astype(BF16), vbuf[...])

    s_w = _dot_nt(qbd, wkbuf[...]) + bw_ref[...]
    o_win = _dot(_softmax_rows(s_w).astype(BF16), wvbuf[...])

    gates = jax.nn.sigmoid(g_ref[0])
    o_full = gates[:, 0:1] * o_cmp + gates[:, 1:2] * o_slc + gates[:, 2:3] * o_win
    pieces = []
    for g in range(N_KV):
        for h in range(HG):
            r0 = (g * HG + h) * tq
            pieces.append(o_full[r0:r0 + tq, g * HEAD_DIM:(g + 1) * HEAD_DIM])
    o_ref[0] = jnp.concatenate(pieces, axis=1).astype(BF16)


def _attn_sample(page_table, pool_k, pool_v, q3, gcol, kc, vc, ksn, vsn, kwn, vwn, kwl, vwl, kst, vst,
                 bc, bs, bw, c2st, expand, *, ns):
    bsz, n_pages = page_table.shape
    tq = q3.shape[1]
    n_state = kst.shape[1] // N_KV
    n_keys = bs.shape[1]
    n_wkeys = bw.shape[1]
    rows = N_HEADS * tq
    wb_new = min(WINDOW, n_state + tq)
    page_specs = [pl.BlockSpec((1, PAGE_SIZE * N_KV, HEAD_DIM),
                               functools.partial(lambda b, pt, i: (pt[b, i], 0, 0), i=i)) for i in range(n_pages)]
    per_b = lambda shape: pl.BlockSpec((1,) + shape, lambda b, pt: (b, 0, 0))
    const = lambda arr: pl.BlockSpec(arr.shape, lambda b, pt: (0, 0))
    in_specs = page_specs + page_specs + [
        per_b((tq, D_MODEL)), per_b((rows, 3)),
        per_b(kc.shape[1:]), per_b(vc.shape[1:]),
        per_b((tq, KV_W)), per_b((tq, KV_W)), per_b((tq, KV_W)), per_b((tq, KV_W)),
        per_b((tq * N_KV, HEAD_DIM)), per_b((tq * N_KV, HEAD_DIM)),
        per_b((n_state * N_KV, HEAD_DIM)), per_b((n_state * N_KV, HEAD_DIM)),
        const(bc), const(bs), const(bw), const(c2st), const(expand),
    ]
    return pl.pallas_call(
        functools.partial(_attn_sample_kernel, n_pages=n_pages, tq=tq, n_sel=min(N_SEL, ns), ns=ns, n_state=n_state),
        grid_spec=pltpu.PrefetchScalarGridSpec(
            num_scalar_prefetch=1,
            grid=(bsz,),
            in_specs=in_specs,
            out_specs=[per_b((tq, D_MODEL)), per_b((wb_new * N_KV, HEAD_DIM)), per_b((wb_new * N_KV, HEAD_DIM))],
            scratch_shapes=[
                pltpu.VMEM((n_keys, KV_W), BF16), pltpu.VMEM((n_keys, KV_W), BF16),
                pltpu.VMEM((n_wkeys, KV_W), BF16), pltpu.VMEM((n_wkeys, KV_W), BF16),
            ],
        ),
        out_shape=[
            jax.ShapeDtypeStruct((bsz, tq, D_MODEL), BF16),
            jax.ShapeDtypeStruct((bsz, wb_new * N_KV, HEAD_DIM), F32),
            jax.ShapeDtypeStruct((bsz, wb_new * N_KV, HEAD_DIM), F32),
        ],
        compiler_params=_params("parallel"),
        name="attn_sample",
    )(page_table, *([pool_k] * n_pages), *([pool_v] * n_pages), q3, gcol, kc, vc, ksn, vsn, kwn, vwn, kwl, vwl,
      kst, vst, bc, bs, bw, c2st, expand)


def _t5_bucket(dist):
    n = jnp.maximum(dist, 0)
    max_exact = N_BUCKETS // 2
    nf = jnp.maximum(n, 1).astype(F32)
    large = max_exact + (jnp.log(nf / max_exact) / math.log(MAX_DIST / max_exact) * (N_BUCKETS - max_exact)).astype(jnp.int32)
    large = jnp.minimum(large, N_BUCKETS - 1)
    return jnp.where(n < max_exact, n, large)


def _bias_by_dist(rel_bias):
    return rel_bias.astype(F32)[_t5_bucket(jnp.arange(MAX_DIST + 1))]


def _bias_tile(btab, dist, valid, shift):
    onehot = (jnp.clip(dist, 0, MAX_DIST)[..., None] == jnp.arange(MAX_DIST + 1)).astype(F32)
    vals = jnp.einsum("...d,dh->h...", onehot, (btab - shift[None, :]) * LOG2E, precision=lax.Precision.HIGHEST)
    return jnp.where(valid[None], vals, NEG)


def _heads_to_lanes(tile):
    h, k, q = tile.shape
    return tile.reshape(N_KV, HG, k, q).transpose(0, 2, 1, 3).reshape(N_KV, k, HG * q)


def _cmp_to_slc_t(nc_pad, nc, ns, ns_pad):
    cs = jnp.arange(nc_pad)[None, :] * STRIDE
    js = jnp.arange(ns_pad)[:, None] * L_SLC
    ov = jnp.clip(jnp.minimum(cs + L_CMP, js + L_SLC) - jnp.maximum(cs, js), 0, None)
    ok = (jnp.arange(nc_pad)[None, :] < nc) & (jnp.arange(ns_pad)[:, None] < ns)
    return jnp.where(ok, ov.astype(F32) / L_CMP, 0.0).astype(BF16)


def _prompt_attention(q, graw, kc, vc, ksb, vsb, kwb, vwb, btab, *, bsz, t):
    nq = t // Q_BLOCK
    nch = t // STRIDE
    nc = (t - L_CMP) // STRIDE + 1
    ns = -(-t // L_SLC)
    nwin = WINDOW // Q_BLOCK
    far = btab[MAX_DIST]
    none = jnp.zeros_like(far)
    graw_t = graw[:, :3 * N_HEADS].reshape(bsz, t, N_KV, HG, 3).transpose(0, 2, 4, 3, 1)

    def transposed_tiles(a, tile):
        return a.reshape(N_KV, bsz, a.shape[2] // tile, tile, HEAD_DIM).transpose(0, 1, 2, 4, 3)

    ks, vs, kw, vw = (a.reshape(N_KV, bsz, t, HEAD_DIM) for a in (ksb, vsb, kwb, vwb))
    block_onehot = (jnp.arange(t)[:, None] // L_SLC == jnp.arange(128)[None, :]).astype(BF16)
    kaug = jnp.concatenate([ks, jnp.broadcast_to(block_onehot, (N_KV, bsz, t, 128))], axis=-1)
    t_far = -(-t // FAR_CHUNK) * FAR_CHUNK
    kaug = jnp.pad(kaug, ((0, 0), (0, 0), (0, t_far - t), (0, 0)))
    vst1 = transposed_tiles(vs, Q_BLOCK)
    vst4 = transposed_tiles(jnp.pad(vs, ((0, 0), (0, 0), (0, t_far - t), (0, 0))), FAR_CHUNK)
    front = ((0, 0), (0, 0), (WINDOW, 0), (0, 0))
    kwp = jnp.pad(kw, front)
    vwt = transposed_tiles(jnp.pad(vw, front), Q_BLOCK)
    vct = jnp.swapaxes(vc, 1, 2)

    kk = jnp.arange(Q_BLOCK)[:, None]
    qq = jnp.arange(Q_BLOCK)[None, :]
    always = jnp.ones((Q_BLOCK, Q_BLOCK), bool)
    t0 = _heads_to_lanes(_bias_tile(btab, qq - kk, qq - kk >= 0, far))
    t1 = _heads_to_lanes(_bias_tile(btab, Q_BLOCK + qq - kk, always, far))
    oldest = jnp.where(kk > qq, 0.0, NEG).astype(F32)
    wadd = jnp.concatenate([jnp.broadcast_to(jnp.tile(oldest, (1, HG)), (N_KV, Q_BLOCK, LANES)),
                            jnp.zeros((N_KV, (nwin - 2) * Q_BLOCK, LANES), F32), t1, t0], axis=1)
    tile_of_row = jnp.arange((nwin + 1) * Q_BLOCK) // Q_BLOCK
    is_padding = tile_of_row[None, :] < nwin - jnp.arange(nwin + 1)[:, None]
    wadd = jnp.where(is_padding[None, :, :, None], NEG, wadd[:, None])
    nshift = (Q_BLOCK // STRIDE) * (nq - 1)
    lo = min(nshift, Q_BLOCK // STRIDE + MAX_DIST // STRIDE)
    cprime = jnp.arange(-lo, Q_BLOCK // STRIDE)[:, None]
    dist_c = qq - STRIDE * cprime - (L_CMP - 1)
    pattern = _heads_to_lanes(_bias_tile(btab, dist_c, dist_c >= 0, none))
    far_rows = jnp.broadcast_to(jnp.repeat(far * LOG2E, Q_BLOCK).reshape(N_KV, 1, LANES), (N_KV, nshift - lo, LANES))
    strip = jnp.concatenate([far_rows, pattern, jnp.full((N_KV, nch - Q_BLOCK // STRIDE, LANES), NEG, F32)], axis=1)
    c2st = _cmp_to_slc_t(nch, nc, ns, -(-ns // 8) * 8)
    return _attn_prompt(q, graw_t, kc, vct, kaug, vst1, vst4, kwp, vwt, strip, t0, t1, wadd, c2st, bsz=bsz, t=t)


def _sample_attention(q, graw, kc, vc, new_bf16, new_rows, page_table, pool_k, pool_v, state_k, state_v, btab,
                      *, dbsz, dt):
    d = D_MODEL
    n_pages = page_table.shape[1]
    n_pool = pool_k.shape[0]
    n_state = state_k.shape[1]
    nch = n_pages * (PAGE_SIZE // STRIDE)
    t_all = PAST_LEN + dt
    nc = (t_all - L_CMP) // STRIDE + 1
    ns = -(-t_all // L_SLC)
    n_keys = -(-(ns * L_SLC) // 128) * 128
    n_wkeys = -(-(n_state + dt) // 128) * 128
    far = btab[MAX_DIST]
    none = jnp.zeros_like(far)
    gcol = graw[:, :3 * N_HEADS].reshape(dbsz, dt, N_HEADS, 3).transpose(0, 2, 1, 3).reshape(dbsz, N_HEADS * dt, 3)

    qpos = (PAST_LEN + jnp.arange(dt))[:, None]
    cc = jnp.arange(nch)[None, :]
    dist = qpos - (cc * STRIDE + L_CMP - 1)
    bc = _bias_tile(btab, dist, (dist >= 0) & (cc < nc), none).reshape(N_HEADS * dt, nch)
    near0 = (PAST_LEN - MAX_DIST) // 128 * 128
    kpos = jnp.arange(near0, n_keys)[None, :]
    dist = qpos - kpos
    bs = jnp.concatenate([jnp.zeros((N_HEADS * dt, near0), F32),
                          _bias_tile(btab, dist, dist >= 0, far).reshape(N_HEADS * dt, n_keys - near0)], axis=1)
    wi = jnp.arange(n_wkeys)[None, :]
    dist = qpos - (PAST_LEN - n_state + wi)
    bw = _bias_tile(btab, dist, (dist >= 0) & (dist < WINDOW) & (wi < n_state + dt), none).reshape(N_HEADS * dt, n_wkeys)
    c2st = _cmp_to_slc_t(nch, nc, ns, 128)
    expand = (jnp.arange(128)[:, None] == (jnp.arange(n_keys)[None, :] // L_SLC)).astype(BF16)

    ksn, vsn, kwn, vwn = (a.transpose(1, 0, 2).reshape(dbsz, dt, KV_W) for a in new_bf16)
    kwl, vwl = (a.reshape(dbsz, dt * N_KV, HEAD_DIM) for a in new_rows)
    o, sk_win, sv_win = _attn_sample(
        page_table, pool_k.reshape(n_pool, PAGE_SIZE * N_KV, HEAD_DIM), pool_v.reshape(n_pool, PAGE_SIZE * N_KV, HEAD_DIM),
        q.reshape(dbsz, dt, d), gcol, kc, vc, ksn, vsn, kwn, vwn, kwl, vwl,
        state_k.reshape(dbsz, n_state * N_KV, HEAD_DIM), state_v.reshape(dbsz, n_state * N_KV, HEAD_DIM),
        bc, bs, bw, c2st, expand, ns=ns)
    wb = sk_win.shape[1] // N_KV
    return (o.reshape(dbsz * dt, d), sk_win.reshape(dbsz, wb, N_KV, HEAD_DIM), sv_win.reshape(dbsz, wb, N_KV, HEAD_DIM))


def _row(v):
    return v.reshape(1, -1).astype(F32)


def kernel(x_prompt, x_sample, cache_k_cmp, cache_v_cmp, cache_k_slc, cache_v_slc, state_k_win, state_v_win, state_conv, page_table, rel_bias, conv_w_pw1, conv_b_pw1, conv_w_dw, conv_b_dw, conv_ln_g, conv_ln_b, conv_w_pw2, conv_b_pw2, nsa_w_qg, nsa_w_o, w_kv, cmp_pe, cmp_w1, cmp_b1, cmp_w2, cmp_b2, ffn_w_in, ffn_w_out, ln_mix_g, ln_mix_b, ln_ffn_g, ln_ffn_b):
    d = D_MODEL
    d_ff = ffn_w_out.shape[1]
    bsz, t, _ = x_prompt.shape
    dbsz, dt, _ = x_sample.shape
    zeros_d = jnp.zeros((1, d), F32)
    zeros_ff = jnp.zeros((1, 2 * d_ff), F32)

    w_pw2 = conv_w_pw2.astype(BF16)
    w_out = ffn_w_out.astype(BF16)
    w_kvb = w_kv.astype(BF16)
    w_qg = jnp.pad(nsa_w_qg[0], ((0, 0), (0, d + 128 - nsa_w_qg.shape[2]))).astype(BF16)
    w_o = nsa_w_o.astype(BF16)
    w_dw = jnp.repeat(conv_w_dw[0], SUBLANES, axis=0)

    cmp_ops = []
    for i in range(2):
        pe8 = jnp.broadcast_to(cmp_pe[i].reshape(1, -1), (8, L_CMP * HEAD_DIM)).astype(BF16)
        pe_w1 = _mm_plain(pe8, cmp_w1[i].astype(BF16))[0:1]
        cmp_ops.append((_compress_weights(cmp_w1[i]), pe_w1 + _row(cmp_b1[i]), cmp_w2[i].astype(BF16),
                        _row(cmp_b2[i])))

    btab = _bias_by_dist(rel_bias)

    def ffn(xf, xb, layer):
        m = xf.shape[0]
        h = _mm_pair(xb, ffn_w_in, layer, zeros_ff, act="swiglu", out_dtype=BF16, tm=min(m, 1024), tn=512)
        return _mm_res_ln(h, w_out, layer, zeros_d, xf, _row(ln_ffn_g[layer]), _row(ln_ffn_b[layer]),
                          tm=min(m, 512), tk=d_ff // 4)

    def trunk_front(x, state, tm_conv):
        b_, t_, _ = x.shape
        m = b_ * t_
        u = _mm_pair(x.reshape(m, d), conv_w_pw1, 0, _row(conv_b_pw1[0]), act="glu", out_dtype=F32, tm=min(m, 1024), tn=512)
        c = _conv_ln_silu(state, u.reshape(b_, t_, d), w_dw, _row(conv_b_dw[0]), _row(conv_ln_g[0]),
                          _row(conv_ln_b[0]), tm=tm_conv)
        x1, x1b = _mm_res_ln(c.reshape(m, d), w_pw2, 0, _row(conv_b_pw2[0]), x.reshape(m, d),
                             _row(ln_mix_g[0]), _row(ln_mix_b[0]), tm=min(m, 512), tk=d // 2)
        x2, x2b = ffn(x1, x1b, 0)
        return x2, x2b, u

    def trunk_back(x2, o):
        x3, x3b = _mm_res_ln(o, w_o, 0, zeros_d, x2, _row(ln_mix_g[1]), _row(ln_mix_b[1]), tm=min(x2.shape[0], 512),
                             tk=d // 2)
        y, _ = ffn(x3, x3b, 1)
        return y

    as4 = lambda a, b_, t_: a.reshape(b_, t_, N_KV, HEAD_DIM)

    x2, x2b, u_p = trunk_front(x_prompt, jnp.zeros((1, bsz, CONV_HIST, d), F32), 128)
    kv = _mm_kv(x2b, w_kvb, tm=256)
    kcr, vcr, ksr, vsr, kwr, vwr = kv[:N_KV_OUT]
    ksb, vsb, kwb, vwb = kv[N_KV_OUT:]
    nch_p = t // STRIDE
    ident = jnp.arange(bsz, dtype=jnp.int32).reshape(bsz, 1)
    kc_p, = _compress([kcr.reshape(bsz, t * N_KV, HEAD_DIM)], ident, cmp_ops[0:1], pages_per_row=1, ch_per_page=nch_p)
    vc_p, = _compress([vcr.reshape(bsz, t * N_KV, HEAD_DIM)], ident, cmp_ops[1:2], pages_per_row=1, ch_per_page=nch_p)
    q_p, graw_p = _mm_qg(x2b, w_qg, tm=512)
    natural = lambda a: a.reshape(bsz, nch_p, KV_W).astype(BF16)
    o_p = _prompt_attention(q_p, graw_p, natural(kc_p), natural(vc_p), ksb, vsb, kwb, vwb, btab, bsz=bsz, t=t)
    y_prompt = trunk_back(x2, o_p).reshape(bsz, t, d)

    wb_p = min(WINDOW, t)
    pk_cmp, pv_cmp, pk_slc, pv_slc = (as4(a, bsz, t) for a in (kcr, vcr, ksr, vsr))
    pk_win = as4(kwr, bsz, t)[:, t - wb_p:]
    pv_win = as4(vwr, bsz, t)[:, t - wb_p:]
    p_conv = u_p.reshape(bsz, t, d)[:, t - (CONV_W - 1):][None]

    st0 = state_conv[0]
    s2, s2b, u_s = trunk_front(x_sample, state_conv, dt)
    kv = _mm_kv(s2b, w_kvb, tm=256)
    skc, svc, sks, svs, skw, svw = kv[:N_KV_OUT]
    n_pages = page_table.shape[1]
    n_pool = cache_k_cmp.shape[0]
    ch_pp = PAGE_SIZE // STRIDE
    kc_s, vc_s = _compress([cache_k_cmp.reshape(n_pool, PAGE_SIZE * N_KV, HEAD_DIM),
                            cache_v_cmp.reshape(n_pool, PAGE_SIZE * N_KV, HEAD_DIM)], page_table, cmp_ops,
                           pages_per_row=n_pages, ch_per_page=ch_pp)
    q_s, graw_s = _mm_qg(s2b, w_qg, tm=512)
    o_s, sk_win, sv_win = _sample_attention(q_s, graw_s, kc_s, vc_s, kv[N_KV_OUT:], (skw, svw), page_table,
                                            cache_k_slc, cache_v_slc, state_k_win, state_v_win, btab,
                                            dbsz=dbsz, dt=dt)
    y_sample = trunk_back(s2, o_s).reshape(dbsz, dt, d)

    sk_cmp, sv_cmp, sk_slc, sv_slc = (as4(a, dbsz, dt) for a in (skc, svc, sks, svs))
    s_conv = jnp.concatenate([st0, u_s.reshape(dbsz, dt, d)], axis=1)[:, -(CONV_W - 1):][None]

    return (y_prompt, y_sample, pk_cmp, pv_cmp, pk_slc, pv_slc, pk_win, pv_win, p_conv,
            sk_cmp, sv_cmp, sk_slc, sv_slc, sk_win, sv_win, s_conv)
```

```python
import functools
import math

import jax
import jax.numpy as jnp
from jax import lax
from jax.experimental import pallas as pl
from jax.experimental.pallas import tpu as pltpu

D_MODEL = 2048
PAST_LEN = 2048
PAGE_SIZE = 128
N_HEADS = 16
HEAD_DIM = D_MODEL // N_HEADS
N_KV = 4
HG = N_HEADS // N_KV
L_CMP = 32
STRIDE = 16
CMP_HID = HEAD_DIM
L_SLC = 64
N_SEL = 16
WINDOW = 512
Q_BLOCK = 128
CONV_W = 31
N_BUCKETS = 32
MAX_DIST = 128
DEPTH = 2
ALPHA = (2 * DEPTH) ** 0.25
LN_EPS = 1e-5
NEG = -1e30
FORCE_BONUS = 1e4
LOG2E = math.log2(math.e)

KV_W = N_KV * HEAD_DIM
LANES = HG * Q_BLOCK
FAR_CHUNK = 8 * Q_BLOCK
VMEM_LIMIT = 56 * 1024 * 1024

F32 = jnp.float32
BF16 = jnp.bfloat16


def _params(*sem):
    return pltpu.CompilerParams(dimension_semantics=sem, vmem_limit_bytes=VMEM_LIMIT)


def _dot(a, b):
    return jnp.dot(a, b, preferred_element_type=F32)


def _dot_nt(a, b):
    return lax.dot_general(a, b, (((1,), (1,)), ((), ())), preferred_element_type=F32)


def _layer_norm(x, g, b):
    mu = jnp.mean(x, axis=-1, keepdims=True)
    xc = x - mu
    var = jnp.mean(xc * xc, axis=-1, keepdims=True)
    return xc * lax.rsqrt(var + LN_EPS) * g + b


def _silu(x):
    return x * jax.nn.sigmoid(x)


def _gelu_tanh(x):
    c = math.sqrt(2.0 / math.pi)
    return 0.5 * x * (1.0 + jnp.tanh(c * (x + 0.044715 * (x * x * x))))


def _pad_rows(x, n):
    if x.shape[0] == n:
        return x
    return jnp.concatenate([x, jnp.zeros((n - x.shape[0], x.shape[1]), x.dtype)], axis=0)


def _mm_pair_kernel(x_ref, wa_ref, wb_ref, ba_ref, bb_ref, o_ref, wa_bf, wb_bf, *, act):
    @pl.when(pl.program_id(1) == 0)
    def _():
        wa_bf[...] = wa_ref[...].astype(BF16)
        wb_bf[...] = wb_ref[...].astype(BF16)

    x = x_ref[...].astype(BF16)
    a = _dot(x, wa_bf[...]) + ba_ref[...]
    b = _dot(x, wb_bf[...]) + bb_ref[...]
    if act == "glu":
        o = a * jax.nn.sigmoid(b)
    else:
        o = _silu(a) * b
    o_ref[...] = o.astype(o_ref.dtype)


def _mm_pair(x, w, layer, bias, *, act, out_dtype, tm, tn):
    m, k = x.shape
    n = w.shape[2] // 2
    nj = n // tn
    return pl.pallas_call(
        functools.partial(_mm_pair_kernel, act=act),
        grid=(nj, m // tm),
        in_specs=[
            pl.BlockSpec((tm, k), lambda j, i: (i, 0)),
            pl.BlockSpec((None, k, tn), lambda j, i: (layer, 0, j)),
            pl.BlockSpec((None, k, tn), lambda j, i: (layer, 0, j + nj)),
            pl.BlockSpec((1, tn), lambda j, i: (0, j)),
            pl.BlockSpec((1, tn), lambda j, i: (0, j + nj)),
        ],
        out_specs=pl.BlockSpec((tm, tn), lambda j, i: (i, j)),
        out_shape=jax.ShapeDtypeStruct((m, n), out_dtype),
        scratch_shapes=[pltpu.VMEM((k, tn), BF16), pltpu.VMEM((k, tn), BF16)],
        compiler_params=_params("parallel", "arbitrary"),
        name="mm_pair_" + act,
    )(x, w, w, bias, bias)


LN_ROWS = 128


ROW_TILES_PER_WEIGHT = 2


def _mm_res_ln_kernel(h_ref, w_ref, bias_ref, xres_ref, g_ref, b_ref, of_ref, ob_ref, acc_ref):
    kk = pl.program_id(1)
    r = pl.program_id(2)

    @pl.when(kk == 0)
    def _():
        acc_ref[r] = jnp.zeros(acc_ref.shape[1:], F32)

    acc_ref[r] += _dot(h_ref[...], w_ref[...])

    @pl.when(kk == pl.num_programs(1) - 1)
    def _():
        step = min(LN_ROWS, of_ref.shape[0])
        for r0 in range(0, of_ref.shape[0], step):
            rows = slice(r0, r0 + step)
            y = ALPHA * xres_ref[rows, :] + (acc_ref[r, rows, :] + bias_ref[...])
            o = _layer_norm(y, g_ref[...], b_ref[...])
            of_ref[rows, :] = o
            ob_ref[rows, :] = o.astype(BF16)


def _mm_res_ln(h, w, layer, bias, xres, g, b, *, tm, tk):
    m, k = h.shape
    d = w.shape[2]
    sub = ROW_TILES_PER_WEIGHT
    last = k // tk - 1
    at_end = lambda i, kk, r: (jnp.where(kk == last, i * sub + r, i * sub), 0)
    vec = lambda i, kk, r: (0, 0)
    return pl.pallas_call(
        _mm_res_ln_kernel,
        grid=(m // (tm * sub), k // tk, sub),
        in_specs=[
            pl.BlockSpec((tm, tk), lambda i, kk, r: (i * sub + r, kk)),
            pl.BlockSpec((None, tk, d), lambda i, kk, r: (layer, kk, 0)),
            pl.BlockSpec((1, d), vec),
            pl.BlockSpec((tm, d), at_end),
            pl.BlockSpec((1, d), vec),
            pl.BlockSpec((1, d), vec),
        ],
        out_specs=[pl.BlockSpec((tm, d), at_end), pl.BlockSpec((tm, d), at_end)],
        out_shape=[jax.ShapeDtypeStruct((m, d), F32), jax.ShapeDtypeStruct((m, d), BF16)],
        scratch_shapes=[pltpu.VMEM((sub, tm, d), F32)],
        compiler_params=_params("parallel", "arbitrary", "arbitrary"),
        name="mm_res_ln",
    )(h, w, bias, xres, g, b)


N_KV_OUT = 6
N_KV_BF16 = 4


def _mm_kv_kernel(x_ref, w_ref, *out_refs, tm):
    x = x_ref[...]
    for idx in range(N_KV_OUT):
        acc = _dot(x, w_ref[:, idx * KV_W:(idx + 1) * KV_W])
        for g in range(N_KV):
            out_refs[idx][pl.ds(g, tm, stride=N_KV), :] = acc[:, g * HEAD_DIM:(g + 1) * HEAD_DIM]
        first_bf16 = N_KV_OUT - N_KV_BF16
        if idx >= first_bf16:
            for g in range(N_KV):
                out_refs[N_KV_OUT + idx - first_bf16][g] = acc[:, g * HEAD_DIM:(g + 1) * HEAD_DIM].astype(BF16)


def _mm_kv(x, w, *, tm):
    m, k = x.shape
    row = lambda i: (i, 0)
    return pl.pallas_call(
        functools.partial(_mm_kv_kernel, tm=tm),
        grid=(m // tm,),
        in_specs=[pl.BlockSpec((tm, k), row), pl.BlockSpec((k, N_KV_OUT * KV_W), lambda i: (0, 0))],
        out_specs=([pl.BlockSpec((tm * N_KV, HEAD_DIM), row)] * N_KV_OUT
                   + [pl.BlockSpec((N_KV, tm, HEAD_DIM), lambda i: (0, i, 0))] * N_KV_BF16),
        out_shape=([jax.ShapeDtypeStruct((m * N_KV, HEAD_DIM), F32)] * N_KV_OUT
                   + [jax.ShapeDtypeStruct((N_KV, m, HEAD_DIM), BF16)] * N_KV_BF16),
        compiler_params=_params("parallel"),
        name="mm_kv",
    )(x, w)


def _mm_qg_kernel(x_ref, w_ref, q_ref, g_ref):
    x = x_ref[...]
    scale = HEAD_DIM ** -0.5 * LOG2E
    for c in range(D_MODEL // 512):
        acc = _dot(x, w_ref[:, c * 512:(c + 1) * 512])
        q_ref[:, c * 512:(c + 1) * 512] = (acc * scale).astype(BF16)
    g_ref[...] = _dot(x, w_ref[:, D_MODEL:])


def _mm_qg(x, w, *, tm):
    m, k = x.shape
    row = lambda i: (i, 0)
    return pl.pallas_call(
        _mm_qg_kernel,
        grid=(m // tm,),
        in_specs=[pl.BlockSpec((tm, k), row), pl.BlockSpec((k, D_MODEL + 128), lambda i: (0, 0))],
        out_specs=[pl.BlockSpec((tm, D_MODEL), row), pl.BlockSpec((tm, 128), row)],
        out_shape=[jax.ShapeDtypeStruct((m, D_MODEL), BF16), jax.ShapeDtypeStruct((m, 128), F32)],
        compiler_params=_params("parallel"),
        name="mm_qg",
    )(x, w)


def _mm_plain_kernel(x_ref, w_ref, o_ref):
    o_ref[...] = _dot(x_ref[...], w_ref[...])


def _mm_plain(x, w):
    return pl.pallas_call(
        _mm_plain_kernel,
        out_shape=jax.ShapeDtypeStruct((x.shape[0], w.shape[1]), F32),
        compiler_params=pltpu.CompilerParams(vmem_limit_bytes=VMEM_LIMIT),
        name="mm_plain",
    )(x, w)


CONV_HIST = CONV_W - 1
CONV_HEAD = 32
CONV_CB = 512
SUBLANES = 8
LANE_TILE = 128


def _conv_kernel(*refs, tm, has_prev):
    if has_prev:
        state_ref, uprev_ref, ucur_ref, w_ref, bdw_ref, g_ref, b_ref, o_ref, z_ref, zs_ref, y_ref = refs
    else:
        state_ref, ucur_ref, w_ref, bdw_ref, g_ref, b_ref, o_ref, z_ref, zs_ref, y_ref = refs
    i = pl.program_id(1)
    pad = CONV_HEAD - CONV_HIST
    rows = CONV_HEAD + tm
    total = rows + SUBLANES

    @pl.when(i == 0)
    def _():
        z_ref[0:pad, :] = jnp.zeros((pad, D_MODEL), F32)
        z_ref[pad:CONV_HEAD, :] = state_ref[0]

    if has_prev:
        @pl.when(i > 0)
        def _():
            z_ref[0:CONV_HEAD, :] = uprev_ref[0]

    z_ref[CONV_HEAD:rows, :] = ucur_ref[0]
    z_ref[rows:total, :] = jnp.zeros((SUBLANES, D_MODEL), F32)
    for c0 in range(0, D_MODEL, CONV_CB):
        zc = z_ref[:, c0:c0 + CONV_CB]
        zs_ref[0, :, c0:c0 + CONV_CB] = zc[0:rows]
        for b in range(1, SUBLANES):
            zs_ref[b, :, c0:c0 + CONV_CB] = pltpu.roll(zc, total - b, 0)[0:rows]
    for c0 in range(0, D_MODEL, LANE_TILE):
        cols = slice(c0, c0 + LANE_TILE)
        taps = [w_ref[w * SUBLANES:(w + 1) * SUBLANES, cols] for w in range(CONV_W)]
        for r0 in range(0, tm, SUBLANES):
            acc = None
            for w in range(CONV_W):
                shift, base = (pad + w) % SUBLANES, (pad + w) // SUBLANES * SUBLANES
                term = zs_ref[shift, r0 + base:r0 + base + SUBLANES, cols] * taps[w]
                acc = term if acc is None else acc + term
            y_ref[r0:r0 + SUBLANES, cols] = acc
    y = _layer_norm(y_ref[...] + bdw_ref[...], g_ref[...], b_ref[...])
    o_ref[0] = _silu(y).astype(BF16)


def _conv_ln_silu(state, u, w_dw, b_dw, g, b, *, tm):
    bsz, t, d = u.shape
    has_prev = t > tm
    per = tm // CONV_HEAD
    vec = lambda bb, i: (0, 0)
    prev = [pl.BlockSpec((1, CONV_HEAD, d), lambda bb, i: (bb, jnp.maximum(i * per - 1, 0), 0))] if has_prev else []
    rows = CONV_HEAD + tm
    return pl.pallas_call(
        functools.partial(_conv_kernel, tm=tm, has_prev=has_prev),
        grid=(bsz, t // tm),
        in_specs=[pl.BlockSpec((None, 1, CONV_HIST, d), lambda bb, i: (0, bb, 0, 0))] + prev + [
            pl.BlockSpec((1, tm, d), lambda bb, i: (bb, i, 0)),
            pl.BlockSpec((CONV_W * SUBLANES, d), vec),
            pl.BlockSpec((1, d), vec),
            pl.BlockSpec((1, d), vec),
            pl.BlockSpec((1, d), vec),
        ],
        out_specs=pl.BlockSpec((1, tm, d), lambda bb, i: (bb, i, 0)),
        out_shape=jax.ShapeDtypeStruct((bsz, t, d), BF16),
        scratch_shapes=[pltpu.VMEM((rows + SUBLANES, d), F32), pltpu.VMEM((SUBLANES, rows, d), F32),
                        pltpu.VMEM((tm, d), F32)],
        compiler_params=_params("parallel", "arbitrary"),
        name="conv_ln_silu",
    )(state, *([u] if has_prev else []), u, w_dw, b_dw, g, b)


CHUNK_ROWS = STRIDE * N_KV
CHUNK_TILES = CHUNK_ROWS // SUBLANES
TOK_PER_TILE = SUBLANES // N_KV


def _compress_kernel(*refs, n_in, ch_pp, n_pools):
    nch = n_in * ch_pp
    rows = nch * N_KV
    pair_rows = ch_pp // TOK_PER_TILE * SUBLANES
    low = lax.broadcasted_iota(jnp.int32, (n_in * pair_rows, HEAD_DIM), 0) % SUBLANES < N_KV
    for pool in range(n_pools):
        x_refs = refs[1 + pool * n_in:1 + (pool + 1) * n_in]
        w1k_ref, bias1_ref, w2_ref, b2_ref = refs[1 + n_pools * n_in + 4 * pool:1 + n_pools * n_in + 4 * (pool + 1)]
        o_ref = refs[1 + n_pools * (n_in + 4) + pool]
        pages = [r[0].reshape(ch_pp // TOK_PER_TILE, TOK_PER_TILE, CHUNK_TILES, SUBLANES, HEAD_DIM) for r in x_refs]
        first, second = [], []
        for k in range(CHUNK_TILES):
            a = jnp.concatenate([p[:, 0, k].reshape(pair_rows, HEAD_DIM) for p in pages], axis=0)
            b = jnp.concatenate([p[:, 1, k].reshape(pair_rows, HEAD_DIM) for p in pages], axis=0)
            first.append(jnp.where(low, a, pltpu.roll(b, N_KV, 0)).astype(BF16))
            second.append(jnp.where(low, pltpu.roll(a, a.shape[0] - N_KV, 0), b).astype(BF16))
        acc = (_dot(jnp.concatenate(first, axis=1), w1k_ref[0])
               + _dot(jnp.concatenate(second, axis=1), w1k_ref[1]))
        lo = acc[:, :CMP_HID]
        hi = acc[:, CMP_HID:]
        hi_next = pltpu.roll(hi, rows - N_KV, 0)
        h = _gelu_tanh(lo + hi_next + bias1_ref[...])
        o_ref[0] = _dot(h.astype(BF16), w2_ref[...]) + b2_ref[...]


def _compress(pools, page_table, ops, *, pages_per_row, ch_per_page):
    bsz = page_table.shape[0]
    nch = pages_per_row * ch_per_page
    const2 = lambda bb, pt: (0, 0)
    page_rows = ch_per_page * CHUNK_ROWS
    page_specs = [pl.BlockSpec((1, page_rows, HEAD_DIM), functools.partial(lambda bb, pt, i: (pt[bb, i], 0, 0), i=i))
                  for i in range(pages_per_row)]
    op_specs = [
        pl.BlockSpec(ops[0][0].shape, lambda bb, pt: (0, 0, 0)),
        pl.BlockSpec((1, CMP_HID), const2),
        pl.BlockSpec((CMP_HID, HEAD_DIM), const2),
        pl.BlockSpec((1, HEAD_DIM), const2),
    ]
    n_pools = len(pools)
    page_args = [p for pool in pools for p in [pool] * pages_per_row]
    return pl.pallas_call(
        functools.partial(_compress_kernel, n_in=pages_per_row, ch_pp=ch_per_page, n_pools=n_pools),
        grid_spec=pltpu.PrefetchScalarGridSpec(
            num_scalar_prefetch=1,
            grid=(bsz,),
            in_specs=page_specs * n_pools + op_specs * n_pools,
            out_specs=[pl.BlockSpec((1, nch * N_KV, HEAD_DIM), lambda bb, pt: (bb, 0, 0))] * n_pools,
        ),
        out_shape=[jax.ShapeDtypeStruct((bsz, nch * N_KV, HEAD_DIM), F32)] * n_pools,
        compiler_params=_params("parallel"),
        name="compress",
    )(page_table, *page_args, *[a for op in ops for a in op])


def _compress_weights(w1):
    w1b = w1.astype(BF16).reshape(2, STRIDE, HEAD_DIM, CMP_HID)
    per_tok = jnp.concatenate([w1b[0], w1b[1]], axis=-1)
    by_parity = per_tok.reshape(CHUNK_TILES, TOK_PER_TILE, HEAD_DIM, 2 * CMP_HID).transpose(1, 0, 2, 3)
    return by_parity.reshape(TOK_PER_TILE, CHUNK_TILES * HEAD_DIM, 2 * CMP_HID)


def _select_topk(score, n_sel):
    jn = score.shape[0]
    sub = lax.broadcasted_iota(jnp.int32, (8, score.shape[1]), 0)
    blks = [score[8 * v:8 * v + 8, :] for v in range(jn // 8)]
    cnt = [jnp.zeros(b.shape, F32) for b in blks]
    for jp in range(jn):
        row = score[jp:jp + 1, :]
        for v, blk in enumerate(blks):
            gt = jnp.where(row > blk, 1.0, 0.0)
            ge = jnp.where(row >= blk, 1.0, 0.0)
            if 8 * v + 8 <= jp:
                beats = gt
            elif 8 * v > jp:
                beats = ge
            else:
                beats = jnp.where(sub > jp % 8, ge, gt)
            cnt[v] = cnt[v] + beats
    return jnp.concatenate(cnt, axis=0) < n_sel


def _col_max(x):
    return jnp.max(x, axis=0, keepdims=True)


def _col_sum(x):
    return jnp.sum(x, axis=0, keepdims=True)


QB_PER_STEP = 2


def _attn_prompt_kernel(*refs, n_sel, nc_pad, strip_zero, nwin):
    (q_ref, graw_ref, kc_ref, vct_ref, kaug_ref, vst1_ref, vst4_ref, kwp_ref, vwt_ref,
     strip_ref, t0_ref, t1_ref) = refs[:12]
    wadd_refs = refs[12:12 + QB_PER_STEP]
    c2st_ref, o_ref = refs[12 + QB_PER_STEP:]
    per = Q_BLOCK // L_SLC
    chunk_tiles = FAR_CHUNK // Q_BLOCK
    max_far = kaug_ref.shape[2] // FAR_CHUNK

    def front(u):
        qb = pl.program_id(2) * QB_PER_STEP + u
        q = q_ref[u * Q_BLOCK:(u + 1) * Q_BLOCK, :]
        qs = jnp.concatenate([q[:, h * HEAD_DIM:(h + 1) * HEAD_DIM] for h in range(HG)], axis=0)

        kw = kwp_ref[0, 0, pl.ds(pl.multiple_of(qb * Q_BLOCK, Q_BLOCK), (nwin + 1) * Q_BLOCK), :]
        sw = _dot_nt(kw, qs) + wadd_refs[u][0, 0]
        pw = jnp.exp2(sw - _col_max(sw))
        acc_w = _dot(vwt_ref[0, 0, qb], pw[0:Q_BLOCK, :].astype(BF16))
        for dd in range(1, nwin + 1):
            acc_w = acc_w + _dot(vwt_ref[0, 0, qb + dd], pw[dd * Q_BLOCK:(dd + 1) * Q_BLOCK, :].astype(BF16))
        o_win = acc_w * (1.0 / jnp.maximum(_col_sum(pw), 1e-30))

        off = pl.multiple_of(strip_zero - (Q_BLOCK // STRIDE) * qb, 8)
        st = _dot_nt(kc_ref[0], qs) + strip_ref[0, pl.ds(off, nc_pad), :]
        m = _col_max(st)
        m = jnp.where(m < 0.1 * NEG, 0.0, m)
        e = jnp.exp2(st - m)
        p = e * (1.0 / jnp.maximum(_col_sum(e), 1e-30))
        o_cmp = _dot(vct_ref[0], p.astype(BF16))
        psum = p[:, 0:Q_BLOCK]
        for h in range(1, HG):
            psum = psum + p[:, h * Q_BLOCK:(h + 1) * Q_BLOCK]
        p_hi = psum.astype(BF16)
        p_lo = (psum - p_hi.astype(F32)).astype(BF16)
        imp = _dot(c2st_ref[...], p_hi) + _dot(c2st_ref[...], p_lo)

        j = lax.broadcasted_iota(jnp.int32, imp.shape, 0)
        cur = (qb * Q_BLOCK + lax.broadcasted_iota(jnp.int32, imp.shape, 1)) // L_SLC
        forced = (j == 0) | (j == cur) | (j == cur - 1)
        score = jnp.where(j <= cur, imp + jnp.where(forced, FORCE_BONUS, 0.0), NEG)
        sel = _select_topk(score, n_sel)

        def with_mask(keep):
            mk = _pad_rows(jnp.where(keep, 0.0, NEG), Q_BLOCK).T
            return jnp.concatenate([qs, jnp.concatenate([mk] * HG, axis=0).astype(BF16)], axis=1)

        qa_near = with_mask(sel)
        qa_far = with_mask(sel & (j < per * (qb - 1)))
        kb1 = jnp.maximum(qb - 1, 0)
        s1 = _dot_nt(kaug_ref[0, 0, pl.ds(pl.multiple_of(kb1 * Q_BLOCK, Q_BLOCK), Q_BLOCK), :], qa_near)
        s1 = s1 + jnp.where(qb >= 1, t1_ref[0], NEG)
        s0 = _dot_nt(kaug_ref[0, 0, pl.ds(pl.multiple_of(qb * Q_BLOCK, Q_BLOCK), Q_BLOCK), :], qa_near) + t0_ref[0]
        return dict(qb=qb, kb1=kb1, o_win=o_win, o_cmp=o_cmp, qa_far=qa_far, s1=s1, s0=s0)

    def far(f):
        def variant(n_chunks):
            def run(mx):
                scores = [_dot_nt(kaug_ref[0, 0, c * FAR_CHUNK:(c + 1) * FAR_CHUNK, :], f["qa_far"])
                          for c in range(n_chunks)]
                for s in scores:
                    mx = jnp.maximum(mx, _col_max(s))
                l = jnp.zeros((1, LANES), F32)
                acc = jnp.zeros((HEAD_DIM, LANES), F32)
                for c, s in enumerate(scores):
                    pc = jnp.exp2(s - mx)
                    l = l + _col_sum(pc)
                    acc = acc + _dot(vst4_ref[0, 0, c], pc.astype(BF16))
                return mx, l, acc
            return run

        mx_near = jnp.maximum(_col_max(f["s1"]), _col_max(f["s0"]))
        n_far = (jnp.maximum(f["qb"] - 1, 0) + chunk_tiles - 1) // chunk_tiles
        return lax.switch(n_far, [variant(n) for n in range(max_far + 1)], mx_near)

    def back(u, f, mx, l_s, acc_s):
        p1 = jnp.exp2(f["s1"] - mx)
        p0 = jnp.exp2(f["s0"] - mx)
        l_s = l_s + _col_sum(p1) + _col_sum(p0)
        acc_s = (acc_s + _dot(vst1_ref[0, 0, f["kb1"]], p1.astype(BF16))
                 + _dot(vst1_ref[0, 0, f["qb"]], p0.astype(BF16)))
        o_slc = acc_s * (1.0 / jnp.maximum(l_s, 1e-30))
        gates = jax.nn.sigmoid(graw_ref[0, 0, :, :, u * Q_BLOCK:(u + 1) * Q_BLOCK])
        gl = [jnp.concatenate([gates[i, h:h + 1, :] for h in range(HG)], axis=1) for i in range(3)]
        o_t = gl[0] * f["o_cmp"] + gl[1] * o_slc + gl[2] * f["o_win"]
        for h in range(HG):
            o_ref[u * Q_BLOCK:(u + 1) * Q_BLOCK, h * HEAD_DIM:(h + 1) * HEAD_DIM] = (
                o_t[:, h * Q_BLOCK:(h + 1) * Q_BLOCK].T.astype(BF16))

    fronts = [front(u) for u in range(QB_PER_STEP)]
    fars = [far(f) for f in fronts]
    for u in range(QB_PER_STEP):
        back(u, fronts[u], *fars[u])


def _attn_prompt(q, graw_t, kc, vct, kaug, vst1, vst4, kwp, vwt, strip, t0, t1, wadd, c2st, *, bsz, t):
    nq = t // Q_BLOCK
    steps = nq // QB_PER_STEP
    rows = QB_PER_STEP * Q_BLOCK
    nc_pad = kc.shape[1]
    ns = c2st.shape[0]
    nwin = WINDOW // Q_BLOCK
    per_bg = lambda arr: pl.BlockSpec((1, 1) + arr.shape[2:], lambda b, g, i: (g, b) + (0,) * (arr.ndim - 2))
    per_g = lambda arr: pl.BlockSpec((1,) + arr.shape[1:], lambda b, g, i: (g,) + (0,) * (arr.ndim - 1))
    wadd_specs = [pl.BlockSpec((1, 1) + wadd.shape[2:],
                               functools.partial(lambda b, g, i, u: (g, jnp.minimum(i * QB_PER_STEP + u, nwin), 0, 0), u=u))
                  for u in range(QB_PER_STEP)]
    return pl.pallas_call(
        functools.partial(_attn_prompt_kernel, n_sel=min(N_SEL, ns), nc_pad=nc_pad,
                          strip_zero=strip.shape[1] - nc_pad, nwin=nwin),
        grid=(bsz, N_KV, steps),
        in_specs=[
            pl.BlockSpec((rows, HG * HEAD_DIM), lambda b, g, i: (b * steps + i, g)),
            pl.BlockSpec((1, 1, 3, HG, rows), lambda b, g, i: (b, g, 0, 0, i)),
            pl.BlockSpec((1, nc_pad, HEAD_DIM), lambda b, g, i: (b, 0, g)),
            pl.BlockSpec((1, HEAD_DIM, nc_pad), lambda b, g, i: (b, g, 0)),
            per_bg(kaug), per_bg(vst1), per_bg(vst4), per_bg(kwp), per_bg(vwt),
            per_g(strip), per_g(t0), per_g(t1),
        ] + wadd_specs + [pl.BlockSpec(c2st.shape, lambda b, g, i: (0, 0))],
        out_specs=pl.BlockSpec((rows, HG * HEAD_DIM), lambda b, g, i: (b * steps + i, g)),
        out_shape=jax.ShapeDtypeStruct((bsz * t, D_MODEL), BF16),
        compiler_params=_params("parallel", "parallel", "arbitrary"),
        name="attn_prompt",
    )(q, graw_t, kc, vct, kaug, vst1, vst4, kwp, vwt, strip, t0, t1, *([wadd] * QB_PER_STEP), c2st)


def _softmax_rows(s):
    m = jnp.max(s, axis=1, keepdims=True)
    m = jnp.where(m < 0.1 * NEG, 0.0, m)
    e = jnp.exp2(s - m)
    return e * (1.0 / jnp.maximum(jnp.sum(e, axis=1, keepdims=True), 1e-30))


def _attn_sample_kernel(*refs, n_pages, tq, n_sel, ns, n_state):
    kpages = refs[1:1 + n_pages]
    vpages = refs[1 + n_pages:1 + 2 * n_pages]
    (q_ref, g_ref, kc_ref, vc_ref, ksn_ref, vsn_ref, kwn_ref, vwn_ref, kwl_ref, vwl_ref, kst_ref, vst_ref,
     bc_ref, bs_ref, bw_ref, c2st_ref, e_ref,
     o_ref, okw_ref, ovw_ref, kbuf, vbuf, wkbuf, wvbuf) = refs[1 + 2 * n_pages:]
    n_cache = n_pages * PAGE_SIZE
    tail = kbuf.shape[0] - n_cache
    wtail = wkbuf.shape[0] - n_state

    def group_rows(ref, g, n):
        return ref[0, pl.ds(g, n, stride=N_KV), :].astype(BF16)

    def new_rows(ref, n):
        return _pad_rows(ref[0].astype(F32), n).astype(BF16)

    for i in range(n_pages):
        for g in range(N_KV):
            cols = slice(g * HEAD_DIM, (g + 1) * HEAD_DIM)
            kbuf[i * PAGE_SIZE:(i + 1) * PAGE_SIZE, cols] = group_rows(kpages[i], g, PAGE_SIZE)
            vbuf[i * PAGE_SIZE:(i + 1) * PAGE_SIZE, cols] = group_rows(vpages[i], g, PAGE_SIZE)
    kbuf[n_cache:, :] = new_rows(ksn_ref, tail)
    vbuf[n_cache:, :] = new_rows(vsn_ref, tail)
    for g in range(N_KV):
        cols = slice(g * HEAD_DIM, (g + 1) * HEAD_DIM)
        wkbuf[0:n_state, cols] = group_rows(kst_ref, g, n_state)
        wvbuf[0:n_state, cols] = group_rows(vst_ref, g, n_state)
    wkbuf[n_state:, :] = new_rows(kwn_ref, wtail)
    wvbuf[n_state:, :] = new_rows(vwn_ref, wtail)
    keep = okw_ref.shape[1] - tq * N_KV
    okw_ref[0, 0:keep, :] = kst_ref[0, n_state * N_KV - keep:, :]
    okw_ref[0, keep:, :] = kwl_ref[0]
    ovw_ref[0, 0:keep, :] = vst_ref[0, n_state * N_KV - keep:, :]
    ovw_ref[0, keep:, :] = vwl_ref[0]

    q = q_ref[0].astype(F32)
    zero = jnp.zeros((tq, HEAD_DIM), F32)
    qrows = []
    for g in range(N_KV):
        for h in range(HG):
            piece = q[:, (g * HG + h) * HEAD_DIM:(g * HG + h + 1) * HEAD_DIM]
            qrows.append(jnp.concatenate([piece if gg == g else zero for gg in range(N_KV)], axis=1))
    qbd = jnp.concatenate(qrows, axis=0).astype(BF16)

    n_cmp = kc_ref.shape[1] // N_KV
    kc = jnp.concatenate([group_rows(kc_ref, g, n_cmp) for g in range(N_KV)], axis=1)
    vc = jnp.concatenate([group_rows(vc_ref, g, n_cmp) for g in range(N_KV)], axis=1)
    p_c = _softmax_rows(_dot_nt(qbd, kc) + bc_ref[...])
    o_cmp = _dot(p_c.astype(BF16), vc)
    reps = []
    for g in range(N_KV):
        s = p_c[g * HG * tq:g * HG * tq + tq, :]
        for h in range(1, HG):
            s = s + p_c[(g * HG + h) * tq:(g * HG + h + 1) * tq, :]
        reps.extend([s] * HG)
    psum = jnp.concatenate(reps, axis=0)
    p_hi = psum.astype(BF16)
    p_lo = (psum - p_hi.astype(F32)).astype(BF16)
    imp_t = _dot_nt(c2st_ref[...], p_hi) + _dot_nt(c2st_ref[...], p_lo)

    nsp = -(-ns // 8) * 8
    imp_t = imp_t[0:nsp, :]
    j = lax.broadcasted_iota(jnp.int32, imp_t.shape, 0)
    tt = lax.broadcasted_iota(jnp.int32, imp_t.shape, 1) % tq
    cur = (PAST_LEN + tt) // L_SLC
    forced = (j == 0) | (j == cur) | (j == cur - 1)
    score = jnp.where(j <= cur, imp_t + jnp.where(forced, FORCE_BONUS, 0.0), NEG)
    score = jnp.where(j < ns, score, 2.0 * NEG)
    sel_t = jnp.where(_select_topk(score, n_sel), 1.0, 0.0)
    sel = _pad_rows(sel_t, e_ref.shape[0]).T.astype(BF16)
    mask = _dot(sel, e_ref[...])

    s_s = _dot_nt(qbd, kbuf[...]) + bs_ref[...] + jnp.where(mask > 0.5, 0.0, NEG)
    o_slc = _dot(_softmax_rows(s_s).astype(BF16), vbuf[...])

    s_w = _dot_nt(qbd, wkbuf[...]) + bw_ref[...]
    o_win = _dot(_softmax_rows(s_w).astype(BF16), wvbuf[...])

    gates = jax.nn.sigmoid(g_ref[0])
    o_full = gates[:, 0:1] * o_cmp + gates[:, 1:2] * o_slc + gates[:, 2:3] * o_win
    pieces = []
    for g in range(N_KV):
        for h in range(HG):
            r0 = (g * HG + h) * tq
            pieces.append(o_full[r0:r0 + tq, g * HEAD_DIM:(g + 1) * HEAD_DIM])
    o_ref[0] = jnp.concatenate(pieces, axis=1).astype(BF16)


def _attn_sample(page_table, pool_k, pool_v, q3, gcol, kc, vc, ksn, vsn, kwn, vwn, kwl, vwl, kst, vst,
                 bc, bs, bw, c2st, expand, *, ns):
    bsz, n_pages = page_table.shape
    tq = q3.shape[1]
    n_state = kst.shape[1] // N_KV
    n_keys = bs.shape[1]
    n_wkeys = bw.shape[1]
    rows = N_HEADS * tq
    wb_new = min(WINDOW, n_state + tq)
    page_specs = [pl.BlockSpec((1, PAGE_SIZE * N_KV, HEAD_DIM),
                               functools.partial(lambda b, pt, i: (pt[b, i], 0, 0), i=i)) for i in range(n_pages)]
    per_b = lambda shape: pl.BlockSpec((1,) + shape, lambda b, pt: (b, 0, 0))
    const = lambda arr: pl.BlockSpec(arr.shape, lambda b, pt: (0, 0))
    in_specs = page_specs + page_specs + [
        per_b((tq, D_MODEL)), per_b((rows, 3)),
        per_b(kc.shape[1:]), per_b(vc.shape[1:]),
        per_b((tq, KV_W)), per_b((tq, KV_W)), per_b((tq, KV_W)), per_b((tq, KV_W)),
        per_b((tq * N_KV, HEAD_DIM)), per_b((tq * N_KV, HEAD_DIM)),
        per_b((n_state * N_KV, HEAD_DIM)), per_b((n_state * N_KV, HEAD_DIM)),
        const(bc), const(bs), const(bw), const(c2st), const(expand),
    ]
    return pl.pallas_call(
        functools.partial(_attn_sample_kernel, n_pages=n_pages, tq=tq, n_sel=min(N_SEL, ns), ns=ns, n_state=n_state),
        grid_spec=pltpu.PrefetchScalarGridSpec(
            num_scalar_prefetch=1,
            grid=(bsz,),
            in_specs=in_specs,
            out_specs=[per_b((tq, D_MODEL)), per_b((wb_new * N_KV, HEAD_DIM)), per_b((wb_new * N_KV, HEAD_DIM))],
            scratch_shapes=[
                pltpu.VMEM((n_keys, KV_W), BF16), pltpu.VMEM((n_keys, KV_W), BF16),
                pltpu.VMEM((n_wkeys, KV_W), BF16), pltpu.VMEM((n_wkeys, KV_W), BF16),
            ],
        ),
        out_shape=[
            jax.ShapeDtypeStruct((bsz, tq, D_MODEL), BF16),
            jax.ShapeDtypeStruct((bsz, wb_new * N_KV, HEAD_DIM), F32),
            jax.ShapeDtypeStruct((bsz, wb_new * N_KV, HEAD_DIM), F32),
        ],
        compiler_params=_params("parallel"),
        name="attn_sample",
    )(page_table, *([pool_k] * n_pages), *([pool_v] * n_pages), q3, gcol, kc, vc, ksn, vsn, kwn, vwn, kwl, vwl,
      kst, vst, bc, bs, bw, c2st, expand)


def _t5_bucket(dist):
    n = jnp.maximum(dist, 0)
    max_exact = N_BUCKETS // 2
    nf = jnp.maximum(n, 1).astype(F32)
    large = max_exact + (jnp.log(nf / max_exact) / math.log(MAX_DIST / max_exact) * (N_BUCKETS - max_exact)).astype(jnp.int32)
    large = jnp.minimum(large, N_BUCKETS - 1)
    return jnp.where(n < max_exact, n, large)


def _bias_by_dist(rel_bias):
    return rel_bias.astype(F32)[_t5_bucket(jnp.arange(MAX_DIST + 1))]


def _bias_tile(btab, dist, valid, shift):
    onehot = (jnp.clip(dist, 0, MAX_DIST)[..., None] == jnp.arange(MAX_DIST + 1)).astype(F32)
    vals = jnp.einsum("...d,dh->h...", onehot, (btab - shift[None, :]) * LOG2E, precision=lax.Precision.HIGHEST)
    return jnp.where(valid[None], vals, NEG)


def _heads_to_lanes(tile):
    h, k, q = tile.shape
    return tile.reshape(N_KV, HG, k, q).transpose(0, 2, 1, 3).reshape(N_KV, k, HG * q)


def _cmp_to_slc_t(nc_pad, nc, ns, ns_pad):
    cs = jnp.arange(nc_pad)[None, :] * STRIDE
    js = jnp.arange(ns_pad)[:, None] * L_SLC
    ov = jnp.clip(jnp.minimum(cs + L_CMP, js + L_SLC) - jnp.maximum(cs, js), 0, None)
    ok = (jnp.arange(nc_pad)[None, :] < nc) & (jnp.arange(ns_pad)[:, None] < ns)
    return jnp.where(ok, ov.astype(F32) / L_CMP, 0.0).astype(BF16)


def _prompt_attention(q, graw, kc, vc, ksb, vsb, kwb, vwb, btab, *, bsz, t):
    nq = t // Q_BLOCK
    nch = t // STRIDE
    nc = (t - L_CMP) // STRIDE + 1
    ns = -(-t // L_SLC)
    nwin = WINDOW // Q_BLOCK
    far = btab[MAX_DIST]
    none = jnp.zeros_like(far)
    graw_t = graw[:, :3 * N_HEADS].reshape(bsz, t, N_KV, HG, 3).transpose(0, 2, 4, 3, 1)

    def transposed_tiles(a, tile):
        return a.reshape(N_KV, bsz, a.shape[2] // tile, tile, HEAD_DIM).transpose(0, 1, 2, 4, 3)

    ks, vs, kw, vw = (a.reshape(N_KV, bsz, t, HEAD_DIM) for a in (ksb, vsb, kwb, vwb))
    block_onehot = (jnp.arange(t)[:, None] // L_SLC == jnp.arange(128)[None, :]).astype(BF16)
    kaug = jnp.concatenate([ks, jnp.broadcast_to(block_onehot, (N_KV, bsz, t, 128))], axis=-1)
    t_far = -(-t // FAR_CHUNK) * FAR_CHUNK
    kaug = jnp.pad(kaug, ((0, 0), (0, 0), (0, t_far - t), (0, 0)))
    vst1 = transposed_tiles(vs, Q_BLOCK)
    vst4 = transposed_tiles(jnp.pad(vs, ((0, 0), (0, 0), (0, t_far - t), (0, 0))), FAR_CHUNK)
    front = ((0, 0), (0, 0), (WINDOW, 0), (0, 0))
    kwp = jnp.pad(kw, front)
    vwt = transposed_tiles(jnp.pad(vw, front), Q_BLOCK)
    vct = jnp.swapaxes(vc, 1, 2)

    kk = jnp.arange(Q_BLOCK)[:, None]
    qq = jnp.arange(Q_BLOCK)[None, :]
    always = jnp.ones((Q_BLOCK, Q_BLOCK), bool)
    t0 = _heads_to_lanes(_bias_tile(btab, qq - kk, qq - kk >= 0, far))
    t1 = _heads_to_lanes(_bias_tile(btab, Q_BLOCK + qq - kk, always, far))
    oldest = jnp.where(kk > qq, 0.0, NEG).astype(F32)
    wadd = jnp.concatenate([jnp.broadcast_to(jnp.tile(oldest, (1, HG)), (N_KV, Q_BLOCK, LANES)),
                            jnp.zeros((N_KV, (nwin - 2) * Q_BLOCK, LANES), F32), t1, t0], axis=1)
    tile_of_row = jnp.arange((nwin + 1) * Q_BLOCK) // Q_BLOCK
    is_padding = tile_of_row[None, :] < nwin - jnp.arange(nwin + 1)[:, None]
    wadd = jnp.where(is_padding[None, :, :, None], NEG, wadd[:, None])
    nshift = (Q_BLOCK // STRIDE) * (nq - 1)
    lo = min(nshift, Q_BLOCK // STRIDE + MAX_DIST // STRIDE)
    cprime = jnp.arange(-lo, Q_BLOCK // STRIDE)[:, None]
    dist_c = qq - STRIDE * cprime - (L_CMP - 1)
    pattern = _heads_to_lanes(_bias_tile(btab, dist_c, dist_c >= 0, none))
    far_rows = jnp.broadcast_to(jnp.repeat(far * LOG2E, Q_BLOCK).reshape(N_KV, 1, LANES), (N_KV, nshift - lo, LANES))
    strip = jnp.concatenate([far_rows, pattern, jnp.full((N_KV, nch - Q_BLOCK // STRIDE, LANES), NEG, F32)], axis=1)
    c2st = _cmp_to_slc_t(nch, nc, ns, -(-ns // 8) * 8)
    return _attn_prompt(q, graw_t, kc, vct, kaug, vst1, vst4, kwp, vwt, strip, t0, t1, wadd, c2st, bsz=bsz, t=t)


def _sample_attention(q, graw, kc, vc, new_bf16, new_rows, page_table, pool_k, pool_v, state_k, state_v, btab,
                      *, dbsz, dt):
    d = D_MODEL
    n_pages = page_table.shape[1]
    n_pool = pool_k.shape[0]
    n_state = state_k.shape[1]
    nch = n_pages * (PAGE_SIZE // STRIDE)
    t_all = PAST_LEN + dt
    nc = (t_all - L_CMP) // STRIDE + 1
    ns = -(-t_all // L_SLC)
    n_keys = -(-(ns * L_SLC) // 128) * 128
    n_wkeys = -(-(n_state + dt) // 128) * 128
    far = btab[MAX_DIST]
    none = jnp.zeros_like(far)
    gcol = graw[:, :3 * N_HEADS].reshape(dbsz, dt, N_HEADS, 3).transpose(0, 2, 1, 3).reshape(dbsz, N_HEADS * dt, 3)

    qpos = (PAST_LEN + jnp.arange(dt))[:, None]
    cc = jnp.arange(nch)[None, :]
    dist = qpos - (cc * STRIDE + L_CMP - 1)
    bc = _bias_tile(btab, dist, (dist >= 0) & (cc < nc), none).reshape(N_HEADS * dt, nch)
    near0 = (PAST_LEN - MAX_DIST) // 128 * 128
    kpos = jnp.arange(near0, n_keys)[None, :]
    dist = qpos - kpos
    bs = jnp.concatenate([jnp.zeros((N_HEADS * dt, near0), F32),
                          _bias_tile(btab, dist, dist >= 0, far).reshape(N_HEADS * dt, n_keys - near0)], axis=1)
    wi = jnp.arange(n_wkeys)[None, :]
    dist = qpos - (PAST_LEN - n_state + wi)
    bw = _bias_tile(btab, dist, (dist >= 0) & (dist < WINDOW) & (wi < n_state + dt), none).reshape(N_HEADS * dt, n_wkeys)
    c2st = _cmp_to_slc_t(nch, nc, ns, 128)
    expand = (jnp.arange(128)[:, None] == (jnp.arange(n_keys)[None, :] // L_SLC)).astype(BF16)

    ksn, vsn, kwn, vwn = (a.transpose(1, 0, 2).reshape(dbsz, dt, KV_W) for a in new_bf16)
    kwl, vwl = (a.reshape(dbsz, dt * N_KV, HEAD_DIM) for a in new_rows)
    o, sk_win, sv_win = _attn_sample(
        page_table, pool_k.reshape(n_pool, PAGE_SIZE * N_KV, HEAD_DIM), pool_v.reshape(n_pool, PAGE_SIZE * N_KV, HEAD_DIM),
        q.reshape(dbsz, dt, d), gcol, kc, vc, ksn, vsn, kwn, vwn, kwl, vwl,
        state_k.reshape(dbsz, n_state * N_KV, HEAD_DIM), state_v.reshape(dbsz, n_state * N_KV, HEAD_DIM),
        bc, bs, bw, c2st, expand, ns=ns)
    wb = sk_win.shape[1] // N_KV
    return (o.reshape(dbsz * dt, d), sk_win.reshape(dbsz, wb, N_KV, HEAD_DIM), sv_win.reshape(dbsz, wb, N_KV, HEAD_DIM))


def _row(v):
    return v.reshape(1, -1).astype(F32)


def kernel(x_prompt, x_sample, cache_k_cmp, cache_v_cmp, cache_k_slc, cache_v_slc, state_k_win, state_v_win, state_conv, page_table, rel_bias, conv_w_pw1, conv_b_pw1, conv_w_dw, conv_b_dw, conv_ln_g, conv_ln_b, conv_w_pw2, conv_b_pw2, nsa_w_qg, nsa_w_o, w_kv, cmp_pe, cmp_w1, cmp_b1, cmp_w2, cmp_b2, ffn_w_in, ffn_w_out, ln_mix_g, ln_mix_b, ln_ffn_g, ln_ffn_b):
    d = D_MODEL
    d_ff = ffn_w_out.shape[1]
    bsz, t, _ = x_prompt.shape
    dbsz, dt, _ = x_sample.shape
    zeros_d = jnp.zeros((1, d), F32)
    zeros_ff = jnp.zeros((1, 2 * d_ff), F32)

    w_pw2 = conv_w_pw2.astype(BF16)
    w_out = ffn_w_out.astype(BF16)
    w_kvb = w_kv.astype(BF16)
    w_qg = jnp.pad(nsa_w_qg[0], ((0, 0), (0, d + 128 - nsa_w_qg.shape[2]))).astype(BF16)
    w_o = nsa_w_o.astype(BF16)
    w_dw = jnp.repeat(conv_w_dw[0], SUBLANES, axis=0)

    cmp_ops = []
    for i in range(2):
        pe8 = jnp.broadcast_to(cmp_pe[i].reshape(1, -1), (8, L_CMP * HEAD_DIM)).astype(BF16)
        pe_w1 = _mm_plain(pe8, cmp_w1[i].astype(BF16))[0:1]
        cmp_ops.append((_compress_weights(cmp_w1[i]), pe_w1 + _row(cmp_b1[i]), cmp_w2[i].astype(BF16),
                        _row(cmp_b2[i])))

    btab = _bias_by_dist(rel_bias)

    def ffn(xf, xb, layer):
        m = xf.shape[0]
        h = _mm_pair(xb, ffn_w_in, layer, zeros_ff, act="swiglu", out_dtype=BF16, tm=min(m, 1024), tn=512)
        return _mm_res_ln(h, w_out, layer, zeros_d, xf, _row(ln_ffn_g[layer]), _row(ln_ffn_b[layer]),
                          tm=min(m, 512), tk=d_ff // 4)

    def trunk_front(x, state, tm_conv):
        b_, t_, _ = x.shape
        m = b_ * t_
        u = _mm_pair(x.reshape(m, d), conv_w_pw1, 0, _row(conv_b_pw1[0]), act="glu", out_dtype=F32, tm=min(m, 1024), tn=512)
        c = _conv_ln_silu(state, u.reshape(b_, t_, d), w_dw, _row(conv_b_dw[0]), _row(conv_ln_g[0]),
                          _row(conv_ln_b[0]), tm=tm_conv)
        x1, x1b = _mm_res_ln(c.reshape(m, d), w_pw2, 0, _row(conv_b_pw2[0]), x.reshape(m, d),
                             _row(ln_mix_g[0]), _row(ln_mix_b[0]), tm=min(m, 512), tk=d // 2)
        x2, x2b = ffn(x1, x1b, 0)
        return x2, x2b, u

    def trunk_back(x2, o):
        x3, x3b = _mm_res_ln(o, w_o, 0, zeros_d, x2, _row(ln_mix_g[1]), _row(ln_mix_b[1]), tm=min(x2.shape[0], 512),
                             tk=d // 2)
        y, _ = ffn(x3, x3b, 1)
        return y

    as4 = lambda a, b_, t_: a.reshape(b_, t_, N_KV, HEAD_DIM)

    x2, x2b, u_p = trunk_front(x_prompt, jnp.zeros((1, bsz, CONV_HIST, d), F32), 128)
    kv = _mm_kv(x2b, w_kvb, tm=256)
    kcr, vcr, ksr, vsr, kwr, vwr = kv[:N_KV_OUT]
    ksb, vsb, kwb, vwb = kv[N_KV_OUT:]
    nch_p = t // STRIDE
    ident = jnp.arange(bsz, dtype=jnp.int32).reshape(bsz, 1)
    kc_p, = _compress([kcr.reshape(bsz, t * N_KV, HEAD_DIM)], ident, cmp_ops[0:1], pages_per_row=1, ch_per_page=nch_p)
    vc_p, = _compress([vcr.reshape(bsz, t * N_KV, HEAD_DIM)], ident, cmp_ops[1:2], pages_per_row=1, ch_per_page=nch_p)
    q_p, graw_p = _mm_qg(x2b, w_qg, tm=512)
    natural = lambda a: a.reshape(bsz, nch_p, KV_W).astype(BF16)
    o_p = _prompt_attention(q_p, graw_p, natural(kc_p), natural(vc_p), ksb, vsb, kwb, vwb, btab, bsz=bsz, t=t)
    y_prompt = trunk_back(x2, o_p).reshape(bsz, t, d)

    wb_p = min(WINDOW, t)
    pk_cmp, pv_cmp, pk_slc, pv_slc = (as4(a, bsz, t) for a in (kcr, vcr, ksr, vsr))
    pk_win = as4(kwr, bsz, t)[:, t - wb_p:]
    pv_win = as4(vwr, bsz, t)[:, t - wb_p:]
    p_conv = u_p.reshape(bsz, t, d)[:, t - (CONV_W - 1):][None]

    st0 = state_conv[0]
    s2, s2b, u_s = trunk_front(x_sample, state_conv, dt)
    kv = _mm_kv(s2b, w_kvb, tm=256)
    skc, svc, sks, svs, skw, svw = kv[:N_KV_OUT]
    n_pages = page_table.shape[1]
    n_pool = cache_k_cmp.shape[0]
    ch_pp = PAGE_SIZE // STRIDE
    kc_s, vc_s = _compress([cache_k_cmp.reshape(n_pool, PAGE_SIZE * N_KV, HEAD_DIM),
                            cache_v_cmp.reshape(n_pool, PAGE_SIZE * N_KV, HEAD_DIM)], page_table, cmp_ops,
                           pages_per_row=n_pages, ch_per_page=ch_pp)
    q_s, graw_s = _mm_qg(s2b, w_qg, tm=512)
    o_s, sk_win, sv_win = _sample_attention(q_s, graw_s, kc_s, vc_s, kv[N_KV_OUT:], (skw, svw), page_table,
                                            cache_k_slc, cache_v_slc, state_k_win, state_v_win, btab,
                                            dbsz=dbsz, dt=dt)
    y_sample = trunk_back(s2, o_s).reshape(dbsz, dt, d)

    sk_cmp, sv_cmp, sk_slc, sv_slc = (as4(a, dbsz, dt) for a in (skc, svc, sks, svs))
    s_conv = jnp.concatenate([st0, u_s.reshape(dbsz, dt, d)], axis=1)[:, -(CONV_W - 1):][None]

    return (y_prompt, y_sample, pk_cmp, pv_cmp, pk_slc, pv_slc, pk_win, pv_win, p_conv,
            sk_cmp, sv_cmp, sk_slc, sv_slc, sk_win, sv_win, s_conv)
```
